```python
import math
import jax
import jax.numpy as jnp
from jax import lax
import numpy as np

D_MODEL = 2048
BATCH = 8
SEQ = 4096
DEPTH = 2

CTX_LEN = 256
GRID_W = 64
EPS = 1e-6
N_MOD = 6

DN_HEADS = 4
DN_DK = 128
DN_DV = 128
DN_CONV = 5
DN_CHUNK = 64
DN_CONV_CH = 2 * DN_HEADS * DN_DK + DN_HEADS * DN_DV
GLA_HEADS = 4
GLA_DK = 64
GLA_DV = 128
GLA_RANK = 16
GLA_TAU = 16.0
GLA_CHUNK = 64
MLA_HEADS = 4
MLA_NOPE = 128
MLA_ROPE = 64
MLA_V = 128
MLA_Q_RANK = 448
MLA_KV_RANK = 128
MLA_Q_BLOCK = 128
ROPE_BASE = 10000.0
FN_GROUPS = 4
FN_CH = 128
MIX_W = DN_HEADS * DN_DV + GLA_HEADS * GLA_DV + MLA_HEADS * MLA_V + FN_GROUPS * FN_CH
PEER_HEADS = 8
PEER_N_KEYS = 128
PEER_EXPERTS = PEER_N_KEYS * PEER_N_KEYS
PEER_KEY_DIM = 256
PEER_HALF = PEER_KEY_DIM // 2
PEER_TOPK = 16
PEER_BLOCK = 128

IN_LAYOUT = (
    ("dn_qkv", DN_CONV_CH),
    ("dn_gate", DN_HEADS * DN_DV),
    ("dn_a", 2 * DN_HEADS),
    ("dn_b", 2 * DN_HEADS),
    ("gla_q", GLA_HEADS * GLA_DK),
    ("gla_k", GLA_HEADS * GLA_DK),
    ("gla_v", GLA_HEADS * GLA_DV),
    ("gla_gate", GLA_HEADS * GLA_DV),
    ("gla_alpha", 2 * GLA_RANK),
    ("mla_cq", MLA_Q_RANK),
    ("mla_ckv", MLA_KV_RANK),
    ("mla_kr", MLA_ROPE),
    ("fn", FN_GROUPS * FN_CH),
)
IN_WIDTH = sum(w for _, w in IN_LAYOUT)

kernel_name = "hybrid_parallel_heads_peer_dit_block"

F32 = jnp.float32


def rms_norm(x, w=None):
    xf = x.astype(F32)
    y = xf * lax.rsqrt(jnp.mean(xf * xf, axis=-1, keepdims=True) + EPS)
    if w is not None:
        y = y * w.astype(F32)
    return y.astype(x.dtype)


def modulate(x, shift, scale):
    return rms_norm(x) * (1 + scale) + shift


def l2_normalize(x):
    xf = x.astype(F32)
    return xf * lax.rsqrt(jnp.sum(xf * xf, axis=-1, keepdims=True) + EPS)


def split_projection(z):
    names = [n for n, _ in IN_LAYOUT]
    bounds = np.cumsum([w for _, w in IN_LAYOUT])[:-1].tolist()
    return dict(zip(names, jnp.split(z, bounds, axis=-1)))


def to_chunks(t, size):
    b, l = t.shape[:2]
    t = t.reshape(b, l // size, size, *t.shape[2:])
    return jnp.moveaxis(t, (1, 3), (0, 2))


def from_chunks(t):
    t = jnp.moveaxis(t, (0, 2), (1, 3))
    b, n, c = t.shape[:3]
    return t.reshape(b, n * c, *t.shape[3:])


def centred_conv(u, w):
    ch = u.shape[-1]
    return lax.conv_general_dilated(
        u, w[:, None, :].astype(u.dtype), window_strides=(1,),
        padding=[(DN_CONV // 2, DN_CONV // 2)],
        dimension_numbers=("NWC", "WIO", "NWC"), feature_group_count=ch)


def gated_rms(o, gate, w):
    b, l, h, dv = o.shape
    y = rms_norm(o, w) * jax.nn.silu(gate.reshape(b, l, h, dv).astype(F32))
    return y.reshape(b, l, h * dv).astype(gate.dtype)


def deltanet_inputs(p, conv_w, a_log, dt_bias):
    b, l, _ = p["dn_qkv"].shape
    qkv = jax.nn.silu(centred_conv(p["dn_qkv"], conv_w))
    q, k, v = jnp.split(qkv, [DN_HEADS * DN_DK, 2 * DN_HEADS * DN_DK], axis=-1)
    q = l2_normalize(q.reshape(b, l, DN_HEADS, DN_DK)) * DN_DK ** -0.5
    k = l2_normalize(k.reshape(b, l, DN_HEADS, DN_DK))
    v = v.reshape(b, l, DN_HEADS, DN_DV).astype(F32)
    a = p["dn_a"].astype(F32).reshape(b, l, 2, DN_HEADS)
    bb = p["dn_b"].astype(F32).reshape(b, l, 2, DN_HEADS)
    g = -jnp.exp(a_log.astype(F32)) * jax.nn.softplus(a + dt_bias.astype(F32))
    beta = jax.nn.sigmoid(bb)
    return ((q, k, v, g[:, :, 0], beta[:, :, 0]), (q, k, v, g[:, :, 1], beta[:, :, 1]))


def delta_chunked(q, k, v, g, beta, s0):
    c = DN_CHUNK
    qc, kc, vc = to_chunks(q, c), to_chunks(k, c), to_chunks(v, c)
    gc = jnp.cumsum(to_chunks(g, c), axis=-1)
    bc = to_chunks(beta, c)
    lower = jnp.tril(jnp.ones((c, c), dtype=bool))
    strict = jnp.tril(jnp.ones((c, c), dtype=bool), -1)
    diff = gc[..., :, None] - gc[..., None, :]
    decay = jnp.where(lower, jnp.exp(jnp.where(lower, diff, 0.0)), 0.0)
    a_mat = jnp.where(strict, bc[..., :, None] * jnp.einsum("nbhcd,nbhed->nbhce", kc, kc) * decay, 0.0)
    t_mat = a_mat + jnp.eye(c, dtype=a_mat.dtype)
    rhs = jnp.concatenate([vc * bc[..., None], kc * (bc * jnp.exp(gc))[..., None]], axis=-1)
    uw = lax.linalg.triangular_solve(t_mat, rhs, left_side=True, lower=True, unit_diagonal=True)
    u, w = jnp.split(uw, [vc.shape[-1]], axis=-1)
    p_mat = jnp.einsum("nbhcd,nbhed->nbhce", qc, kc) * decay
    q_dec = qc * jnp.exp(gc)[..., None]
    g_last = gc[..., -1]
    k_dec = kc * jnp.exp(g_last[..., None] - gc)[..., None]

    def step(s, xs):
        u_i, w_i, p_i, qd_i, kd_i, gl_i = xs
        v_new = u_i - jnp.einsum("bhcd,bhdf->bhcf", w_i, s)
        o_i = jnp.einsum("bhcd,bhdf->bhcf", qd_i, s) + jnp.einsum("bhce,bhef->bhcf", p_i, v_new)
        s = jnp.exp(gl_i)[..., None, None] * s + jnp.einsum("bhcd,bhcf->bhdf", kd_i, v_new)
        return s, o_i

    s_fin, o = lax.scan(step, s0, (u, w, p_mat, q_dec, k_dec, g_last))
    return from_chunks(o), s_fin


def gla_inputs(p, w2, b2):
    b, l, _ = p["gla_q"].shape
    q = p["gla_q"].astype(F32).reshape(b, l, GLA_HEADS, GLA_DK) * GLA_DK ** -0.5
    k = p["gla_k"].astype(F32).reshape(b, l, GLA_HEADS, GLA_DK)
    v = p["gla_v"].astype(F32).reshape(b, l, GLA_HEADS, GLA_DV)
    r = p["gla_alpha"].astype(F32).reshape(b, l, 2, GLA_RANK)
    z = jnp.einsum("blpr,prk->blpk", r, w2.astype(F32)) + b2.astype(F32)
    log_a = (jax.nn.log_sigmoid(z) / GLA_TAU).reshape(b, l, 2, GLA_HEADS, GLA_DK)
    return ((q, k, v, log_a[:, :, 0]), (q, k, v, log_a[:, :, 1]))


def gla_chunked(q, k, v, log_a, s0):
    c = GLA_CHUNK
    qc, kc, vc, ac = (to_chunks(t, c) for t in (q, k, v, log_a))
    bcum = jnp.cumsum(ac, axis=-2)
    b_last = bcum[..., -1:, :]
    q_in = qc * jnp.exp(bcum)
    q_rel = qc * jnp.exp(bcum - b_last)
    k_rel = kc * jnp.exp(b_last - bcum)
    lower = jnp.tril(jnp.ones((c, c), dtype=bool))
    att = jnp.where(lower, jnp.einsum("nbhcd,nbhed->nbhce", q_rel, k_rel), 0.0)
    o_intra = jnp.einsum("nbhce,nbhef->nbhcf", att, vc)

    def step(s, xs):
        qi, ki, vi, bl = xs
        o_i = jnp.einsum("bhcd,bhdf->bhcf", qi, s)
        s = jnp.exp(bl)[..., 0, :, None] * s + jnp.einsum("bhcd,bhcf->bhdf", ki, vi)
        return s, o_i

    s_fin, o_inter = lax.scan(step, s0, (q_in, k_rel, vc, b_last))
    return from_chunks(o_intra + o_inter), s_fin


def bidirectional(scan_fn, lat_dirs, ctx_dirs, s_zero):
    outs_l, outs_c = [], []
    for reverse, lat_args, ctx_args in zip((False, True), lat_dirs, ctx_dirs):
        flip = (lambda t: jnp.flip(t, axis=1)) if reverse else (lambda t: t)
        o_c, s_c = scan_fn(*[flip(a) for a in ctx_args], s_zero)
        o_l, _ = scan_fn(*[flip(a) for a in lat_args], s_c)
        outs_c.append(flip(o_c))
        outs_l.append(flip(o_l))
    return outs_l[0] + outs_l[1], outs_c[0] + outs_c[1]


def axial_rope_tables(rows):
    row = jnp.repeat(jnp.arange(rows), GRID_W).astype(F32)
    col = jnp.tile(jnp.arange(GRID_W), rows).astype(F32)
    half = MLA_ROPE // 2
    inv_freq = ROPE_BASE ** (-jnp.arange(0, half, 2, dtype=F32) / half)
    ang_r = row[:, None] * inv_freq
    ang_c = col[:, None] * inv_freq
    return (jnp.cos(ang_r)[:, None, :], jnp.sin(ang_r)[:, None, :],
            jnp.cos(ang_c)[:, None, :], jnp.sin(ang_c)[:, None, :])


def rotate_half(x, cos, sin):
    x1, x2 = jnp.split(x, 2, axis=-1)
    return jnp.concatenate([x1 * cos - x2 * sin, x2 * cos + x1 * sin], axis=-1)


def axial_rope(x, tabs):
    cr, sr, cc, sc = tabs
    xf = x.astype(F32)
    x_row, x_col = jnp.split(xf, 2, axis=-1)
    return jnp.concatenate([rotate_half(x_row, cr, sr), rotate_half(x_col, cc, sc)], axis=-1).astype(x.dtype)


def mla_inputs(p, q_norm, w_uq, kv_norm, w_ukv, rope):
    b, l, _ = p["mla_cq"].shape
    q = (rms_norm(p["mla_cq"], q_norm) @ w_uq).reshape(b, l, MLA_HEADS, MLA_NOPE + MLA_ROPE)
    kv = (rms_norm(p["mla_ckv"], kv_norm) @ w_ukv).reshape(b, l, MLA_HEADS, MLA_NOPE + MLA_V)
    q_nope, q_rope = jnp.split(q, [MLA_NOPE], axis=-1)
    k_nope, v = jnp.split(kv, [MLA_NOPE], axis=-1)
    k_rope = p["mla_kr"][:, :, None, :]
    if rope is not None:
        q_rope = axial_rope(q_rope, rope)
        k_rope = axial_rope(k_rope, rope)
    k = jnp.concatenate([k_nope, jnp.broadcast_to(k_rope, (b, l, MLA_HEADS, MLA_ROPE))], axis=-1)
    return jnp.concatenate([q_nope, q_rope], axis=-1), k, v


def softmax_attend(q, k, v):
    s = jnp.einsum("bqhd,bkhd->bhqk", q, k).astype(F32) * (q.shape[-1] ** -0.5)
    p = jax.nn.softmax(s, axis=-1).astype(v.dtype)
    return jnp.einsum("bhqk,bkhd->bqhd", p, v)


def mla_latent_attend(q, k_all, v_all):
    b, l, h, dq = q.shape
    nb = l // MLA_Q_BLOCK
    qb = jnp.moveaxis(q.reshape(b, nb, MLA_Q_BLOCK, h, dq), 1, 0)
    o = lax.map(lambda qi: softmax_attend(qi, k_all, v_all), qb)
    return jnp.moveaxis(o, 0, 1).reshape(b, l, h * v_all.shape[-1])


def fourier_mix(z):
    b, l, _ = z.shape
    zg = z.astype(F32).reshape(b, l, FN_GROUPS, FN_CH)
    y = jnp.fft.fftn(zg, axes=(1, 3), norm="ortho").real
    return y.reshape(b, l, FN_GROUPS * FN_CH).astype(z.dtype)


def peer_ffn(h, wq, keys, u_tab, v_tab):
    b, l, d = h.shape
    q = (h @ wq).reshape(b, l, PEER_HEADS, 2, PEER_HALF)
    s = jnp.einsum("blhpk,hpnk->blhpn", q, keys).astype(F32)
    s1, i1 = lax.top_k(s[..., 0, :], PEER_TOPK)
    s2, i2 = lax.top_k(s[..., 1, :], PEER_TOPK)
    cand = (s1[..., :, None] + s2[..., None, :]).reshape(b, l, PEER_HEADS, PEER_TOPK * PEER_TOPK)
    cidx = (i1[..., :, None] * PEER_N_KEYS + i2[..., None, :]).reshape(b, l, PEER_HEADS, PEER_TOPK * PEER_TOPK)
    top, pos = lax.top_k(cand, PEER_TOPK)
    idx = jnp.take_along_axis(cidx, pos, axis=-1)
    gate = jax.nn.softmax(top, axis=-1).astype(h.dtype)
    nblk = (b * l) // PEER_BLOCK
    hb = h.reshape(nblk, PEER_BLOCK, d)
    ib = idx.reshape(nblk, PEER_BLOCK, PEER_HEADS * PEER_TOPK)
    gb = gate.reshape(nblk, PEER_BLOCK, PEER_HEADS * PEER_TOPK)

    def block(args):
        hh, ii, gg = args
        act = jnp.einsum("td,tkd->tk", hh, u_tab[ii])
        wgt = gg * jax.nn.gelu(act, approximate=False)
        return jnp.einsum("tk,tkd->td", wgt, v_tab[ii])

    return lax.map(block, (hb, ib, gb)).reshape(b, l, d)


def token_mixers(pl, pc, rope, dn_conv, dn_a_log, dn_dt_bias, dn_norm, gla_w2, gla_b2, gla_norm,
                 mla_q_norm, mla_w_uq, mla_kv_norm, mla_w_ukv, ctx_out):
    b = pl["fn"].shape[0]
    s0_dn = jnp.zeros((b, DN_HEADS, DN_DK, DN_DV), F32)
    oa_l, oa_c = bidirectional(delta_chunked,
                               deltanet_inputs(pl, dn_conv, dn_a_log, dn_dt_bias),
                               deltanet_inputs(pc, dn_conv, dn_a_log, dn_dt_bias), s0_dn)
    s0_gla = jnp.zeros((b, GLA_HEADS, GLA_DK, GLA_DV), F32)
    ob_l, ob_c = bidirectional(gla_chunked, gla_inputs(pl, gla_w2, gla_b2),
                               gla_inputs(pc, gla_w2, gla_b2), s0_gla)
    q_l, k_l, v_l = mla_inputs(pl, mla_q_norm, mla_w_uq, mla_kv_norm, mla_w_ukv, rope)
    q_c, k_c, v_c = mla_inputs(pc, mla_q_norm, mla_w_uq, mla_kv_norm, mla_w_ukv, None)
    yc_l = mla_latent_attend(q_l, jnp.concatenate([k_l, k_c], axis=1), jnp.concatenate([v_l, v_c], axis=1))
    y_l = jnp.concatenate([gated_rms(oa_l, pl["dn_gate"], dn_norm),
                           gated_rms(ob_l, pl["gla_gate"], gla_norm),
                           yc_l, fourier_mix(pl["fn"])], axis=-1)
    if not ctx_out:
        return y_l, None
    bc, lc = q_c.shape[:2]
    y_c = jnp.concatenate([gated_rms(oa_c, pc["dn_gate"], dn_norm),
                           gated_rms(ob_c, pc["gla_gate"], gla_norm),
                           softmax_attend(q_c, k_c, v_c).reshape(bc, lc, MLA_HEADS * MLA_V),
                           fourier_mix(pc["fn"])], axis=-1)
    return y_l, y_c


def hybrid_layer(x, xc, mod, mod_c, rope, w_in, dn_conv, dn_a_log, dn_dt_bias, dn_norm,
                 gla_w2, gla_b2, gla_norm, mla_q_norm, mla_w_uq, mla_kv_norm, mla_w_ukv,
                 w_out, peer_wq, peer_keys, peer_u, peer_v, ctx_out):
    sh1, sc1, g1, sh2, sc2, g2 = jnp.split(mod[:, None, :], N_MOD, axis=-1)
    csh1, csc1, cg1, csh2, csc2, cg2 = jnp.split(mod_c, N_MOD, axis=-1)
    pl = split_projection(modulate(x, sh1, sc1) @ w_in)
    pc = split_projection(modulate(xc, csh1, csc1) @ w_in)
    y_l, y_c = token_mixers(pl, pc, rope, dn_conv, dn_a_log, dn_dt_bias, dn_norm, gla_w2, gla_b2,
                            gla_norm, mla_q_norm, mla_w_uq, mla_kv_norm, mla_w_ukv, ctx_out)
    x = x + g1 * (y_l @ w_out)
    x = x + g2 * peer_ffn(modulate(x, sh2, sc2), peer_wq, peer_keys, peer_u, peer_v)
    if ctx_out:
        xc = xc + cg1 * (y_c @ w_out)
        xc = xc + cg2 * peer_ffn(modulate(xc, csh2, csc2), peer_wq, peer_keys, peer_u, peer_v)
    return x, xc


def setup_inputs(seed: int = 0) -> dict:
    key = jax.random.key(seed)
    ks = jax.random.split(key, 26)
    d = D_MODEL

    def nrm(k, shape, scale):
        return jax.random.normal(k, shape, F32) * scale

    dt_init = jnp.exp(jax.random.uniform(ks[9], (DEPTH, 2, DN_HEADS), F32, math.log(1e-3), math.log(1e-1)))
    return {
        "x": nrm(ks[0], (BATCH, SEQ, d), 1.0),
        "c": nrm(ks[1], (BATCH, d), 1.0),
        "ctx": nrm(ks[2], (BATCH, CTX_LEN, d), 1.0),
        "c_ctx": nrm(ks[3], (d,), 1.0),
        "w_ada": nrm(ks[4], (DEPTH, d, N_MOD * d), 0.5 * d ** -0.5),
        "b_ada": nrm(ks[5], (DEPTH, N_MOD * d), 0.02),
        "w_in": nrm(ks[6], (DEPTH, d, IN_WIDTH), d ** -0.5),
        "dn_conv": nrm(ks[7], (DEPTH, DN_CONV, DN_CONV_CH), DN_CONV ** -0.5),
        "dn_a_log": jnp.log(jax.random.uniform(ks[8], (DEPTH, 2, DN_HEADS), F32, 1.0, 16.0)),
        "dn_dt_bias": dt_init + jnp.log(-jnp.expm1(-dt_init)),
        "dn_norm": 1.0 + nrm(ks[10], (DEPTH, DN_DV), 0.02),
        "gla_w2": nrm(ks[11], (DEPTH, 2, GLA_RANK, GLA_HEADS * GLA_DK), GLA_RANK ** -0.5),
        "gla_b2": nrm(ks[12], (DEPTH, 2, GLA_HEADS * GLA_DK), 0.02),
        "gla_norm": 1.0 + nrm(ks[13], (DEPTH, GLA_DV), 0.02),
        "mla_q_norm": 1.0 + nrm(ks[14], (DEPTH, MLA_Q_RANK), 0.02),
        "mla_w_uq": nrm(ks[15], (DEPTH, MLA_Q_RANK, MLA_HEADS * (MLA_NOPE + MLA_ROPE)), MLA_Q_RANK ** -0.5),
        "mla_kv_norm": 1.0 + nrm(ks[16], (DEPTH, MLA_KV_RANK), 0.02),
        "mla_w_ukv": nrm(ks[17], (DEPTH, MLA_KV_RANK, MLA_HEADS * (MLA_NOPE + MLA_V)), MLA_KV_RANK ** -0.5),
        "w_out": nrm(ks[18], (DEPTH, MIX_W, d), MIX_W ** -0.5),
        "peer_wq": nrm(ks[19], (DEPTH, d, PEER_HEADS * PEER_KEY_DIM), d ** -0.5),
        "peer_keys": nrm(ks[20], (DEPTH, PEER_HEADS, 2, PEER_N_KEYS, PEER_HALF), PEER_HALF ** -0.5),
        "peer_u": nrm(ks[21], (DEPTH, PEER_EXPERTS, d), d ** -0.5),
        "peer_v": nrm(ks[22], (DEPTH, PEER_EXPERTS, d), 1.0),
        "final_norm": 1.0 + nrm(ks[23], (d,), 0.02),
    }


def reference(x, c, ctx, c_ctx, w_ada, b_ada, w_in, dn_conv, dn_a_log, dn_dt_bias, dn_norm,
              gla_w2, gla_b2, gla_norm, mla_q_norm, mla_w_uq, mla_kv_norm, mla_w_ukv,
              w_out, peer_wq, peer_keys, peer_u, peer_v, final_norm):
    rows = x.shape[1] // GRID_W
    rope = axial_rope_tables(rows)
    xc = ctx
    for l in range(DEPTH):
        mod = jax.nn.silu(c) @ w_ada[l] + b_ada[l]
        mod_c = jax.nn.silu(c_ctx) @ w_ada[l] + b_ada[l]
        x, xc = hybrid_layer(x, xc, mod, mod_c, rope, w_in[l], dn_conv[l], dn_a_log[l], dn_dt_bias[l],
                             dn_norm[l], gla_w2[l], gla_b2[l], gla_norm[l], mla_q_norm[l], mla_w_uq[l],
                             mla_kv_norm[l], mla_w_ukv[l], w_out[l], peer_wq[l], peer_keys[l],
                             peer_u[l], peer_v[l], ctx_out=(l < DEPTH - 1))
    return rms_norm(x, final_norm)
```

```python
import functools
import math

import numpy as np
import jax
import jax.numpy as jnp
from jax import lax
from jax.experimental import pallas as pl
from jax.experimental.pallas import tpu as pltpu

F32 = jnp.float32
BF16 = jnp.bfloat16

EPS = 1e-6
N_MOD = 6
GRID_W = 64
ROPE_BASE = 10000.0

DN_HEADS, DN_DK, DN_DV, DN_CONV, DN_CHUNK = 4, 128, 128, 5, 64
GLA_HEADS, GLA_DK, GLA_DV, GLA_RANK, GLA_TAU, GLA_CHUNK = 4, 64, 128, 16, 16.0, 64
MLA_HEADS, MLA_NOPE, MLA_ROPE, MLA_V, MLA_Q_RANK, MLA_KV_RANK = 4, 128, 64, 128, 448, 128
FN_GROUPS, FN_CH = 4, 128
PEER_HEADS, PEER_N_KEYS, PEER_HALF, PEER_TOPK = 8, 128, 128, 16

VMEM_LIMIT_BYTES = 56 * 1024 * 1024
LANES = 128
SUBLANES = 8

C_DNQ, C_DNK, C_DNV, C_DNG = 0, 512, 1024, 1536
C_GLAV, C_GLAG, C_FN, C_CQ = 2048, 2560, 3072, 3584
C_GLAQ, C_GLAK = 4096, 4352
C_CKV, C_KR, C_SMALL = 4608, 4736, 4864
NP = 4992
PROJ_TN = NP // 3

NT_DIMS = (((1,), (1,)), ((), ()))
TN_DIMS = (((0,), (0,)), ((), ()))


def _cparams(sem):
    return pltpu.CompilerParams(dimension_semantics=sem, vmem_limit_bytes=VMEM_LIMIT_BYTES)


def _rms(x):
    return x * lax.rsqrt(jnp.mean(x * x, axis=-1, keepdims=True) + EPS)


def _ada_kernel(c_ref, w_ref, b_ref, o_ref):
    c = c_ref[...]
    a = (c * jax.nn.sigmoid(c)).astype(BF16)
    o_ref[0] = jnp.dot(a, w_ref[0].astype(BF16), preferred_element_type=F32) + b_ref[0]


def ada_mod(c_all, w_ada, b_ada):
    depth, d, n = w_ada.shape
    rows = c_all.shape[0]
    tn = 1024
    return pl.pallas_call(
        _ada_kernel,
        grid=(depth, n // tn),
        in_specs=[
            pl.BlockSpec((rows, d), lambda l, j: (0, 0)),
            pl.BlockSpec((1, d, tn), lambda l, j: (l, 0, j)),
            pl.BlockSpec((1, 1, tn), lambda l, j: (l, 0, j)),
        ],
        out_specs=pl.BlockSpec((1, rows, tn), lambda l, j: (l, 0, j)),
        out_shape=jax.ShapeDtypeStruct((depth, rows, n), F32),
        compiler_params=_cparams(("parallel", "parallel")),
        name="ada_mod",
    )(c_all, w_ada, b_ada.reshape(depth, 1, n))


def _proj_kernel(x_ref, sh_ref, sc_ref, w_ref, *rest):
    o_ref, xn_ref = rest[-2], rest[-1]

    @pl.when(pl.program_id(2) == 0)
    def _():
        xn_ref[...] = (_rms(x_ref[0]) * (1.0 + sc_ref[0]) + sh_ref[0]).astype(BF16)

    o_ref[0] = jnp.dot(xn_ref[...], w_ref[...], preferred_element_type=F32).astype(o_ref.dtype)


def proj_in(x, x_off, n_tiles, tm, shift, scale, w, out_rows, out_off, prev):
    b, _, d = x.shape
    bm = shift.shape[0]
    nj = w.shape[1] // PROJ_TN

    def mod_map(bi, i, j):
        return (bi if bm > 1 else 0, 0, 0)

    in_specs = [
        pl.BlockSpec((1, tm, d), lambda bi, i, j: (bi, i + x_off, 0)),
        pl.BlockSpec((1, 1, d), mod_map),
        pl.BlockSpec((1, 1, d), mod_map),
        pl.BlockSpec((d, PROJ_TN), lambda bi, i, j: (0, j)),
    ]
    args = [x, shift, scale, w]
    aliases = {}
    if prev is not None:
        in_specs.append(pl.BlockSpec(memory_space=pl.ANY))
        args.append(prev)
        aliases = {4: 0}
    return pl.pallas_call(
        _proj_kernel,
        grid=(b, n_tiles, nj),
        in_specs=in_specs,
        out_specs=pl.BlockSpec((1, tm, PROJ_TN), lambda bi, i, j: (bi, i + out_off, j)),
        out_shape=jax.ShapeDtypeStruct((b, out_rows, w.shape[1]), BF16),
        scratch_shapes=[pltpu.VMEM((tm, d), BF16)],
        input_output_aliases=aliases,
        compiler_params=_cparams(("parallel", "arbitrary", "arbitrary")),
        name="proj_in",
    )(*args)


def _mix_kernel(odf, odb, dg, ogf, ogb, gg, ym, yf, x_ref, g1, sh2, sc2, dnw, glw, w_ref, *rest):
    xo_ref, h_ref = rest[-2], rest[-1]

    def gated(of, ob, gate, wn):
        o = of[0].astype(F32) + ob[0].astype(F32)
        on = jnp.concatenate(
            [_rms(o[:, hh * LANES:(hh + 1) * LANES]) for hh in range(o.shape[1] // LANES)], axis=-1)
        g = gate[0].astype(F32)
        return (on * wn[...] * (g * jax.nn.sigmoid(g))).astype(BF16)

    q = w_ref.shape[0] // 4
    acc = jnp.dot(gated(odf, odb, dg, dnw), w_ref[0:q], preferred_element_type=F32)
    acc += jnp.dot(gated(ogf, ogb, gg, glw), w_ref[q:2 * q], preferred_element_type=F32)
    acc += jnp.dot(ym[0], w_ref[2 * q:3 * q], preferred_element_type=F32)
    acc += jnp.dot(yf[0], w_ref[3 * q:4 * q], preferred_element_type=F32)
    xn = x_ref[0] + g1[0] * acc
    xo_ref[0] = xn
    h_ref[0] = (_rms(xn) * (1.0 + sc2[0]) + sh2[0]).astype(BF16)


def mix_out(o_dn_f, o_dn_b, o_gla_f, o_gla_b, y_mla, y_fn, proj, off, x, x_off, n_tiles, tm,
            g1, sh2, sc2, dn_norm, gla_norm, w_out, out_rows, out_off, prev_x, prev_h):
    b, _, d = x.shape
    bm = g1.shape[0]
    wq = 512

    def mix_spec(col):
        return pl.BlockSpec((1, tm, wq), lambda bi, i: (bi, i + off, col))

    def mod_map(bi, i):
        return (bi if bm > 1 else 0, 0, 0)

    in_specs = [
        mix_spec(0), mix_spec(0), mix_spec(C_DNG // wq),
        mix_spec(0), mix_spec(0), mix_spec(C_GLAG // wq),
        mix_spec(0), mix_spec(0),
        pl.BlockSpec((1, tm, d), lambda bi, i: (bi, i + x_off, 0)),
        pl.BlockSpec((1, 1, d), mod_map), pl.BlockSpec((1, 1, d), mod_map), pl.BlockSpec((1, 1, d), mod_map),
        pl.BlockSpec((1, wq), lambda bi, i: (0, 0)), pl.BlockSpec((1, wq), lambda bi, i: (0, 0)),
        pl.BlockSpec((4 * wq, d), lambda bi, i: (0, 0)),
    ]
    args = [o_dn_f, o_dn_b, proj, o_gla_f, o_gla_b, proj, y_mla, y_fn, x, g1, sh2, sc2,
            jnp.tile(dn_norm.astype(F32), DN_HEADS).reshape(1, wq),
            jnp.tile(gla_norm.astype(F32), GLA_HEADS).reshape(1, wq), w_out]
    aliases = {}
    if prev_x is not None:
        in_specs += [pl.BlockSpec(memory_space=pl.ANY), pl.BlockSpec(memory_space=pl.ANY)]
        args += [prev_x, prev_h]
        aliases = {len(args) - 2: 0, len(args) - 1: 1}
    out_spec = pl.BlockSpec((1, tm, d), lambda bi, i: (bi, i + out_off, 0))
    return pl.pallas_call(
        _mix_kernel,
        grid=(b, n_tiles),
        in_specs=in_specs,
        out_specs=[out_spec, out_spec],
        out_shape=[jax.ShapeDtypeStruct((b, out_rows, d), F32), jax.ShapeDtypeStruct((b, out_rows, d), BF16)],
        input_output_aliases=aliases,
        compiler_params=_cparams(("parallel", "parallel")),
        name="mix_out",
    )(*args)


N_RANK = PEER_TOPK + 1
VAL_ROWS = 24
NEG = -1e30


def _top_values(work, n):
    vals = []
    for _ in range(n):
        m = jnp.max(work, axis=0, keepdims=True)
        vals.append(m)
        work = jnp.where(work >= m, NEG, work)
    return vals


def _peer_score_kernel(h_ref, wq_ref, keys_ref, s_ref, aux_ref, vals_ref):
    tm = h_ref.shape[0]
    n_chunks = tm // LANES
    q = jnp.dot(h_ref[...], wq_ref[...], preferred_element_type=F32).astype(BF16)
    for hp in range(2 * PEER_HEADS):
        s_ref[hp] = lax.dot_general(keys_ref[hp], q[:, hp * PEER_HALF:(hp + 1) * PEER_HALF], NT_DIMS,
                                    preferred_element_type=F32)

    def stage1(hp, carry):
        for c in range(n_chunks):
            lanes = slice(c * LANES, (c + 1) * LANES)
            vals = _top_values(s_ref[hp, :, lanes], N_RANK)
            vals += [jnp.full((1, LANES), NEG, F32)] * (VAL_ROWS - N_RANK)
            vals_ref[hp, :, lanes] = jnp.concatenate(vals, axis=0)
        return carry

    lax.fori_loop(0, 2 * PEER_HEADS, stage1, 0)

    row = lax.broadcasted_iota(jnp.int32, (SUBLANES, LANES), 0)

    def stage2(hh, carry):
        for c in range(n_chunks):
            lanes = slice(c * LANES, (c + 1) * LANES)
            v1 = vals_ref[2 * hh, :, lanes]
            v2 = vals_ref[2 * hh + 1, :, lanes]
            cands = []
            for r1 in range(N_RANK):
                n = N_RANK // (r1 + 1)
                for g in range(-(-n // SUBLANES)):
                    blk = v2[g * SUBLANES:(g + 1) * SUBLANES] + v1[r1:r1 + 1]
                    cands.append(jnp.where(row + g * SUBLANES < n, blk, NEG))
            t0 = v1[0:1] + v2[0:1]
            taken = jnp.zeros((1, LANES), F32)
            z = jnp.zeros((1, LANES), F32)
            t_k = jnp.zeros((1, LANES), F32)
            t_k1 = jnp.zeros((1, LANES), F32)
            for _ in range(N_RANK):
                m8 = functools.reduce(jnp.maximum, cands)
                m = jnp.max(m8, axis=0, keepdims=True)
                cnt8 = functools.reduce(lambda a, b: a + b, [jnp.where(cd == m, 1.0, 0.0) for cd in cands])
                cnt = jnp.sum(cnt8, axis=0, keepdims=True)
                after = taken + cnt
                z = z + jnp.clip(PEER_TOPK - taken, 0.0, cnt) * jnp.exp(m - t0)
                t_k = jnp.where((taken < PEER_TOPK) & (after >= PEER_TOPK), m, t_k)
                t_k1 = jnp.where((taken < N_RANK) & (after >= N_RANK), m, t_k1)
                taken = after
                cands = [jnp.where(cd >= m, NEG, cd) for cd in cands]
            tau = 0.5 * (t_k + t_k1)
            aux = jnp.concatenate([tau, v1[0:1], v2[0:1], 1.0 / z, jnp.zeros((4, LANES), F32)], axis=0)
            aux_ref[hh, :, lanes] = aux
        return carry

    lax.fori_loop(0, PEER_HEADS, stage2, 0)


def peer_scores(h, wq, keys, tm):
    tn, d = h.shape
    nk = 2 * PEER_HEADS
    return pl.pallas_call(
        _peer_score_kernel,
        grid=(tn // tm,),
        in_specs=[
            pl.BlockSpec((tm, d), lambda i: (i, 0)),
            pl.BlockSpec(wq.shape, lambda i: (0, 0)),
            pl.BlockSpec(keys.shape, lambda i: (0, 0, 0)),
        ],
        out_specs=[
            pl.BlockSpec((nk, PEER_N_KEYS, tm), lambda i: (0, 0, i)),
            pl.BlockSpec((PEER_HEADS, SUBLANES, tm), lambda i: (0, 0, i)),
        ],
        out_shape=[jax.ShapeDtypeStruct((nk, PEER_N_KEYS, tn), F32),
                   jax.ShapeDtypeStruct((PEER_HEADS, SUBLANES, tn), F32)],
        scratch_shapes=[pltpu.VMEM((nk, VAL_ROWS, tm), F32)],
        compiler_params=_cparams(("parallel",)),
        name="peer_scores",
    )(h, wq, keys)


PEER_TE = 1024
PEER_G2_ROWS = 256


def _gelu(a):
    return 0.5 * a * (1.0 + lax.erf(a * (1.0 / math.sqrt(2.0))))


def _peer_expert_kernel(h_ref, s_ref, aux_ref, u_ref, vt_ref, x_ref, g2_ref, fn_ref, o_ref,
                        acc_ref, e2_ref, act_ref, wt_ref, *, final):
    e = pl.program_id(1)
    tm = h_ref.shape[0]
    n_chunks = tm // LANES
    n_i = PEER_TE // PEER_N_KEYS

    @pl.when(e == 0)
    def _():
        acc_ref[...] = jnp.zeros_like(acc_ref)
        for hh in range(PEER_HEADS):
            e2_ref[hh] = jnp.exp(s_ref[2 * hh + 1] - aux_ref[hh, 2:3, :]) * aux_ref[hh, 3:4, :]

    act_ref[...] = lax.dot_general(u_ref[...], h_ref[...], NT_DIMS, preferred_element_type=F32)

    i0 = pl.multiple_of(e * n_i, n_i)
    thr, e1 = [], []
    for hh in range(PEER_HEADS):
        s1 = s_ref[2 * hh, pl.ds(i0, n_i), :]
        thr.append(aux_ref[hh, 0:1, :] - s1)
        e1.append(jnp.exp(s1 - aux_ref[hh, 1:2, :]))
    for il in range(n_i):
        rows = slice(il * PEER_N_KEYS, (il + 1) * PEER_N_KEYS)
        for c in range(n_chunks):
            lanes = slice(c * LANES, (c + 1) * LANES)
            g = jnp.zeros((PEER_N_KEYS, LANES), F32)
            for hh in range(PEER_HEADS):
                sel = jnp.where(s_ref[2 * hh + 1, :, lanes] >= thr[hh][il:il + 1, lanes], e2_ref[hh, :, lanes], 0.0)
                g = g + sel * e1[hh][il:il + 1, lanes]
            wt_ref[rows, lanes] = (g * _gelu(act_ref[rows, lanes])).astype(BF16)

    acc_ref[...] += jnp.dot(vt_ref[...], wt_ref[...], preferred_element_type=F32)

    @pl.when(e == pl.num_programs(1) - 1)
    def _():
        y = jnp.transpose(acc_ref[...])
        for r in range(tm // PEER_G2_ROWS):
            rows = slice(r * PEER_G2_ROWS, (r + 1) * PEER_G2_ROWS)
            out = x_ref[rows, :] + g2_ref[r] * y[rows, :]
            if final:
                out = _rms(out) * fn_ref[...]
            o_ref[rows, :] = out


def peer_experts(h, s, aux, u, vt, x, g2_rows, final_norm, tm, final):
    tn, d = h.shape
    ne = u.shape[0]
    nk = 2 * PEER_HEADS
    once = pl.Buffered(1)
    return pl.pallas_call(
        functools.partial(_peer_expert_kernel, final=final),
        grid=(tn // tm, ne // PEER_TE),
        in_specs=[
            pl.BlockSpec((tm, d), lambda i, e: (i, 0), pipeline_mode=once),
            pl.BlockSpec((nk, PEER_N_KEYS, tm), lambda i, e: (0, 0, i), pipeline_mode=once),
            pl.BlockSpec((PEER_HEADS, SUBLANES, tm), lambda i, e: (0, 0, i)),
            pl.BlockSpec((PEER_TE, d), lambda i, e: (e, 0)),
            pl.BlockSpec((d, PEER_TE), lambda i, e: (0, e)),
            pl.BlockSpec((tm, d), lambda i, e: (i, 0), pipeline_mode=once),
            pl.BlockSpec((tm // PEER_G2_ROWS, 1, d), lambda i, e: (i, 0, 0)),
            pl.BlockSpec((1, d), lambda i, e: (0, 0)),
        ],
        out_specs=pl.BlockSpec((tm, d), lambda i, e: (i, 0)),
        out_shape=jax.ShapeDtypeStruct((tn, d), F32),
        scratch_shapes=[
            pltpu.VMEM((d, tm), F32),
            pltpu.VMEM((PEER_HEADS, PEER_N_KEYS, tm), F32),
            pltpu.VMEM((PEER_TE, tm), F32),
            pltpu.VMEM((PEER_TE, tm), BF16),
        ],
        compiler_params=_cparams(("parallel", "arbitrary")),
        name="peer_experts",
    )(h, s, aux, u, vt, x, g2_rows, final_norm.reshape(1, d).astype(F32))


IN_LAYOUT = (
    ("dn_qkv", 1536), ("dn_gate", 512), ("dn_a", 8), ("dn_b", 8), ("gla_q", 256), ("gla_k", 256),
    ("gla_v", 512), ("gla_gate", 512), ("gla_alpha", 32), ("mla_cq", 448), ("mla_ckv", 128),
    ("mla_kr", 64), ("fn", 512),
)


def _in_cols():
    offs, o = {}, 0
    for name, w in IN_LAYOUT:
        offs[name] = (o, o + w)
        o += w
    return offs


def _rot_half_cols(w):
    return jnp.concatenate([-w[:, 16:32], w[:, 0:16], -w[:, 48:64], w[:, 32:48]], axis=1)


def prep_w_in(w_in):
    c = _in_cols()
    d = w_in.shape[0]

    def cols(name):
        a, b = c[name]
        return w_in[:, a:b]

    z = lambda n: jnp.zeros((d, n), w_in.dtype)
    kr = cols("mla_kr")
    pieces = [
        cols("dn_qkv"), cols("dn_gate"), cols("gla_v"), cols("gla_gate"), cols("fn"),
        cols("mla_cq"), z(512 - MLA_Q_RANK), cols("gla_q"), cols("gla_k"), cols("mla_ckv"),
        kr, _rot_half_cols(kr), cols("dn_a"), cols("dn_b"), cols("gla_alpha"), z(128 - 48),
    ]
    w = jnp.concatenate(pieces, axis=1)
    assert w.shape[1] == NP
    return w.astype(BF16)


def split_proj(p):
    f = lambda a, w: p[..., a:a + w].astype(F32)
    return {
        "dn_qkv": f(C_DNQ, 1536), "dn_gate": f(C_DNG, 512), "dn_a": f(C_SMALL, 8), "dn_b": f(C_SMALL + 8, 8),
        "gla_q": f(C_GLAQ, 256), "gla_k": f(C_GLAK, 256), "gla_v": f(C_GLAV, 512), "gla_gate": f(C_GLAG, 512),
        "gla_alpha": f(C_SMALL + 16, 32), "mla_cq": f(C_CQ, MLA_Q_RANK), "mla_ckv": f(C_CKV, 128),
        "mla_kr": f(C_KR, 64), "fn": f(C_FN, 512),
    }


def _x_rms_norm(x, w=None):
    xf = x.astype(F32)
    y = xf * lax.rsqrt(jnp.mean(xf * xf, axis=-1, keepdims=True) + EPS)
    if w is not None:
        y = y * w.astype(F32)
    return y.astype(x.dtype)


def _x_l2(x):
    return x * lax.rsqrt(jnp.sum(x * x, axis=-1, keepdims=True) + EPS)


def _x_to_chunks(t, size):
    b, l = t.shape[:2]
    t = t.reshape(b, l // size, size, *t.shape[2:])
    return jnp.moveaxis(t, (1, 3), (0, 2))


def _x_from_chunks(t):
    t = jnp.moveaxis(t, (0, 2), (1, 3))
    b, n, c = t.shape[:3]
    return t.reshape(b, n * c, *t.shape[3:])


def _x_conv(u, w):
    ch = u.shape[-1]
    return lax.conv_general_dilated(
        u, w[:, None, :].astype(u.dtype), window_strides=(1,), padding=[(DN_CONV // 2, DN_CONV // 2)],
        dimension_numbers=("NWC", "WIO", "NWC"), feature_group_count=ch)


def _x_dn_inputs(p, conv_w, a_log, dt_bias):
    b, l, _ = p["dn_qkv"].shape
    qkv = jax.nn.silu(_x_conv(p["dn_qkv"], conv_w))
    q, k, v = jnp.split(qkv, [DN_HEADS * DN_DK, 2 * DN_HEADS * DN_DK], axis=-1)
    q = _x_l2(q.reshape(b, l, DN_HEADS, DN_DK)) * DN_DK ** -0.5
    k = _x_l2(k.reshape(b, l, DN_HEADS, DN_DK))
    v = v.reshape(b, l, DN_HEADS, DN_DV)
    a = p["dn_a"].reshape(b, l, 2, DN_HEADS)
    bb = p["dn_b"].reshape(b, l, 2, DN_HEADS)
    g = -jnp.exp(a_log.astype(F32)) * jax.nn.softplus(a + dt_bias.astype(F32))
    beta = jax.nn.sigmoid(bb)
    return ((q, k, v, g[:, :, 0], beta[:, :, 0]), (q, k, v, g[:, :, 1], beta[:, :, 1]))


def _x_delta(q, k, v, g, beta, s0):
    c = DN_CHUNK
    qc, kc, vc = _x_to_chunks(q, c), _x_to_chunks(k, c), _x_to_chunks(v, c)
    gc = jnp.cumsum(_x_to_chunks(g, c), axis=-1)
    bc = _x_to_chunks(beta, c)
    lower = jnp.tril(jnp.ones((c, c), dtype=bool))
    strict = jnp.tril(jnp.ones((c, c), dtype=bool), -1)
    diff = gc[..., :, None] - gc[..., None, :]
    decay = jnp.where(lower, jnp.exp(jnp.where(lower, diff, 0.0)), 0.0)
    a_mat = jnp.where(strict, bc[..., :, None] * jnp.einsum("nbhcd,nbhed->nbhce", kc, kc) * decay, 0.0)
    t_mat = a_mat + jnp.eye(c, dtype=a_mat.dtype)
    rhs = jnp.concatenate([vc * bc[..., None], kc * (bc * jnp.exp(gc))[..., None]], axis=-1)
    uw = lax.linalg.triangular_solve(t_mat, rhs, left_side=True, lower=True, unit_diagonal=True)
    u, w = jnp.split(uw, [vc.shape[-1]], axis=-1)
    p_mat = jnp.einsum("nbhcd,nbhed->nbhce", qc, kc) * decay
    q_dec = qc * jnp.exp(gc)[..., None]
    g_last = gc[..., -1]
    k_dec = kc * jnp.exp(g_last[..., None] - gc)[..., None]

    def step(s, xs):
        u_i, w_i, p_i, qd_i, kd_i, gl_i = xs
        v_new = u_i - jnp.einsum("bhcd,bhdf->bhcf", w_i, s)
        o_i = jnp.einsum("bhcd,bhdf->bhcf", qd_i, s) + jnp.einsum("bhce,bhef->bhcf", p_i, v_new)
        s = jnp.exp(gl_i)[..., None, None] * s + jnp.einsum("bhcd,bhcf->bhdf", kd_i, v_new)
        return s, o_i

    s_fin, o = lax.scan(step, s0, (u, w, p_mat, q_dec, k_dec, g_last))
    return _x_from_chunks(o), s_fin


def _x_gla_inputs(p, w2, b2):
    b, l, _ = p["gla_q"].shape
    q = p["gla_q"].reshape(b, l, GLA_HEADS, GLA_DK) * GLA_DK ** -0.5
    k = p["gla_k"].reshape(b, l, GLA_HEADS, GLA_DK)
    v = p["gla_v"].reshape(b, l, GLA_HEADS, GLA_DV)
    r = p["gla_alpha"].reshape(b, l, 2, GLA_RANK)
    z = jnp.einsum("blpr,prk->blpk", r, w2.astype(F32)) + b2.astype(F32)
    log_a = (jax.nn.log_sigmoid(z) / GLA_TAU).reshape(b, l, 2, GLA_HEADS, GLA_DK)
    return ((q, k, v, log_a[:, :, 0]), (q, k, v, log_a[:, :, 1]))


def _x_gla(q, k, v, log_a, s0):
    c = GLA_CHUNK
    qc, kc, vc, ac = (_x_to_chunks(t, c) for t in (q, k, v, log_a))
    bcum = jnp.cumsum(ac, axis=-2)
    b_last = bcum[..., -1:, :]
    q_in = qc * jnp.exp(bcum)
    q_rel = qc * jnp.exp(bcum - b_last)
    k_rel = kc * jnp.exp(b_last - bcum)
    lower = jnp.tril(jnp.ones((c, c), dtype=bool))
    att = jnp.where(lower, jnp.einsum("nbhcd,nbhed->nbhce", q_rel, k_rel), 0.0)
    o_intra = jnp.einsum("nbhce,nbhef->nbhcf", att, vc)

    def step(s, xs):
        qi, ki, vi, bl = xs
        o_i = jnp.einsum("bhcd,bhdf->bhcf", qi, s)
        s = jnp.exp(bl)[..., 0, :, None] * s + jnp.einsum("bhcd,bhcf->bhdf", ki, vi)
        return s, o_i

    s_fin, o_inter = lax.scan(step, s0, (q_in, k_rel, vc, b_last))
    return _x_from_chunks(o_intra + o_inter), s_fin


def _x_bidir(scan_fn, lat_dirs, ctx_dirs, s_zero):
    outs_l, outs_c = [], []
    for reverse, lat_args, ctx_args in zip((False, True), lat_dirs, ctx_dirs):
        flip = (lambda t: jnp.flip(t, axis=1)) if reverse else (lambda t: t)
        o_c, s_c = scan_fn(*[flip(a) for a in ctx_args], s_zero)
        o_l, _ = scan_fn(*[flip(a) for a in lat_args], s_c)
        outs_c.append(flip(o_c))
        outs_l.append(flip(o_l))
    return outs_l, outs_c


def _x_rope_tables(rows):
    row = jnp.repeat(jnp.arange(rows), GRID_W).astype(F32)
    col = jnp.tile(jnp.arange(GRID_W), rows).astype(F32)
    half = MLA_ROPE // 2
    inv_freq = ROPE_BASE ** (-jnp.arange(0, half, 2, dtype=F32) / half)
    ang_r = row[:, None] * inv_freq
    ang_c = col[:, None] * inv_freq
    return (jnp.cos(ang_r)[:, None, :], jnp.sin(ang_r)[:, None, :],
            jnp.cos(ang_c)[:, None, :], jnp.sin(ang_c)[:, None, :])


def _x_rot_half(x, cos, sin):
    x1, x2 = jnp.split(x, 2, axis=-1)
    return jnp.concatenate([x1 * cos - x2 * sin, x2 * cos + x1 * sin], axis=-1)


def _x_rope(x, tabs):
    cr, sr, cc, sc = tabs
    x_row, x_col = jnp.split(x, 2, axis=-1)
    return jnp.concatenate([_x_rot_half(x_row, cr, sr), _x_rot_half(x_col, cc, sc)], axis=-1)


def _x_mla_inputs(p, q_norm, w_uq, kv_norm, w_ukv, rope):
    b, l, _ = p["mla_cq"].shape
    q = (_x_rms_norm(p["mla_cq"], q_norm) @ w_uq).reshape(b, l, MLA_HEADS, MLA_NOPE + MLA_ROPE)
    kv = (_x_rms_norm(p["mla_ckv"], kv_norm) @ w_ukv).reshape(b, l, MLA_HEADS, MLA_NOPE + MLA_V)
    q_nope, q_rope = jnp.split(q, [MLA_NOPE], axis=-1)
    k_nope, v = jnp.split(kv, [MLA_NOPE], axis=-1)
    k_rope = p["mla_kr"][:, :, None, :]
    if rope is not None:
        q_rope = _x_rope(q_rope, rope)
        k_rope = _x_rope(k_rope, rope)
    k = jnp.concatenate([k_nope, jnp.broadcast_to(k_rope, (b, l, MLA_HEADS, MLA_ROPE))], axis=-1)
    return jnp.concatenate([q_nope, q_rope], axis=-1), k, v


def _x_attend(q, k, v):
    s = jnp.einsum("bqhd,bkhd->bhqk", q, k) * (q.shape[-1] ** -0.5)
    p = jax.nn.softmax(s, axis=-1)
    o = jnp.einsum("bhqk,bkhd->bqhd", p, v)
    return o.reshape(o.shape[0], o.shape[1], -1)


def _x_fourier(z):
    b, l, _ = z.shape
    zg = z.reshape(b, l, FN_GROUPS, FN_CH)
    y = jnp.fft.fftn(zg, axes=(1, 3), norm="ortho").real
    return y.reshape(b, l, FN_GROUPS * FN_CH)


def jax_mixers(proj, l, prm, which):
    b = proj.shape[0]
    p_l, p_c = split_proj(proj[:, :l]), split_proj(proj[:, l:])
    cat = lambda a, c: jnp.concatenate([a, c], axis=1).astype(BF16)
    out = {}
    if "dn" in which:
        s0 = jnp.zeros((b, DN_HEADS, DN_DK, DN_DV), F32)
        ol, oc = _x_bidir(_x_delta, _x_dn_inputs(p_l, prm["dn_conv"], prm["dn_a_log"], prm["dn_dt_bias"]),
                          _x_dn_inputs(p_c, prm["dn_conv"], prm["dn_a_log"], prm["dn_dt_bias"]), s0)
        out["dn_f"] = cat(ol[0].reshape(b, l, -1), oc[0].reshape(b, -1, 512))
        out["dn_b"] = cat(ol[1].reshape(b, l, -1), oc[1].reshape(b, -1, 512))
    if "gla" in which:
        s0 = jnp.zeros((b, GLA_HEADS, GLA_DK, GLA_DV), F32)
        ol, oc = _x_bidir(_x_gla, _x_gla_inputs(p_l, prm["gla_w2"], prm["gla_b2"]),
                          _x_gla_inputs(p_c, prm["gla_w2"], prm["gla_b2"]), s0)
        out["gla_f"] = cat(ol[0].reshape(b, l, -1), oc[0].reshape(b, -1, 512))
        out["gla_b"] = cat(ol[1].reshape(b, l, -1), oc[1].reshape(b, -1, 512))
    if "mla" in which:
        rope = _x_rope_tables(l // GRID_W)
        q_l, k_l, v_l = _x_mla_inputs(p_l, prm["mla_q_norm"], prm["mla_w_uq"], prm["mla_kv_norm"], prm["mla_w_ukv"], rope)
        q_c, k_c, v_c = _x_mla_inputs(p_c, prm["mla_q_norm"], prm["mla_w_uq"], prm["mla_kv_norm"], prm["mla_w_ukv"], None)
        y_l = _x_attend(q_l, jnp.concatenate([k_l, k_c], axis=1), jnp.concatenate([v_l, v_c], axis=1))
        out["mla"] = cat(y_l, _x_attend(q_c, k_c, v_c))
    if "fn" in which:
        out["fn"] = cat(_x_fourier(p_l["fn"]), _x_fourier(p_c["fn"]))
    return out


CTX_TM = 256


def kernel(x, c, ctx, c_ctx, w_ada, b_ada, w_in, dn_conv, dn_a_log, dn_dt_bias, dn_norm, gla_w2, gla_b2,
           gla_norm, mla_q_norm, mla_w_uq, mla_kv_norm, mla_w_ukv, w_out, peer_wq, peer_keys, peer_u, peer_v,
           final_norm):
    b, l, d = x.shape
    lc = ctx.shape[1]
    lt = l + lc
    depth = w_ada.shape[0]
    lat_tm = min(1024, l)
    mix_tm = min(512, l)
    peer_tm = 512
    assert lc == CTX_TM and l % lat_tm == 0 and l % CTX_TM == 0

    pad = (-(b + 1)) % SUBLANES
    c_all = jnp.concatenate([c, c_ctx[None, :], jnp.zeros((pad, d), F32)], axis=0)
    mod = ada_mod(c_all, w_ada, b_ada)

    x_lat, x_lat_off, x_ctx, x_ctx_off = x, 0, ctx, 0
    out = None
    for li in range(depth):
        last = li == depth - 1
        m_lat = [mod[li, :b, k * d:(k + 1) * d].reshape(b, 1, d) for k in range(N_MOD)]
        m_ctx = [mod[li, b:b + 1, k * d:(k + 1) * d].reshape(1, 1, d) for k in range(N_MOD)]
        prm = dict(dn_conv=dn_conv[li], dn_a_log=dn_a_log[li], dn_dt_bias=dn_dt_bias[li], gla_w2=gla_w2[li],
                   gla_b2=gla_b2[li], mla_q_norm=mla_q_norm[li], mla_w_uq=mla_w_uq[li],
                   mla_kv_norm=mla_kv_norm[li], mla_w_ukv=mla_w_ukv[li])

        w_p = prep_w_in(w_in[li])
        proj = proj_in(x_lat, x_lat_off, l // lat_tm, lat_tm, m_lat[0], m_lat[1], w_p, lt, 0, None)
        proj = proj_in(x_ctx, x_ctx_off, 1, CTX_TM, m_ctx[0], m_ctx[1], w_p, lt, l // CTX_TM, proj)

        mx = jax_mixers(proj, l, prm, ("dn", "gla", "mla", "fn"))

        w_o = w_out[li].astype(BF16)
        rows = l if last else lt
        xn, h = mix_out(mx["dn_f"], mx["dn_b"], mx["gla_f"], mx["gla_b"], mx["mla"], mx["fn"], proj, 0,
                        x_lat, x_lat_off, l // mix_tm, mix_tm, m_lat[2], m_lat[3], m_lat[4],
                        dn_norm[li], gla_norm[li], w_o, rows, 0, None, None)
        if not last:
            xn, h = mix_out(mx["dn_f"], mx["dn_b"], mx["gla_f"], mx["gla_b"], mx["mla"], mx["fn"], proj,
                            l // CTX_TM, x_ctx, x_ctx_off, 1, CTX_TM, m_ctx[2], m_ctx[3], m_ctx[4],
                            dn_norm[li], gla_norm[li], w_o, rows, l // CTX_TM, xn, h)

        tn = b * rows
        g2_lat = jnp.broadcast_to(m_lat[5], (b, l // PEER_G2_ROWS, d))
        if last:
            g2_rows = g2_lat.reshape(tn // PEER_G2_ROWS, 1, d)
        else:
            g2_ctx = jnp.broadcast_to(m_ctx[5], (b, lc // PEER_G2_ROWS, d))
            g2_rows = jnp.concatenate([g2_lat, g2_ctx], axis=1).reshape(tn // PEER_G2_ROWS, 1, d)
        hf = h.reshape(tn, d)
        s, aux = peer_scores(hf, peer_wq[li].astype(BF16),
                             peer_keys[li].reshape(2 * PEER_HEADS, PEER_N_KEYS, PEER_HALF).astype(BF16), peer_tm)
        y = peer_experts(hf, s, aux, peer_u[li].astype(BF16), peer_v[li].T.astype(BF16), xn.reshape(tn, d),
                         g2_rows, final_norm, peer_tm, last)
        out = y.reshape(b, rows, d)
        x_lat, x_lat_off, x_ctx, x_ctx_off = out, 0, out, l // CTX_TM
    return out
```

```python
import functools
import math

import numpy as np
import jax
import jax.numpy as jnp
from jax import lax
from jax.experimental import pallas as pl
from jax.experimental.pallas import tpu as pltpu

F32 = jnp.float32
BF16 = jnp.bfloat16

EPS = 1e-6
N_MOD = 6
GRID_W = 64
ROPE_BASE = 10000.0

DN_HEADS, DN_DK, DN_DV, DN_CONV, DN_CHUNK = 4, 128, 128, 5, 64
GLA_HEADS, GLA_DK, GLA_DV, GLA_RANK, GLA_TAU, GLA_CHUNK = 4, 64, 128, 16, 16.0, 64
MLA_HEADS, MLA_NOPE, MLA_ROPE, MLA_V, MLA_Q_RANK, MLA_KV_RANK = 4, 128, 64, 128, 448, 128
FN_GROUPS, FN_CH = 4, 128
PEER_HEADS, PEER_N_KEYS, PEER_HALF, PEER_TOPK = 8, 128, 128, 16

VMEM_LIMIT_BYTES = 56 * 1024 * 1024
LANES = 128
SUBLANES = 8

C_DNQ, C_DNK, C_DNV, C_DNG = 0, 512, 1024, 1536
C_GLAV, C_GLAG, C_FN, C_CQ = 2048, 2560, 3072, 3584
C_GLAQ, C_GLAK = 4096, 4352
C_CKV, C_KR, C_SMALL = 4608, 4736, 4864
NP = 4992
PROJ_TN = NP // 3

NT_DIMS = (((1,), (1,)), ((), ()))
TN_DIMS = (((0,), (0,)), ((), ()))


def _cparams(sem):
    return pltpu.CompilerParams(dimension_semantics=sem, vmem_limit_bytes=VMEM_LIMIT_BYTES)


def _rms(x):
    return x * lax.rsqrt(jnp.mean(x * x, axis=-1, keepdims=True) + EPS)


def _ada_kernel(c_ref, w_ref, b_ref, o_ref):
    c = c_ref[...]
    a = (c * jax.nn.sigmoid(c)).astype(BF16)
    o_ref[0] = jnp.dot(a, w_ref[0].astype(BF16), preferred_element_type=F32) + b_ref[0]


def ada_mod(c_all, w_ada, b_ada):
    depth, d, n = w_ada.shape
    rows = c_all.shape[0]
    tn = 1024
    return pl.pallas_call(
        _ada_kernel,
        grid=(depth, n // tn),
        in_specs=[
            pl.BlockSpec((rows, d), lambda l, j: (0, 0)),
            pl.BlockSpec((1, d, tn), lambda l, j: (l, 0, j)),
            pl.BlockSpec((1, 1, tn), lambda l, j: (l, 0, j)),
        ],
        out_specs=pl.BlockSpec((1, rows, tn), lambda l, j: (l, 0, j)),
        out_shape=jax.ShapeDtypeStruct((depth, rows, n), F32),
        compiler_params=_cparams(("parallel", "parallel")),
        name="ada_mod",
    )(c_all, w_ada, b_ada.reshape(depth, 1, n))


def _proj_kernel(x_ref, sh_ref, sc_ref, w_ref, *rest):
    o_ref, xn_ref = rest[-2], rest[-1]

    @pl.when(pl.program_id(2) == 0)
    def _():
        xn_ref[...] = (_rms(x_ref[0]) * (1.0 + sc_ref[0]) + sh_ref[0]).astype(BF16)

    o_ref[0] = jnp.dot(xn_ref[...], w_ref[...], preferred_element_type=F32).astype(o_ref.dtype)


def proj_in(x, x_off, n_tiles, tm, shift, scale, w, out_rows, out_off, prev):
    b, _, d = x.shape
    bm = shift.shape[0]
    nj = w.shape[1] // PROJ_TN

    def mod_map(bi, i, j):
        return (bi if bm > 1 else 0, 0, 0)

    in_specs = [
        pl.BlockSpec((1, tm, d), lambda bi, i, j: (bi, i + x_off, 0)),
        pl.BlockSpec((1, 1, d), mod_map),
        pl.BlockSpec((1, 1, d), mod_map),
        pl.BlockSpec((d, PROJ_TN), lambda bi, i, j: (0, j)),
    ]
    args = [x, shift, scale, w]
    aliases = {}
    if prev is not None:
        in_specs.append(pl.BlockSpec(memory_space=pl.ANY))
        args.append(prev)
        aliases = {4: 0}
    return pl.pallas_call(
        _proj_kernel,
        grid=(b, n_tiles, nj),
        in_specs=in_specs,
        out_specs=pl.BlockSpec((1, tm, PROJ_TN), lambda bi, i, j: (bi, i + out_off, j)),
        out_shape=jax.ShapeDtypeStruct((b, out_rows, w.shape[1]), BF16),
        scratch_shapes=[pltpu.VMEM((tm, d), BF16)],
        input_output_aliases=aliases,
        compiler_params=_cparams(("parallel", "arbitrary", "arbitrary")),
        name="proj_in",
    )(*args)


def _mix_kernel(odf, odb, dg, ogf, ogb, gg, ym, yf, x_ref, g1, sh2, sc2, dnw, glw, w_ref, *rest):
    xo_ref, h_ref = rest[-2], rest[-1]

    def gated(of, ob, gate, wn):
        o = of[0].astype(F32) + ob[0].astype(F32)
        on = jnp.concatenate(
            [_rms(o[:, hh * LANES:(hh + 1) * LANES]) for hh in range(o.shape[1] // LANES)], axis=-1)
        g = gate[0].astype(F32)
        return (on * wn[...] * (g * jax.nn.sigmoid(g))).astype(BF16)

    q = w_ref.shape[0] // 4
    acc = jnp.dot(gated(odf, odb, dg, dnw), w_ref[0:q], preferred_element_type=F32)
    acc += jnp.dot(gated(ogf, ogb, gg, glw), w_ref[q:2 * q], preferred_element_type=F32)
    acc += jnp.dot(ym[0], w_ref[2 * q:3 * q], preferred_element_type=F32)
    acc += jnp.dot(yf[0], w_ref[3 * q:4 * q], preferred_element_type=F32)
    xn = x_ref[0] + g1[0] * acc
    xo_ref[0] = xn
    h_ref[0] = (_rms(xn) * (1.0 + sc2[0]) + sh2[0]).astype(BF16)


def mix_out(o_dn_f, o_dn_b, o_gla_f, o_gla_b, y_mla, y_fn, proj, off, x, x_off, n_tiles, tm,
            g1, sh2, sc2, dn_norm, gla_norm, w_out, out_rows, out_off, prev_x, prev_h):
    b, _, d = x.shape
    bm = g1.shape[0]
    wq = 512

    def mix_spec(col):
        return pl.BlockSpec((1, tm, wq), lambda bi, i: (bi, i + off, col))

    def mod_map(bi, i):
        return (bi if bm > 1 else 0, 0, 0)

    in_specs = [
        mix_spec(0), mix_spec(0), mix_spec(C_DNG // wq),
        mix_spec(0), mix_spec(0), mix_spec(C_GLAG // wq),
        mix_spec(0), mix_spec(0),
        pl.BlockSpec((1, tm, d), lambda bi, i: (bi, i + x_off, 0)),
        pl.BlockSpec((1, 1, d), mod_map), pl.BlockSpec((1, 1, d), mod_map), pl.BlockSpec((1, 1, d), mod_map),
        pl.BlockSpec((1, wq), lambda bi, i: (0, 0)), pl.BlockSpec((1, wq), lambda bi, i: (0, 0)),
        pl.BlockSpec((4 * wq, d), lambda bi, i: (0, 0)),
    ]
    args = [o_dn_f, o_dn_b, proj, o_gla_f, o_gla_b, proj, y_mla, y_fn, x, g1, sh2, sc2,
            jnp.tile(dn_norm.astype(F32), DN_HEADS).reshape(1, wq),
            jnp.tile(gla_norm.astype(F32), GLA_HEADS).reshape(1, wq), w_out]
    aliases = {}
    if prev_x is not None:
        in_specs += [pl.BlockSpec(memory_space=pl.ANY), pl.BlockSpec(memory_space=pl.ANY)]
        args += [prev_x, prev_h]
        aliases = {len(args) - 2: 0, len(args) - 1: 1}
    out_spec = pl.BlockSpec((1, tm, d), lambda bi, i: (bi, i + out_off, 0))
    return pl.pallas_call(
        _mix_kernel,
        grid=(b, n_tiles),
        in_specs=in_specs,
        out_specs=[out_spec, out_spec],
        out_shape=[jax.ShapeDtypeStruct((b, out_rows, d), F32), jax.ShapeDtypeStruct((b, out_rows, d), BF16)],
        input_output_aliases=aliases,
        compiler_params=_cparams(("parallel", "parallel")),
        name="mix_out",
    )(*args)


N_RANK = PEER_TOPK + 1
VAL_ROWS = 24
NEG = -1e30


def _top_values(work, n):
    vals = []
    for _ in range(n):
        m = jnp.max(work, axis=0, keepdims=True)
        vals.append(m)
        work = jnp.where(work >= m, NEG, work)
    return vals


def _peer_score_kernel(h_ref, wq_ref, keys_ref, s_ref, aux_ref, vals_ref):
    tm = h_ref.shape[0]
    n_chunks = tm // LANES
    q = jnp.dot(h_ref[...], wq_ref[...], preferred_element_type=F32).astype(BF16)
    for hp in range(2 * PEER_HEADS):
        s_ref[hp] = lax.dot_general(keys_ref[hp], q[:, hp * PEER_HALF:(hp + 1) * PEER_HALF], NT_DIMS,
                                    preferred_element_type=F32)

    def stage1(hp, carry):
        for c in range(n_chunks):
            lanes = slice(c * LANES, (c + 1) * LANES)
            vals = _top_values(s_ref[hp, :, lanes], N_RANK)
            vals += [jnp.full((1, LANES), NEG, F32)] * (VAL_ROWS - N_RANK)
            vals_ref[hp, :, lanes] = jnp.concatenate(vals, axis=0)
        return carry

    lax.fori_loop(0, 2 * PEER_HEADS, stage1, 0)

    row = lax.broadcasted_iota(jnp.int32, (SUBLANES, LANES), 0)

    def stage2(hh, carry):
        for c in range(n_chunks):
            lanes = slice(c * LANES, (c + 1) * LANES)
            v1 = vals_ref[2 * hh, :, lanes]
            v2 = vals_ref[2 * hh + 1, :, lanes]
            cands = []
            for r1 in range(N_RANK):
                n = N_RANK // (r1 + 1)
                for g in range(-(-n // SUBLANES)):
                    blk = v2[g * SUBLANES:(g + 1) * SUBLANES] + v1[r1:r1 + 1]
                    cands.append(jnp.where(row + g * SUBLANES < n, blk, NEG))
            t0 = v1[0:1] + v2[0:1]
            taken = jnp.zeros((1, LANES), F32)
            z = jnp.zeros((1, LANES), F32)
            t_k = jnp.zeros((1, LANES), F32)
            t_k1 = jnp.zeros((1, LANES), F32)
            for _ in range(N_RANK):
                m8 = functools.reduce(jnp.maximum, cands)
                m = jnp.max(m8, axis=0, keepdims=True)
                cnt8 = functools.reduce(lambda a, b: a + b, [jnp.where(cd == m, 1.0, 0.0) for cd in cands])
                cnt = jnp.sum(cnt8, axis=0, keepdims=True)
                after = taken + cnt
                z = z + jnp.clip(PEER_TOPK - taken, 0.0, cnt) * jnp.exp(m - t0)
                t_k = jnp.where((taken < PEER_TOPK) & (after >= PEER_TOPK), m, t_k)
                t_k1 = jnp.where((taken < N_RANK) & (after >= N_RANK), m, t_k1)
                taken = after
                cands = [jnp.where(cd >= m, NEG, cd) for cd in cands]
            tau = 0.5 * (t_k + t_k1)
            aux = jnp.concatenate([tau, v1[0:1], v2[0:1], 1.0 / z, jnp.zeros((4, LANES), F32)], axis=0)
            aux_ref[hh, :, lanes] = aux
        return carry

    lax.fori_loop(0, PEER_HEADS, stage2, 0)


def peer_scores(h, wq, keys, tm):
    tn, d = h.shape
    nk = 2 * PEER_HEADS
    return pl.pallas_call(
        _peer_score_kernel,
        grid=(tn // tm,),
        in_specs=[
            pl.BlockSpec((tm, d), lambda i: (i, 0)),
            pl.BlockSpec(wq.shape, lambda i: (0, 0)),
            pl.BlockSpec(keys.shape, lambda i: (0, 0, 0)),
        ],
        out_specs=[
            pl.BlockSpec((nk, PEER_N_KEYS, tm), lambda i: (0, 0, i)),
            pl.BlockSpec((PEER_HEADS, SUBLANES, tm), lambda i: (0, 0, i)),
        ],
        out_shape=[jax.ShapeDtypeStruct((nk, PEER_N_KEYS, tn), F32),
                   jax.ShapeDtypeStruct((PEER_HEADS, SUBLANES, tn), F32)],
        scratch_shapes=[pltpu.VMEM((nk, VAL_ROWS, tm), F32)],
        compiler_params=_cparams(("parallel",)),
        name="peer_scores",
    )(h, wq, keys)


PEER_TE = 1024
PEER_G2_ROWS = 256


def _gelu(a):
    return 0.5 * a * (1.0 + lax.erf(a * (1.0 / math.sqrt(2.0))))


def _peer_expert_kernel(h_ref, s_ref, aux_ref, u_ref, vt_ref, x_ref, g2_ref, fn_ref, o_ref,
                        acc_ref, e2_ref, act_ref, wt_ref, *, final):
    e = pl.program_id(1)
    tm = h_ref.shape[0]
    n_chunks = tm // LANES
    n_i = PEER_TE // PEER_N_KEYS

    @pl.when(e == 0)
    def _():
        acc_ref[...] = jnp.zeros_like(acc_ref)
        for hh in range(PEER_HEADS):
            e2_ref[hh] = jnp.exp(s_ref[2 * hh + 1] - aux_ref[hh, 2:3, :]) * aux_ref[hh, 3:4, :]

    act_ref[...] = lax.dot_general(u_ref[...], h_ref[...], NT_DIMS, preferred_element_type=F32)

    i0 = pl.multiple_of(e * n_i, n_i)
    thr, e1 = [], []
    for hh in range(PEER_HEADS):
        s1 = s_ref[2 * hh, pl.ds(i0, n_i), :]
        thr.append(aux_ref[hh, 0:1, :] - s1)
        e1.append(jnp.exp(s1 - aux_ref[hh, 1:2, :]))
    for il in range(n_i):
        rows = slice(il * PEER_N_KEYS, (il + 1) * PEER_N_KEYS)
        for c in range(n_chunks):
            lanes = slice(c * LANES, (c + 1) * LANES)
            g = jnp.zeros((PEER_N_KEYS, LANES), F32)
            for hh in range(PEER_HEADS):
                sel = jnp.where(s_ref[2 * hh + 1, :, lanes] >= thr[hh][il:il + 1, lanes], e2_ref[hh, :, lanes], 0.0)
                g = g + sel * e1[hh][il:il + 1, lanes]
            wt_ref[rows, lanes] = (g * _gelu(act_ref[rows, lanes])).astype(BF16)

    acc_ref[...] += jnp.dot(vt_ref[...], wt_ref[...], preferred_element_type=F32)

    @pl.when(e == pl.num_programs(1) - 1)
    def _():
        y = jnp.transpose(acc_ref[...])
        for r in range(tm // PEER_G2_ROWS):
            rows = slice(r * PEER_G2_ROWS, (r + 1) * PEER_G2_ROWS)
            out = x_ref[rows, :] + g2_ref[r] * y[rows, :]
            if final:
                out = _rms(out) * fn_ref[...]
            o_ref[rows, :] = out


def peer_experts(h, s, aux, u, vt, x, g2_rows, final_norm, tm, final):
    tn, d = h.shape
    ne = u.shape[0]
    nk = 2 * PEER_HEADS
    once = pl.Buffered(1)
    return pl.pallas_call(
        functools.partial(_peer_expert_kernel, final=final),
        grid=(tn // tm, ne // PEER_TE),
        in_specs=[
            pl.BlockSpec((tm, d), lambda i, e: (i, 0), pipeline_mode=once),
            pl.BlockSpec((nk, PEER_N_KEYS, tm), lambda i, e: (0, 0, i), pipeline_mode=once),
            pl.BlockSpec((PEER_HEADS, SUBLANES, tm), lambda i, e: (0, 0, i)),
            pl.BlockSpec((PEER_TE, d), lambda i, e: (e, 0)),
            pl.BlockSpec((d, PEER_TE), lambda i, e: (0, e)),
            pl.BlockSpec((tm, d), lambda i, e: (i, 0), pipeline_mode=once),
            pl.BlockSpec((tm // PEER_G2_ROWS, 1, d), lambda i, e: (i, 0, 0)),
            pl.BlockSpec((1, d), lambda i, e: (0, 0)),
        ],
        out_specs=pl.BlockSpec((tm, d), lambda i, e: (i, 0)),
        out_shape=jax.ShapeDtypeStruct((tn, d), F32),
        scratch_shapes=[
            pltpu.VMEM((d, tm), F32),
            pltpu.VMEM((PEER_HEADS, PEER_N_KEYS, tm), F32),
            pltpu.VMEM((PEER_TE, tm), F32),
            pltpu.VMEM((PEER_TE, tm), BF16),
        ],
        compiler_params=_cparams(("parallel", "arbitrary")),
        name="peer_experts",
    )(h, s, aux, u, vt, x, g2_rows, final_norm.reshape(1, d).astype(F32))


IN_LAYOUT = (
    ("dn_qkv", 1536), ("dn_gate", 512), ("dn_a", 8), ("dn_b", 8), ("gla_q", 256), ("gla_k", 256),
    ("gla_v", 512), ("gla_gate", 512), ("gla_alpha", 32), ("mla_cq", 448), ("mla_ckv", 128),
    ("mla_kr", 64), ("fn", 512),
)


def _in_cols():
    offs, o = {}, 0
    for name, w in IN_LAYOUT:
        offs[name] = (o, o + w)
        o += w
    return offs


def _rot_half_cols(w):
    return jnp.concatenate([-w[:, 16:32], w[:, 0:16], -w[:, 48:64], w[:, 32:48]], axis=1)


def prep_w_in(w_in):
    c = _in_cols()
    d = w_in.shape[0]

    def cols(name):
        a, b = c[name]
        return w_in[:, a:b]

    z = lambda n: jnp.zeros((d, n), w_in.dtype)
    kr = cols("mla_kr")
    pieces = [
        cols("dn_qkv"), cols("dn_gate"), cols("gla_v"), cols("gla_gate"), cols("fn"),
        cols("mla_cq"), z(512 - MLA_Q_RANK), cols("gla_q"), cols("gla_k"), cols("mla_ckv"),
        kr, _rot_half_cols(kr), cols("dn_a"), cols("dn_b"), cols("gla_alpha"), z(128 - 48),
    ]
    w = jnp.concatenate(pieces, axis=1)
    assert w.shape[1] == NP
    return w.astype(BF16)


MLA_SLOT = 256


def prep_mla_weights(q_norm, w_uq, kv_norm, w_ukv):
    scale = (MLA_NOPE + MLA_ROPE) ** -0.5
    wq = w_uq * q_norm[:, None] * scale
    per = MLA_NOPE + MLA_ROPE
    cols = []
    for h in range(MLA_HEADS):
        nope = wq[:, h * per:h * per + MLA_NOPE]
        rope = wq[:, h * per + MLA_NOPE:(h + 1) * per]
        cols += [nope, rope, _rot_half_cols(rope)]
    wq_p = jnp.concatenate(cols, axis=1)
    wq_p = jnp.concatenate([wq_p, jnp.zeros((512 - MLA_Q_RANK, wq_p.shape[1]), wq_p.dtype)], axis=0)
    wkv = w_ukv * kv_norm[:, None]
    per = MLA_NOPE + MLA_V
    wkv_p = jnp.concatenate([wkv[:, h * per:h * per + MLA_NOPE] for h in range(MLA_HEADS)]
                            + [wkv[:, h * per + MLA_NOPE:(h + 1) * per] for h in range(MLA_HEADS)], axis=1)
    return wq_p.astype(BF16), wkv_p.astype(BF16)


def rope_table(l, lc):
    rows = l // GRID_W
    row = np.repeat(np.arange(rows), GRID_W).astype(np.float32)
    col = np.tile(np.arange(GRID_W), rows).astype(np.float32)
    half = MLA_ROPE // 2
    inv_freq = jnp.asarray(ROPE_BASE, F32) ** (-jnp.arange(0, half, 2, dtype=F32) / half)
    ang_r = jnp.asarray(row)[:, None] * inv_freq
    ang_c = jnp.asarray(col)[:, None] * inv_freq
    cos = jnp.concatenate([jnp.cos(ang_r)] * 2 + [jnp.cos(ang_c)] * 2, axis=1)
    sin = jnp.concatenate([jnp.sin(ang_r)] * 2 + [jnp.sin(ang_c)] * 2, axis=1)
    lat = jnp.concatenate([cos, sin], axis=1)
    ctx = jnp.concatenate([jnp.ones((lc, MLA_ROPE), F32), jnp.zeros((lc, MLA_ROPE), F32)], axis=1)
    return jnp.concatenate([lat, ctx], axis=0)


def _mla_prep_kernel(cq_ref, ckv_ref, kr_ref, cs_ref, wq_ref, wkv_ref, q_ref, k_ref, v_ref):
    cs = cs_ref[...]
    half = LANES // 2
    lane = lax.broadcasted_iota(jnp.int32, cs.shape, 1)

    def rotary(u):
        return u + pltpu.roll(u, half, axis=1)

    cq = cq_ref[0].astype(F32)
    cqn = cq * lax.rsqrt(jnp.sum(cq * cq, axis=-1, keepdims=True) * (1.0 / MLA_Q_RANK) + EPS)
    q = jnp.dot(cqn.astype(BF16), wq_ref[...], preferred_element_type=F32)
    qs = []
    for h in range(MLA_HEADS):
        qs.append(q[:, h * MLA_SLOT:h * MLA_SLOT + LANES])
        qs.append(rotary(q[:, h * MLA_SLOT + LANES:(h + 1) * MLA_SLOT] * cs))
    q_ref[0] = jnp.concatenate(qs, axis=-1).astype(BF16)

    kv = jnp.dot(_rms(ckv_ref[0].astype(F32)).astype(BF16), wkv_ref[...], preferred_element_type=F32)
    kr = jnp.where(lane < half, rotary(kr_ref[0].astype(F32) * cs), 0.0)
    ks = []
    for h in range(MLA_HEADS):
        ks += [kv[:, h * LANES:(h + 1) * LANES], kr]
    k_ref[0] = jnp.concatenate(ks, axis=-1).astype(BF16)
    v_ref[0] = kv[:, MLA_HEADS * LANES:].astype(BF16)


def mla_prep(proj, cs, wq_p, wkv_p, tm):
    b, lt, _ = proj.shape
    nq = MLA_HEADS * MLA_SLOT
    return pl.pallas_call(
        _mla_prep_kernel,
        grid=(b, lt // tm),
        in_specs=[
            pl.BlockSpec((1, tm, 512), lambda bi, i: (bi, i, C_CQ // 512)),
            pl.BlockSpec((1, tm, LANES), lambda bi, i: (bi, i, C_CKV // LANES)),
            pl.BlockSpec((1, tm, LANES), lambda bi, i: (bi, i, C_KR // LANES)),
            pl.BlockSpec((tm, LANES), lambda bi, i: (i, 0)),
            pl.BlockSpec(wq_p.shape, lambda bi, i: (0, 0)),
            pl.BlockSpec(wkv_p.shape, lambda bi, i: (0, 0)),
        ],
        out_specs=[
            pl.BlockSpec((1, tm, nq), lambda bi, i: (bi, i, 0)),
            pl.BlockSpec((1, tm, nq), lambda bi, i: (bi, i, 0)),
            pl.BlockSpec((1, tm, MLA_HEADS * MLA_V), lambda bi, i: (bi, i, 0)),
        ],
        out_shape=[jax.ShapeDtypeStruct((b, lt, nq), BF16), jax.ShapeDtypeStruct((b, lt, nq), BF16),
                   jax.ShapeDtypeStruct((b, lt, MLA_HEADS * MLA_V), BF16)],
        compiler_params=_cparams(("parallel", "parallel")),
        name="mla_prep",
    )(proj, proj, proj, cs, wq_p, wkv_p)


def _mla_attn_kernel(q_ref, k_ref, v_ref, *rest):
    o_ref = rest[-1]
    s = lax.dot_general(q_ref[0], k_ref[0], NT_DIMS, preferred_element_type=F32)
    p = jnp.exp(s - jnp.max(s, axis=-1, keepdims=True))
    o = jnp.dot(p.astype(BF16), v_ref[0], preferred_element_type=F32)
    o_ref[0] = (o / jnp.sum(p, axis=-1, keepdims=True)).astype(o_ref.dtype)


def mla_attend(q4, k4, v4, q_off, n_tiles, tq, k_rows, k_blk, prev):
    b, lt, _ = q4.shape
    in_specs = [
        pl.BlockSpec((1, tq, MLA_SLOT), lambda bi, h, i: (bi, i + q_off, h)),
        pl.BlockSpec((1, k_rows, MLA_SLOT), lambda bi, h, i: (bi, k_blk, h)),
        pl.BlockSpec((1, k_rows, MLA_V), lambda bi, h, i: (bi, k_blk, h)),
    ]
    args = [q4, k4, v4]
    aliases = {}
    if prev is not None:
        in_specs.append(pl.BlockSpec(memory_space=pl.ANY))
        args.append(prev)
        aliases = {3: 0}
    return pl.pallas_call(
        _mla_attn_kernel,
        grid=(b, MLA_HEADS, n_tiles),
        in_specs=in_specs,
        out_specs=pl.BlockSpec((1, tq, MLA_V), lambda bi, h, i: (bi, i + q_off, h)),
        out_shape=jax.ShapeDtypeStruct((b, lt, MLA_HEADS * MLA_V), BF16),
        input_output_aliases=aliases,
        compiler_params=_cparams(("parallel", "parallel", "parallel")),
        name="mla_attend",
    )(*args)


def dft_tables(n, scale):
    j = np.arange(n, dtype=np.int64)
    root = int(round(math.sqrt(n))) if int(round(math.sqrt(n))) ** 2 == n else 1
    if root == 1:
        ph = 2.0 * np.pi * ((j[:, None] * j[None, :]) % n) / n
        return jnp.asarray(np.cos(ph) * scale, BF16), jnp.asarray(np.sin(ph) * scale, BF16)
    u = np.arange(root, dtype=np.int64)
    pa = 2.0 * np.pi * ((j[:, None] * (root * u)[None, :]) % n) / n
    pb = 2.0 * np.pi * ((j[:, None] * u[None, :]) % n) / n
    ca, sa, cb, sb = (jnp.asarray(t, F32) for t in (np.cos(pa), np.sin(pa), np.cos(pb) * scale, np.sin(pb) * scale))
    cos = ca[:, :, None] * cb[:, None, :] - sa[:, :, None] * sb[:, None, :]
    sin = sa[:, :, None] * cb[:, None, :] + ca[:, :, None] * sb[:, None, :]
    return cos.reshape(n, n).astype(BF16), sin.reshape(n, n).astype(BF16)


def channel_dft_tables():
    c, s = dft_tables(FN_CH, 1.0)
    eye = jnp.eye(FN_GROUPS, dtype=F32)
    return (jnp.kron(eye, c.astype(F32)).astype(BF16), jnp.kron(eye, s.astype(F32)).astype(BF16))


def _fourier_kernel(z_ref, cl_ref, sl_ref, cc_ref, sc_ref, *rest):
    o_ref = rest[-1]
    z = z_ref[0]
    re = jnp.dot(cl_ref[...], z, preferred_element_type=F32).astype(BF16)
    im = jnp.dot(sl_ref[...], z, preferred_element_type=F32).astype(BF16)
    y = jnp.dot(re, cc_ref[...], preferred_element_type=F32) - jnp.dot(im, sc_ref[...], preferred_element_type=F32)
    o_ref[0] = y.astype(o_ref.dtype)


def fourier_mix(proj, seq_blk, n, cl, sl, cc, sc, prev):
    b, lt, _ = proj.shape
    w = FN_GROUPS * FN_CH
    tm = min(512, n)
    nt = n // tm
    in_specs = [
        pl.BlockSpec((1, n, w), lambda i, bi: (bi, seq_blk, C_FN // w)),
        pl.BlockSpec((tm, n), lambda i, bi: (i, 0)),
        pl.BlockSpec((tm, n), lambda i, bi: (i, 0)),
        pl.BlockSpec((w, w), lambda i, bi: (0, 0)),
        pl.BlockSpec((w, w), lambda i, bi: (0, 0)),
    ]
    args = [proj, cl, sl, cc, sc]
    aliases = {}
    if prev is not None:
        in_specs.append(pl.BlockSpec(memory_space=pl.ANY))
        args.append(prev)
        aliases = {5: 0}
    return pl.pallas_call(
        _fourier_kernel,
        grid=(nt, b),
        in_specs=in_specs,
        out_specs=pl.BlockSpec((1, tm, w), lambda i, bi: (bi, seq_blk * nt + i, 0)),
        out_shape=jax.ShapeDtypeStruct((b, lt, w), BF16),
        input_output_aliases=aliases,
        compiler_params=_cparams(("parallel", "parallel")),
        name="fourier_mix",
    )(*args)


CHUNK = 64


def _scan_maps(l, lc):
    n_lat, n_ctx = l // CHUNK, lc // CHUNK

    def fwd(n):
        return jnp.where(n < n_ctx, n + n_lat, n - n_ctx)

    def rev(n):
        return n_lat + n_ctx - 1 - n

    return n_lat + n_ctx, fwd, rev


def _order_masks(size, blocks, reverse):
    n = size * blocks
    r = lax.broadcasted_iota(jnp.int32, (n, n), 0)
    c = lax.broadcasted_iota(jnp.int32, (n, n), 1)
    same = (r // size) == (c // size)
    if reverse:
        return same & (c >= r), same & (c > r)
    return same & (c <= r), same & (c < r)


def _dot_exact_lhs(m, x):
    mb = m.astype(BF16)
    hi = x.astype(BF16)
    r1 = x - hi.astype(F32)
    mid = r1.astype(BF16)
    lo = (r1 - mid.astype(F32)).astype(BF16)
    out = jnp.dot(mb, hi, preferred_element_type=F32)
    out += jnp.dot(mb, mid, preferred_element_type=F32)
    return out + jnp.dot(mb, lo, preferred_element_type=F32)


def _gla_kernel(qf, kf, vf, af, qr, kr, vr, ar, w2_ref, b2_ref, of_ref, or_ref, st_ref):
    @pl.when(pl.program_id(1) == 0)
    def _():
        st_ref[...] = jnp.zeros_like(st_ref)

    nq = GLA_HEADS * GLA_DK
    nv = GLA_HEADS * GLA_DV
    lane_q = lax.broadcasted_iota(jnp.int32, (CHUNK, nq), 1)
    bd = (lax.broadcasted_iota(jnp.int32, (nv, nq), 0) // GLA_DV) == (lax.broadcasted_iota(jnp.int32, (nv, nq), 1) // GLA_DK)

    for d, (q_ref, k_ref, v_ref, a_ref, o_ref) in enumerate(((qf, kf, vf, af, of_ref), (qr, kr, vr, ar, or_ref))):
        inc, _ = _order_masks(CHUNK, 1, d == 1)
        last = 0 if d == 1 else CHUNK - 1
        z = jnp.dot(a_ref[0].astype(F32), w2_ref[d], preferred_element_type=F32,
                    precision=lax.Precision.HIGHEST) + b2_ref[d]
        log_a = jax.nn.log_sigmoid(z) * (1.0 / GLA_TAU)
        bcum = _dot_exact_lhs(inc, log_a)
        b_last = bcum[last:last + 1, :]
        q = q_ref[0].astype(F32) * (GLA_DK ** -0.5)
        k = k_ref[0].astype(F32)
        v = v_ref[0]
        q_in = (q * jnp.exp(bcum)).astype(BF16)
        q_rel = q * jnp.exp(bcum - b_last)
        k_rel = (k * jnp.exp(b_last - bcum)).astype(BF16)
        st = st_ref[d]
        o_inter = lax.dot_general(q_in, st.astype(BF16), NT_DIMS, preferred_element_type=F32)
        outs = []
        for h in range(GLA_HEADS):
            qh = jnp.where(lane_q // GLA_DK == h, q_rel, 0.0).astype(BF16)
            att = jnp.where(inc, lax.dot_general(qh, k_rel, NT_DIMS, preferred_element_type=F32), 0.0)
            outs.append(jnp.dot(att.astype(BF16), v[:, h * GLA_DV:(h + 1) * GLA_DV], preferred_element_type=F32))
        o_ref[0] = (jnp.concatenate(outs, axis=-1) + o_inter).astype(o_ref.dtype)
        upd = lax.dot_general(v, k_rel, TN_DIMS, preferred_element_type=F32)
        st_ref[d] = st * jnp.exp(b_last) + jnp.where(bd, upd, 0.0)


def gla_scan(proj, w2, b2, l, lc):
    b, lt, _ = proj.shape
    n_steps, fwd, rev = _scan_maps(l, lc)
    nq, nv = GLA_HEADS * GLA_DK, GLA_HEADS * GLA_DV

    def specs(cmap):
        return [
            pl.BlockSpec((1, CHUNK, nq), lambda bi, n: (bi, cmap(n), C_GLAQ // nq)),
            pl.BlockSpec((1, CHUNK, nq), lambda bi, n: (bi, cmap(n), C_GLAK // nq)),
            pl.BlockSpec((1, CHUNK, nv), lambda bi, n: (bi, cmap(n), C_GLAV // nv)),
            pl.BlockSpec((1, CHUNK, LANES), lambda bi, n: (bi, cmap(n), C_SMALL // LANES)),
        ]

    w2p = jnp.zeros((2, LANES, nq), F32)
    for d in range(2):
        lo = 2 * DN_HEADS * 2 + GLA_RANK * d
        w2p = w2p.at[d, lo:lo + GLA_RANK].set(w2[d].astype(F32))
    out_sd = jax.ShapeDtypeStruct((b, lt, nv), BF16)
    return pl.pallas_call(
        _gla_kernel,
        grid=(b, n_steps),
        in_specs=specs(fwd) + specs(rev) + [
            pl.BlockSpec((2, LANES, nq), lambda bi, n: (0, 0, 0)),
            pl.BlockSpec((2, 1, nq), lambda bi, n: (0, 0, 0)),
        ],
        out_specs=[pl.BlockSpec((1, CHUNK, nv), lambda bi, n: (bi, fwd(n), 0)),
                   pl.BlockSpec((1, CHUNK, nv), lambda bi, n: (bi, rev(n), 0))],
        out_shape=[out_sd, out_sd],
        scratch_shapes=[pltpu.VMEM((2, nv, nq), F32)],
        compiler_params=_cparams(("parallel", "arbitrary")),
        name="gla_scan",
    )(*([proj] * 8), w2p, b2.astype(F32).reshape(2, 1, nq))


DN_TL = 256
DN_W = DN_HEADS * DN_DK


def _dn_conv_kernel(x_ref, p_ref, n_ref, w_ref, o_ref, *, first_tiles, last_tiles):
    i = pl.program_id(1)
    tl = x_ref.shape[1]
    is_first = functools.reduce(jnp.logical_or, [i == t for t in first_tiles])
    is_last = functools.reduce(jnp.logical_or, [i == t for t in last_tiles])
    prev = jnp.where(is_first, 0.0, p_ref[0].astype(F32))
    nxt = jnp.where(is_last, 0.0, n_ref[0].astype(F32))
    ext = jnp.concatenate([prev, x_ref[0].astype(F32), nxt], axis=0)
    pad = DN_CONV // 2
    acc = jnp.zeros((tl, ext.shape[1]), F32)
    for t in range(DN_CONV):
        start = SUBLANES - pad + t
        acc = acc + ext[start:start + tl, :] * w_ref[t:t + 1, :]
    y = acc * jax.nn.sigmoid(acc)
    outs = []
    for h in range(3 * DN_HEADS):
        yh = y[:, h * DN_DK:(h + 1) * DN_DK]
        if h < 2 * DN_HEADS:
            yh = yh * lax.rsqrt(jnp.sum(yh * yh, axis=-1, keepdims=True) + EPS)
            if h < DN_HEADS:
                yh = yh * (DN_DK ** -0.5)
        outs.append(yh)
    o_ref[0] = jnp.concatenate(outs, axis=-1).astype(o_ref.dtype)


def dn_conv_norm(proj, conv_w, l, lc):
    b, lt, _ = proj.shape
    w = 3 * DN_W
    tl = DN_TL
    hb = tl // SUBLANES
    n_halo = lt // SUBLANES
    first_tiles = (0, l // tl)
    last_tiles = (l // tl - 1, lt // tl - 1)
    return pl.pallas_call(
        functools.partial(_dn_conv_kernel, first_tiles=first_tiles, last_tiles=last_tiles),
        grid=(b, lt // tl),
        in_specs=[
            pl.BlockSpec((1, tl, w), lambda bi, i: (bi, i, 0)),
            pl.BlockSpec((1, SUBLANES, w), lambda bi, i: (bi, jnp.maximum(i * hb - 1, 0), 0)),
            pl.BlockSpec((1, SUBLANES, w), lambda bi, i: (bi, jnp.minimum((i + 1) * hb, n_halo - 1), 0)),
            pl.BlockSpec((DN_CONV, w), lambda bi, i: (0, 0)),
        ],
        out_specs=pl.BlockSpec((1, tl, w), lambda bi, i: (bi, i, 0)),
        out_shape=jax.ShapeDtypeStruct((b, lt, w), BF16),
        compiler_params=_cparams(("parallel", "parallel")),
        name="dn_conv",
    )(proj, proj, proj, conv_w.astype(F32))


def _stack_heads(x, width):
    return jnp.concatenate([x[:, h * width:(h + 1) * width] for h in range(DN_HEADS)], axis=0)


def _dn_scan_kernel(xf, af, xr, ar, gp_ref, of_ref, or_ref, s_ref):
    @pl.when(pl.program_id(1) == 0)
    def _():
        s_ref[...] = jnp.zeros_like(s_ref)

    n = DN_HEADS * CHUNK
    eye = lax.broadcasted_iota(jnp.int32, (n, n), 0) == lax.broadcasted_iota(jnp.int32, (n, n), 1)
    r = lax.broadcasted_iota(jnp.int32, (n, n), 0)
    c = lax.broadcasted_iota(jnp.int32, (n, n), 1)

    for d, (x_ref, a_ref, o_ref) in enumerate(((xf, af, of_ref), (xr, ar, or_ref))):
        rev = d == 1
        inc, strict = _order_masks(CHUNK, DN_HEADS, rev)
        last = 0 if rev else CHUNK - 1
        x = x_ref[0].astype(F32)
        qs = _stack_heads(x[:, 0:DN_W], DN_DK)
        ks = _stack_heads(x[:, DN_W:2 * DN_W], DN_DK)
        vs = _stack_heads(x[:, 2 * DN_W:3 * DN_W], DN_DV)
        a = a_ref[0].astype(F32)
        g8 = gp_ref[0:1, :] * jax.nn.softplus(a + gp_ref[1:2, :])
        beta8 = jax.nn.sigmoid(a)
        g_col = jnp.concatenate([g8[:, DN_HEADS * d + h:DN_HEADS * d + h + 1] for h in range(DN_HEADS)], axis=0)
        b_col = jnp.concatenate([beta8[:, 2 * DN_HEADS + DN_HEADS * d + h:2 * DN_HEADS + DN_HEADS * d + h + 1]
                                 for h in range(DN_HEADS)], axis=0)
        g_b = jnp.broadcast_to(g_col, (n, n))
        beta_b = jnp.broadcast_to(b_col, (n, n))
        diff = _dot_exact_lhs(inc, jnp.where(strict, g_b, 0.0))
        decay = jnp.where(inc, jnp.exp(diff), 0.0)
        gc = _dot_exact_lhs(inc, g_b[:, :DN_DK])
        kb = ks.astype(BF16)
        kk = lax.dot_general(kb, kb, NT_DIMS, preferred_element_type=F32)
        a_mat = jnp.where(strict, beta_b * kk * decay, 0.0)
        inv = jnp.where(eye, 1.0, 0.0) - jnp.where(_pair_mask(r, c, 1, rev), a_mat, 0.0)
        size = 2
        while size < CHUNK:
            off = jnp.where(_pair_mask(r, c, size, rev), a_mat, 0.0).astype(BF16)
            t = jnp.dot(inv.astype(BF16), off, preferred_element_type=F32)
            inv = inv - jnp.dot(t.astype(BF16), inv.astype(BF16), preferred_element_type=F32)
            size *= 2
        beta_k = beta_b[:, :DN_DK]
        e_gc = jnp.exp(gc)
        invb = inv.astype(BF16)
        u = jnp.dot(invb, (vs * beta_k).astype(BF16), preferred_element_type=F32)
        w = jnp.dot(invb, (ks * beta_k * e_gc).astype(BF16), preferred_element_type=F32).astype(BF16)
        p = jnp.where(inc, lax.dot_general(qs.astype(BF16), kb, NT_DIMS, preferred_element_type=F32) * decay, 0.0)
        q_dec = (qs * e_gc).astype(BF16)
        gl = jnp.concatenate([jnp.broadcast_to(gc[h * CHUNK + last:h * CHUNK + last + 1, :], (CHUNK, DN_DK))
                              for h in range(DN_HEADS)], axis=0)
        k_dec = (ks * jnp.exp(gl - gc)).astype(BF16)
        v_new, o_state = [], []
        for h in range(DN_HEADS):
            rows = slice(h * CHUNK, (h + 1) * CHUNK)
            sb = s_ref[d, h].astype(BF16)
            v_new.append(u[rows] - jnp.dot(w[rows], sb, preferred_element_type=F32))
            o_state.append(jnp.dot(q_dec[rows], sb, preferred_element_type=F32))
        v_new = jnp.concatenate(v_new, axis=0).astype(BF16)
        o_st = jnp.dot(p.astype(BF16), v_new, preferred_element_type=F32)
        outs = []
        for h in range(DN_HEADS):
            rows = slice(h * CHUNK, (h + 1) * CHUNK)
            outs.append(o_st[rows] + o_state[h])
            e_last = jnp.exp(gc[h * CHUNK + last:h * CHUNK + last + 1, :])
            s_ref[d, h] = s_ref[d, h] * e_last + lax.dot_general(k_dec[rows], v_new[rows], TN_DIMS,
                                                                  preferred_element_type=F32)
        o_ref[0] = jnp.concatenate(outs, axis=-1).astype(o_ref.dtype)


def _pair_mask(r, c, size, reverse):
    same = (r // (2 * size)) == (c // (2 * size))
    lo_r, lo_c = (r % (2 * size)) < size, (c % (2 * size)) < size
    if reverse:
        return same & lo_r & ~lo_c
    return same & ~lo_r & lo_c


def dn_scan(qkv, proj, a_log, dt_bias, l, lc):
    b, lt, _ = qkv.shape
    n_steps, fwd, rev = _scan_maps(l, lc)
    w = 3 * DN_W
    gp = jnp.zeros((SUBLANES, LANES), F32)
    gp = gp.at[0, :2 * DN_HEADS].set(-jnp.exp(a_log.astype(F32)).reshape(-1))
    gp = gp.at[1, :2 * DN_HEADS].set(dt_bias.astype(F32).reshape(-1))

    def specs(cmap):
        return [pl.BlockSpec((1, CHUNK, w), lambda bi, n: (bi, cmap(n), 0)),
                pl.BlockSpec((1, CHUNK, LANES), lambda bi, n: (bi, cmap(n), C_SMALL // LANES))]

    out_sd = jax.ShapeDtypeStruct((b, lt, DN_HEADS * DN_DV), BF16)
    return pl.pallas_call(
        _dn_scan_kernel,
        grid=(b, n_steps),
        in_specs=specs(fwd) + specs(rev) + [pl.BlockSpec((SUBLANES, LANES), lambda bi, n: (0, 0))],
        out_specs=[pl.BlockSpec((1, CHUNK, DN_HEADS * DN_DV), lambda bi, n: (bi, fwd(n), 0)),
                   pl.BlockSpec((1, CHUNK, DN_HEADS * DN_DV), lambda bi, n: (bi, rev(n), 0))],
        out_shape=[out_sd, out_sd],
        scratch_shapes=[pltpu.VMEM((2, DN_HEADS, DN_DK, DN_DV), F32)],
        compiler_params=_cparams(("parallel", "arbitrary")),
        name="dn_scan",
    )(qkv, proj, qkv, proj, gp)


CTX_TM = 256


def kernel(x, c, ctx, c_ctx, w_ada, b_ada, w_in, dn_conv, dn_a_log, dn_dt_bias, dn_norm, gla_w2, gla_b2,
           gla_norm, mla_q_norm, mla_w_uq, mla_kv_norm, mla_w_ukv, w_out, peer_wq, peer_keys, peer_u, peer_v,
           final_norm):
    b, l, d = x.shape
    lc = ctx.shape[1]
    lt = l + lc
    depth = w_ada.shape[0]
    lat_tm = min(1024, l)
    mix_tm = min(512, l)
    peer_tm = 512
    assert lc == CTX_TM and l % lat_tm == 0 and l % CTX_TM == 0

    pad = (-(b + 1)) % SUBLANES
    c_all = jnp.concatenate([c, c_ctx[None, :], jnp.zeros((pad, d), F32)], axis=0)
    mod = ada_mod(c_all, w_ada, b_ada)

    cs = rope_table(l, lc)
    cc, sc = channel_dft_tables()
    cl_lat, sl_lat = dft_tables(l, (l * FN_CH) ** -0.5)
    cl_ctx, sl_ctx = dft_tables(lc, (lc * FN_CH) ** -0.5)
    ctx_blk = l // lc

    x_lat, x_lat_off, x_ctx, x_ctx_off = x, 0, ctx, 0
    out = None
    for li in range(depth):
        last = li == depth - 1
        m_lat = [mod[li, :b, k * d:(k + 1) * d].reshape(b, 1, d) for k in range(N_MOD)]
        m_ctx = [mod[li, b:b + 1, k * d:(k + 1) * d].reshape(1, 1, d) for k in range(N_MOD)]

        w_p = prep_w_in(w_in[li])
        proj = proj_in(x_lat, x_lat_off, l // lat_tm, lat_tm, m_lat[0], m_lat[1], w_p, lt, 0, None)
        proj = proj_in(x_ctx, x_ctx_off, 1, CTX_TM, m_ctx[0], m_ctx[1], w_p, lt, l // CTX_TM, proj)

        wq_p, wkv_p = prep_mla_weights(mla_q_norm[li], mla_w_uq[li], mla_kv_norm[li], mla_w_ukv[li])
        q4, k4, v4 = mla_prep(proj, cs, wq_p, wkv_p, CTX_TM)
        y_mla = mla_attend(q4, k4, v4, 0, l // CTX_TM, CTX_TM, lt, 0, None)
        y_fn = fourier_mix(proj, 0, l, cl_lat, sl_lat, cc, sc, None)
        if not last:
            y_mla = mla_attend(q4, k4, v4, l // CTX_TM, lc // CTX_TM, CTX_TM, lc, ctx_blk, y_mla)
            y_fn = fourier_mix(proj, ctx_blk, lc, cl_ctx, sl_ctx, cc, sc, y_fn)
        gla_f, gla_b = gla_scan(proj, gla_w2[li], gla_b2[li], l, lc)
        qkv = dn_conv_norm(proj, dn_conv[li], l, lc)
        dn_f, dn_b = dn_scan(qkv, proj, dn_a_log[li], dn_dt_bias[li], l, lc)

        w_o = w_out[li].astype(BF16)
        rows = l if last else lt
        xn, h = mix_out(dn_f, dn_b, gla_f, gla_b, y_mla, y_fn, proj, 0,
                        x_lat, x_lat_off, l // mix_tm, mix_tm, m_lat[2], m_lat[3], m_lat[4],
                        dn_norm[li], gla_norm[li], w_o, rows, 0, None, None)
        if not last:
            xn, h = mix_out(dn_f, dn_b, gla_f, gla_b, y_mla, y_fn, proj,
                            l // CTX_TM, x_ctx, x_ctx_off, 1, CTX_TM, m_ctx[2], m_ctx[3], m_ctx[4],
                            dn_norm[li], gla_norm[li], w_o, rows, l // CTX_TM, xn, h)

        tn = b * rows
        g2_lat = jnp.broadcast_to(m_lat[5], (b, l // PEER_G2_ROWS, d))
        if last:
            g2_rows = g2_lat.reshape(tn // PEER_G2_ROWS, 1, d)
        else:
            g2_ctx = jnp.broadcast_to(m_ctx[5], (b, lc // PEER_G2_ROWS, d))
            g2_rows = jnp.concatenate([g2_lat, g2_ctx], axis=1).reshape(tn // PEER_G2_ROWS, 1, d)
        hf = h.reshape(tn, d)
        s, aux = peer_scores(hf, peer_wq[li].astype(BF16),
                             peer_keys[li].reshape(2 * PEER_HEADS, PEER_N_KEYS, PEER_HALF).astype(BF16), peer_tm)
        y = peer_experts(hf, s, aux, peer_u[li].astype(BF16), peer_v[li].T.astype(BF16), xn.reshape(tn, d),
                         g2_rows, final_norm, peer_tm, last)
        out = y.reshape(b, rows, d)
        x_lat, x_lat_off, x_ctx, x_ctx_off = out, 0, out, l // CTX_TM
    return out
```

```python
import functools
import math

import numpy as np
import jax
import jax.numpy as jnp
from jax import lax
from jax.experimental import pallas as pl
from jax.experimental.pallas import tpu as pltpu

F32 = jnp.float32
BF16 = jnp.bfloat16

EPS = 1e-6
N_MOD = 6
GRID_W = 64
ROPE_BASE = 10000.0

DN_HEADS, DN_DK, DN_DV, DN_CONV, DN_CHUNK = 4, 128, 128, 5, 64
GLA_HEADS, GLA_DK, GLA_DV, GLA_RANK, GLA_TAU, GLA_CHUNK = 4, 64, 128, 16, 16.0, 64
MLA_HEADS, MLA_NOPE, MLA_ROPE, MLA_V, MLA_Q_RANK, MLA_KV_RANK = 4, 128, 64, 128, 448, 128
FN_GROUPS, FN_CH = 4, 128
PEER_HEADS, PEER_N_KEYS, PEER_HALF, PEER_TOPK = 8, 128, 128, 16

VMEM_LIMIT_BYTES = 56 * 1024 * 1024
LANES = 128
SUBLANES = 8

C_DNQ, C_DNK, C_DNV, C_DNG = 0, 512, 1024, 1536
C_GLAV, C_GLAG, C_FN, C_CQ = 2048, 2560, 3072, 3584
C_GLAQ, C_GLAK = 4096, 4352
C_CKV, C_KR, C_SMALL = 4608, 4736, 4864
NP = 4992
PROJ_TN = NP // 3

NT_DIMS = (((1,), (1,)), ((), ()))
TN_DIMS = (((0,), (0,)), ((), ()))


def _cparams(sem):
    return pltpu.CompilerParams(dimension_semantics=sem, vmem_limit_bytes=VMEM_LIMIT_BYTES)


def _rms(x):
    return x * lax.rsqrt(jnp.mean(x * x, axis=-1, keepdims=True) + EPS)


def _ada_kernel(c_ref, w_ref, b_ref, o_ref):
    c = c_ref[...]
    a = (c * jax.nn.sigmoid(c)).astype(BF16)
    o_ref[0] = jnp.dot(a, w_ref[0].astype(BF16), preferred_element_type=F32) + b_ref[0]


def ada_mod(c_all, w_ada, b_ada):
    depth, d, n = w_ada.shape
    rows = c_all.shape[0]
    tn = 1024
    return pl.pallas_call(
        _ada_kernel,
        grid=(depth, n // tn),
        in_specs=[
            pl.BlockSpec((rows, d), lambda l, j: (0, 0)),
            pl.BlockSpec((1, d, tn), lambda l, j: (l, 0, j)),
            pl.BlockSpec((1, 1, tn), lambda l, j: (l, 0, j)),
        ],
        out_specs=pl.BlockSpec((1, rows, tn), lambda l, j: (l, 0, j)),
        out_shape=jax.ShapeDtypeStruct((depth, rows, n), F32),
        compiler_params=_cparams(("parallel", "parallel")),
        name="ada_mod",
    )(c_all, w_ada, b_ada.reshape(depth, 1, n))


def _proj_kernel(x_ref, sh_ref, sc_ref, w_ref, *rest):
    o_ref, xn_ref = rest[-2], rest[-1]

    @pl.when(pl.program_id(2) == 0)
    def _():
        xn_ref[...] = (_rms(x_ref[0]) * (1.0 + sc_ref[0]) + sh_ref[0]).astype(BF16)

    o_ref[0] = jnp.dot(xn_ref[...], w_ref[...], preferred_element_type=F32).astype(o_ref.dtype)


def proj_in(x, x_off, n_tiles, tm, shift, scale, w, out_rows, out_off, prev):
    b, _, d = x.shape
    bm = shift.shape[0]
    nj = w.shape[1] // PROJ_TN

    def mod_map(bi, i, j):
        return (bi if bm > 1 else 0, 0, 0)

    in_specs = [
        pl.BlockSpec((1, tm, d), lambda bi, i, j: (bi, i + x_off, 0)),
        pl.BlockSpec((1, 1, d), mod_map),
        pl.BlockSpec((1, 1, d), mod_map),
        pl.BlockSpec((d, PROJ_TN), lambda bi, i, j: (0, j)),
    ]
    args = [x, shift, scale, w]
    aliases = {}
    if prev is not None:
        in_specs.append(pl.BlockSpec(memory_space=pl.ANY))
        args.append(prev)
        aliases = {4: 0}
    return pl.pallas_call(
        _proj_kernel,
        grid=(b, n_tiles, nj),
        in_specs=in_specs,
        out_specs=pl.BlockSpec((1, tm, PROJ_TN), lambda bi, i, j: (bi, i + out_off, j)),
        out_shape=jax.ShapeDtypeStruct((b, out_rows, w.shape[1]), BF16),
        scratch_shapes=[pltpu.VMEM((tm, d), BF16)],
        input_output_aliases=aliases,
        compiler_params=_cparams(("parallel", "arbitrary", "arbitrary")),
        name="proj_in",
    )(*args)


def _mix_kernel(odf, odb, dg, ogf, ogb, gg, ym, yf, x_ref, g1, sh2, sc2, dnw, glw, w_ref, *rest):
    xo_ref, h_ref = rest[-2], rest[-1]

    def gated(of, ob, gate, wn):
        o = of[0].astype(F32) + ob[0].astype(F32)
        on = jnp.concatenate(
            [_rms(o[:, hh * LANES:(hh + 1) * LANES]) for hh in range(o.shape[1] // LANES)], axis=-1)
        g = gate[0].astype(F32)
        return (on * wn[...] * (g * jax.nn.sigmoid(g))).astype(BF16)

    q = w_ref.shape[0] // 4
    acc = jnp.dot(gated(odf, odb, dg, dnw), w_ref[0:q], preferred_element_type=F32)
    acc += jnp.dot(gated(ogf, ogb, gg, glw), w_ref[q:2 * q], preferred_element_type=F32)
    acc += jnp.dot(ym[0], w_ref[2 * q:3 * q], preferred_element_type=F32)
    acc += jnp.dot(yf[0], w_ref[3 * q:4 * q], preferred_element_type=F32)
    xn = x_ref[0] + g1[0] * acc
    xo_ref[0] = xn
    h_ref[0] = (_rms(xn) * (1.0 + sc2[0]) + sh2[0]).astype(BF16)


def mix_out(o_dn_f, o_dn_b, o_gla_f, o_gla_b, y_mla, y_fn, proj, off, x, x_off, n_tiles, tm,
            g1, sh2, sc2, dn_norm, gla_norm, w_out, out_rows, out_off, prev_x, prev_h):
    b, _, d = x.shape
    bm = g1.shape[0]
    wq = 512

    def mix_spec(col):
        return pl.BlockSpec((1, tm, wq), lambda bi, i: (bi, i + off, col))

    def mod_map(bi, i):
        return (bi if bm > 1 else 0, 0, 0)

    in_specs = [
        mix_spec(0), mix_spec(0), mix_spec(C_DNG // wq),
        mix_spec(0), mix_spec(0), mix_spec(C_GLAG // wq),
        mix_spec(0), mix_spec(0),
        pl.BlockSpec((1, tm, d), lambda bi, i: (bi, i + x_off, 0)),
        pl.BlockSpec((1, 1, d), mod_map), pl.BlockSpec((1, 1, d), mod_map), pl.BlockSpec((1, 1, d), mod_map),
        pl.BlockSpec((1, wq), lambda bi, i: (0, 0)), pl.BlockSpec((1, wq), lambda bi, i: (0, 0)),
        pl.BlockSpec((4 * wq, d), lambda bi, i: (0, 0)),
    ]
    args = [o_dn_f, o_dn_b, proj, o_gla_f, o_gla_b, proj, y_mla, y_fn, x, g1, sh2, sc2,
            jnp.tile(dn_norm.astype(F32), DN_HEADS).reshape(1, wq),
            jnp.tile(gla_norm.astype(F32), GLA_HEADS).reshape(1, wq), w_out]
    aliases = {}
    if prev_x is not None:
        in_specs += [pl.BlockSpec(memory_space=pl.ANY), pl.BlockSpec(memory_space=pl.ANY)]
        args += [prev_x, prev_h]
        aliases = {len(args) - 2: 0, len(args) - 1: 1}
    out_spec = pl.BlockSpec((1, tm, d), lambda bi, i: (bi, i + out_off, 0))
    return pl.pallas_call(
        _mix_kernel,
        grid=(b, n_tiles),
        in_specs=in_specs,
        out_specs=[out_spec, out_spec],
        out_shape=[jax.ShapeDtypeStruct((b, out_rows, d), F32), jax.ShapeDtypeStruct((b, out_rows, d), BF16)],
        input_output_aliases=aliases,
        compiler_params=_cparams(("parallel", "parallel")),
        name="mix_out",
    )(*args)


N_RANK = PEER_TOPK + 1
VAL_ROWS = 24
NEG = -1e30


def _candidate_tiles():
    runs = []
    for r1 in range(N_RANK):
        n = N_RANK // (r1 + 1)
        for r2_0 in range(0, n, SUBLANES):
            runs.append((r1, r2_0, min(SUBLANES, n - r2_0)))
    runs.sort(key=lambda t: -t[2])
    tiles, used = [], []
    for r1, r2_0, length in runs:
        for i in range(len(tiles)):
            if used[i] + length <= SUBLANES:
                tiles[i].append((used[i], r1, r2_0, length))
                used[i] += length
                break
        else:
            tiles.append([(0, r1, r2_0, length)])
            used.append(length)
    return tiles


_CAND_TILES = _candidate_tiles()


def _top_values(work, n):
    vals = []
    for _ in range(n):
        m = jnp.max(work, axis=0, keepdims=True)
        vals.append(m)
        work = jnp.where(work >= m, NEG, work)
    return vals


def _peer_score_kernel(h_ref, wq_ref, keys_ref, s_ref, aux_ref, vals_ref):
    tm = h_ref.shape[0]
    n_chunks = tm // LANES
    q = jnp.dot(h_ref[...], wq_ref[...], preferred_element_type=F32).astype(BF16)
    for hp in range(2 * PEER_HEADS):
        s_ref[hp] = lax.dot_general(keys_ref[hp], q[:, hp * PEER_HALF:(hp + 1) * PEER_HALF], NT_DIMS,
                                    preferred_element_type=F32)

    def stage1(hp, carry):
        for c in range(n_chunks):
            lanes = slice(c * LANES, (c + 1) * LANES)
            vals = _top_values(s_ref[hp, :, lanes], N_RANK)
            vals += [jnp.full((1, LANES), NEG, F32)] * (VAL_ROWS - N_RANK)
            vals_ref[hp, :, lanes] = jnp.concatenate(vals, axis=0)
        return carry

    lax.fori_loop(0, 2 * PEER_HEADS, stage1, 0)

    row = lax.broadcasted_iota(jnp.int32, (SUBLANES, LANES), 0)

    def stage2(hh, carry):
        for c in range(n_chunks):
            lanes = slice(c * LANES, (c + 1) * LANES)
            v1 = vals_ref[2 * hh, :, lanes]
            v2 = vals_ref[2 * hh + 1, :, lanes]
            cands = []
            for segments in _CAND_TILES:
                tile = jnp.full((SUBLANES, LANES), NEG, F32)
                for row0, r1, r2_0, length in segments:
                    blk, off = r2_0 // SUBLANES, r2_0 % SUBLANES
                    src = v2[blk * SUBLANES:(blk + 1) * SUBLANES]
                    shift = (row0 - off) % SUBLANES
                    if shift:
                        src = pltpu.roll(src, shift, axis=0)
                    tile = jnp.where((row >= row0) & (row < row0 + length), src + v1[r1:r1 + 1], tile)
                cands.append(tile)
            t0 = v1[0:1] + v2[0:1]
            taken = jnp.zeros((1, LANES), F32)
            z = jnp.zeros((1, LANES), F32)
            t_k = jnp.zeros((1, LANES), F32)
            t_k1 = jnp.zeros((1, LANES), F32)
            for _ in range(N_RANK):
                m8 = functools.reduce(jnp.maximum, cands)
                m = jnp.max(m8, axis=0, keepdims=True)
                cnt8 = functools.reduce(lambda a, b: a + b, [jnp.where(cd == m, 1.0, 0.0) for cd in cands])
                cnt = jnp.sum(cnt8, axis=0, keepdims=True)
                after = taken + cnt
                z = z + jnp.clip(PEER_TOPK - taken, 0.0, cnt) * jnp.exp(m - t0)
                t_k = jnp.where((taken < PEER_TOPK) & (after >= PEER_TOPK), m, t_k)
                t_k1 = jnp.where((taken < N_RANK) & (after >= N_RANK), m, t_k1)
                taken = after
                cands = [jnp.where(cd >= m, NEG, cd) for cd in cands]
            tau = 0.5 * (t_k + t_k1)
            aux = jnp.concatenate([tau, v1[0:1], v2[0:1], 1.0 / z, jnp.zeros((4, LANES), F32)], axis=0)
            aux_ref[hh, :, lanes] = aux
        return carry

    lax.fori_loop(0, PEER_HEADS, stage2, 0)


def peer_scores(h, wq, keys, tm):
    tn, d = h.shape
    nk = 2 * PEER_HEADS
    return pl.pallas_call(
        _peer_score_kernel,
        grid=(tn // tm,),
        in_specs=[
            pl.BlockSpec((tm, d), lambda i: (i, 0)),
            pl.BlockSpec(wq.shape, lambda i: (0, 0)),
            pl.BlockSpec(keys.shape, lambda i: (0, 0, 0)),
        ],
        out_specs=[
            pl.BlockSpec((nk, PEER_N_KEYS, tm), lambda i: (0, 0, i)),
            pl.BlockSpec((PEER_HEADS, SUBLANES, tm), lambda i: (0, 0, i)),
        ],
        out_shape=[jax.ShapeDtypeStruct((nk, PEER_N_KEYS, tn), F32),
                   jax.ShapeDtypeStruct((PEER_HEADS, SUBLANES, tn), F32)],
        scratch_shapes=[pltpu.VMEM((nk, VAL_ROWS, tm), F32)],
        compiler_params=_cparams(("parallel",)),
        name="peer_scores",
    )(h, wq, keys)


PEER_TE = 1024
PEER_G2_ROWS = 256


def _gelu(a):
    return 0.5 * a * (1.0 + lax.erf(a * (1.0 / math.sqrt(2.0))))


def _peer_expert_kernel(h_ref, s_ref, aux_ref, u_ref, vt_ref, x_ref, g2_ref, fn_ref, o_ref,
                        acc_ref, e2_ref, act_ref, wt_ref, *, final):
    e = pl.program_id(1)
    tm = h_ref.shape[0]
    n_chunks = tm // LANES
    n_i = PEER_TE // PEER_N_KEYS

    @pl.when(e == 0)
    def _():
        acc_ref[...] = jnp.zeros_like(acc_ref)
        for hh in range(PEER_HEADS):
            e2_ref[hh] = jnp.exp(s_ref[2 * hh + 1] - aux_ref[hh, 2:3, :]) * aux_ref[hh, 3:4, :]

    act_ref[...] = lax.dot_general(u_ref[...], h_ref[...], NT_DIMS, preferred_element_type=F32)

    i0 = pl.multiple_of(e * n_i, n_i)
    thr, e1 = [], []
    for hh in range(PEER_HEADS):
        s1 = s_ref[2 * hh, pl.ds(i0, n_i), :]
        thr.append(aux_ref[hh, 0:1, :] - s1)
        e1.append(jnp.exp(s1 - aux_ref[hh, 1:2, :]))
    for il in range(n_i):
        rows = slice(il * PEER_N_KEYS, (il + 1) * PEER_N_KEYS)
        for c in range(n_chunks):
            lanes = slice(c * LANES, (c + 1) * LANES)
            g = jnp.zeros((PEER_N_KEYS, LANES), F32)
            for hh in range(PEER_HEADS):
                sel = jnp.where(s_ref[2 * hh + 1, :, lanes] >= thr[hh][il:il + 1, lanes], e2_ref[hh, :, lanes], 0.0)
                g = g + sel * e1[hh][il:il + 1, lanes]
            wt_ref[rows, lanes] = (g * _gelu(act_ref[rows, lanes])).astype(BF16)

    acc_ref[...] += jnp.dot(vt_ref[...], wt_ref[...], preferred_element_type=F32)

    @pl.when(e == pl.num_programs(1) - 1)
    def _():
        y = jnp.transpose(acc_ref[...])
        for r in range(tm // PEER_G2_ROWS):
            rows = slice(r * PEER_G2_ROWS, (r + 1) * PEER_G2_ROWS)
            out = x_ref[rows, :] + g2_ref[r] * y[rows, :]
            if final:
                out = _rms(out) * fn_ref[...]
            o_ref[rows, :] = out


def peer_experts(h, s, aux, u, vt, x, g2_rows, final_norm, tm, final):
    tn, d = h.shape
    ne = u.shape[0]
    nk = 2 * PEER_HEADS
    once = pl.Buffered(1)
    return pl.pallas_call(
        functools.partial(_peer_expert_kernel, final=final),
        grid=(tn // tm, ne // PEER_TE),
        in_specs=[
            pl.BlockSpec((tm, d), lambda i, e: (i, 0), pipeline_mode=once),
            pl.BlockSpec((nk, PEER_N_KEYS, tm), lambda i, e: (0, 0, i), pipeline_mode=once),
            pl.BlockSpec((PEER_HEADS, SUBLANES, tm), lambda i, e: (0, 0, i)),
            pl.BlockSpec((PEER_TE, d), lambda i, e: (e, 0)),
            pl.BlockSpec((d, PEER_TE), lambda i, e: (0, e)),
            pl.BlockSpec((tm, d), lambda i, e: (i, 0), pipeline_mode=once),
            pl.BlockSpec((tm // PEER_G2_ROWS, 1, d), lambda i, e: (i, 0, 0)),
            pl.BlockSpec((1, d), lambda i, e: (0, 0)),
        ],
        out_specs=pl.BlockSpec((tm, d), lambda i, e: (i, 0)),
        out_shape=jax.ShapeDtypeStruct((tn, d), F32),
        scratch_shapes=[
            pltpu.VMEM((d, tm), F32),
            pltpu.VMEM((PEER_HEADS, PEER_N_KEYS, tm), F32),
            pltpu.VMEM((PEER_TE, tm), F32),
            pltpu.VMEM((PEER_TE, tm), BF16),
        ],
        compiler_params=_cparams(("parallel", "arbitrary")),
        name="peer_experts",
    )(h, s, aux, u, vt, x, g2_rows, final_norm.reshape(1, d).astype(F32))


IN_LAYOUT = (
    ("dn_qkv", 1536), ("dn_gate", 512), ("dn_a", 8), ("dn_b", 8), ("gla_q", 256), ("gla_k", 256),
    ("gla_v", 512), ("gla_gate", 512), ("gla_alpha", 32), ("mla_cq", 448), ("mla_ckv", 128),
    ("mla_kr", 64), ("fn", 512),
)


def _in_cols():
    offs, o = {}, 0
    for name, w in IN_LAYOUT:
        offs[name] = (o, o + w)
        o += w
    return offs


def _rot_half_cols(w):
    return jnp.concatenate([-w[:, 16:32], w[:, 0:16], -w[:, 48:64], w[:, 32:48]], axis=1)


def prep_w_in(w_in):
    c = _in_cols()
    d = w_in.shape[0]

    def cols(name):
        a, b = c[name]
        return w_in[:, a:b]

    z = lambda n: jnp.zeros((d, n), w_in.dtype)
    kr = cols("mla_kr")
    pieces = [
        cols("dn_qkv"), cols("dn_gate"), cols("gla_v"), cols("gla_gate"), cols("fn"),
        cols("mla_cq"), z(512 - MLA_Q_RANK), cols("gla_q"), cols("gla_k"), cols("mla_ckv"),
        kr, _rot_half_cols(kr), cols("dn_a"), cols("dn_b"), cols("gla_alpha"), z(128 - 48),
    ]
    w = jnp.concatenate(pieces, axis=1)
    assert w.shape[1] == NP
    return w.astype(BF16)


MLA_SLOT = 256


def prep_mla_weights(q_norm, w_uq, kv_norm, w_ukv):
    scale = (MLA_NOPE + MLA_ROPE) ** -0.5
    wq = w_uq * q_norm[:, None] * scale
    per = MLA_NOPE + MLA_ROPE
    cols = []
    for h in range(MLA_HEADS):
        nope = wq[:, h * per:h * per + MLA_NOPE]
        rope = wq[:, h * per + MLA_NOPE:(h + 1) * per]
        cols += [nope, rope, _rot_half_cols(rope)]
    wq_p = jnp.concatenate(cols, axis=1)
    wq_p = jnp.concatenate([wq_p, jnp.zeros((512 - MLA_Q_RANK, wq_p.shape[1]), wq_p.dtype)], axis=0)
    wkv = w_ukv * kv_norm[:, None]
    per = MLA_NOPE + MLA_V
    wkv_p = jnp.concatenate([wkv[:, h * per:h * per + MLA_NOPE] for h in range(MLA_HEADS)]
                            + [wkv[:, h * per + MLA_NOPE:(h + 1) * per] for h in range(MLA_HEADS)], axis=1)
    return wq_p.astype(BF16), wkv_p.astype(BF16)


def rope_table(l, lc):
    rows = l // GRID_W
    row = np.repeat(np.arange(rows), GRID_W).astype(np.float32)
    col = np.tile(np.arange(GRID_W), rows).astype(np.float32)
    half = MLA_ROPE // 2
    inv_freq = jnp.asarray(ROPE_BASE, F32) ** (-jnp.arange(0, half, 2, dtype=F32) / half)
    ang_r = jnp.asarray(row)[:, None] * inv_freq
    ang_c = jnp.asarray(col)[:, None] * inv_freq
    cos = jnp.concatenate([jnp.cos(ang_r)] * 2 + [jnp.cos(ang_c)] * 2, axis=1)
    sin = jnp.concatenate([jnp.sin(ang_r)] * 2 + [jnp.sin(ang_c)] * 2, axis=1)
    lat = jnp.concatenate([cos, sin], axis=1)
    ctx = jnp.concatenate([jnp.ones((lc, MLA_ROPE), F32), jnp.zeros((lc, MLA_ROPE), F32)], axis=1)
    return jnp.concatenate([lat, ctx], axis=0)


def _mla_prep_kernel(cq_ref, ckv_ref, kr_ref, cs_ref, wq_ref, wkv_ref, q_ref, k_ref, v_ref):
    cs = cs_ref[...]
    half = LANES // 2
    lane = lax.broadcasted_iota(jnp.int32, cs.shape, 1)

    def rotary(u):
        return u + pltpu.roll(u, half, axis=1)

    cq = cq_ref[0].astype(F32)
    cqn = cq * lax.rsqrt(jnp.sum(cq * cq, axis=-1, keepdims=True) * (1.0 / MLA_Q_RANK) + EPS)
    q = jnp.dot(cqn.astype(BF16), wq_ref[...], preferred_element_type=F32)
    qs = []
    for h in range(MLA_HEADS):
        qs.append(q[:, h * MLA_SLOT:h * MLA_SLOT + LANES])
        qs.append(rotary(q[:, h * MLA_SLOT + LANES:(h + 1) * MLA_SLOT] * cs))
    q_ref[0] = jnp.concatenate(qs, axis=-1).astype(BF16)

    kv = jnp.dot(_rms(ckv_ref[0].astype(F32)).astype(BF16), wkv_ref[...], preferred_element_type=F32)
    kr = jnp.where(lane < half, rotary(kr_ref[0].astype(F32) * cs), 0.0)
    ks = []
    for h in range(MLA_HEADS):
        ks += [kv[:, h * LANES:(h + 1) * LANES], kr]
    k_ref[0] = jnp.concatenate(ks, axis=-1).astype(BF16)
    v_ref[0] = kv[:, MLA_HEADS * LANES:].astype(BF16)


def mla_prep(proj, cs, wq_p, wkv_p, tm):
    b, lt, _ = proj.shape
    nq = MLA_HEADS * MLA_SLOT
    return pl.pallas_call(
        _mla_prep_kernel,
        grid=(b, lt // tm),
        in_specs=[
            pl.BlockSpec((1, tm, 512), lambda bi, i: (bi, i, C_CQ // 512)),
            pl.BlockSpec((1, tm, LANES), lambda bi, i: (bi, i, C_CKV // LANES)),
            pl.BlockSpec((1, tm, LANES), lambda bi, i: (bi, i, C_KR // LANES)),
            pl.BlockSpec((tm, LANES), lambda bi, i: (i, 0)),
            pl.BlockSpec(wq_p.shape, lambda bi, i: (0, 0)),
            pl.BlockSpec(wkv_p.shape, lambda bi, i: (0, 0)),
        ],
        out_specs=[
            pl.BlockSpec((1, tm, nq), lambda bi, i: (bi, i, 0)),
            pl.BlockSpec((1, tm, nq), lambda bi, i: (bi, i, 0)),
            pl.BlockSpec((1, tm, MLA_HEADS * MLA_V), lambda bi, i: (bi, i, 0)),
        ],
        out_shape=[jax.ShapeDtypeStruct((b, lt, nq), BF16), jax.ShapeDtypeStruct((b, lt, nq), BF16),
                   jax.ShapeDtypeStruct((b, lt, MLA_HEADS * MLA_V), BF16)],
        compiler_params=_cparams(("parallel", "parallel")),
        name="mla_prep",
    )(proj, proj, proj, cs, wq_p, wkv_p)


def _mla_attn_kernel(q_ref, k_ref, v_ref, *rest):
    o_ref = rest[-1]
    s = lax.dot_general(q_ref[0], k_ref[0], NT_DIMS, preferred_element_type=F32)
    p = jnp.exp(s - jnp.max(s, axis=-1, keepdims=True))
    o = jnp.dot(p.astype(BF16), v_ref[0], preferred_element_type=F32)
    o_ref[0] = (o / jnp.sum(p, axis=-1, keepdims=True)).astype(o_ref.dtype)


def mla_attend(q4, k4, v4, q_off, n_tiles, tq, k_rows, k_blk, prev):
    b, lt, _ = q4.shape
    in_specs = [
        pl.BlockSpec((1, tq, MLA_SLOT), lambda bi, h, i: (bi, i + q_off, h)),
        pl.BlockSpec((1, k_rows, MLA_SLOT), lambda bi, h, i: (bi, k_blk, h)),
        pl.BlockSpec((1, k_rows, MLA_V), lambda bi, h, i: (bi, k_blk, h)),
    ]
    args = [q4, k4, v4]
    aliases = {}
    if prev is not None:
        in_specs.append(pl.BlockSpec(memory_space=pl.ANY))
        args.append(prev)
        aliases = {3: 0}
    return pl.pallas_call(
        _mla_attn_kernel,
        grid=(b, MLA_HEADS, n_tiles),
        in_specs=in_specs,
        out_specs=pl.BlockSpec((1, tq, MLA_V), lambda bi, h, i: (bi, i + q_off, h)),
        out_shape=jax.ShapeDtypeStruct((b, lt, MLA_HEADS * MLA_V), BF16),
        input_output_aliases=aliases,
        compiler_params=_cparams(("parallel", "parallel", "parallel")),
        name="mla_attend",
    )(*args)


def dft_tables(n, scale):
    j = np.arange(n, dtype=np.int64)
    root = int(round(math.sqrt(n))) if int(round(math.sqrt(n))) ** 2 == n else 1
    if root == 1:
        ph = 2.0 * np.pi * ((j[:, None] * j[None, :]) % n) / n
        return jnp.asarray(np.cos(ph) * scale, BF16), jnp.asarray(np.sin(ph) * scale, BF16)
    u = np.arange(root, dtype=np.int64)
    pa = 2.0 * np.pi * ((j[:, None] * (root * u)[None, :]) % n) / n
    pb = 2.0 * np.pi * ((j[:, None] * u[None, :]) % n) / n
    ca, sa, cb, sb = (jnp.asarray(t, F32) for t in (np.cos(pa), np.sin(pa), np.cos(pb) * scale, np.sin(pb) * scale))
    cos = ca[:, :, None] * cb[:, None, :] - sa[:, :, None] * sb[:, None, :]
    sin = sa[:, :, None] * cb[:, None, :] + ca[:, :, None] * sb[:, None, :]
    return cos.reshape(n, n).astype(BF16), sin.reshape(n, n).astype(BF16)


def channel_dft_tables():
    c, s = dft_tables(FN_CH, 1.0)
    eye = jnp.eye(FN_GROUPS, dtype=F32)
    return (jnp.kron(eye, c.astype(F32)).astype(BF16), jnp.kron(eye, s.astype(F32)).astype(BF16))


def _fourier_kernel(z_ref, cl_ref, sl_ref, cc_ref, sc_ref, *rest):
    o_ref = rest[-1]
    z = z_ref[0]
    re = jnp.dot(cl_ref[...], z, preferred_element_type=F32).astype(BF16)
    im = jnp.dot(sl_ref[...], z, preferred_element_type=F32).astype(BF16)
    y = jnp.dot(re, cc_ref[...], preferred_element_type=F32) - jnp.dot(im, sc_ref[...], preferred_element_type=F32)
    o_ref[0] = y.astype(o_ref.dtype)


def fourier_mix(proj, seq_blk, n, cl, sl, cc, sc, prev):
    b, lt, _ = proj.shape
    w = FN_GROUPS * FN_CH
    tm = min(512, n)
    nt = n // tm
    in_specs = [
        pl.BlockSpec((1, n, w), lambda i, bi: (bi, seq_blk, C_FN // w)),
        pl.BlockSpec((tm, n), lambda i, bi: (i, 0)),
        pl.BlockSpec((tm, n), lambda i, bi: (i, 0)),
        pl.BlockSpec((w, w), lambda i, bi: (0, 0)),
        pl.BlockSpec((w, w), lambda i, bi: (0, 0)),
    ]
    args = [proj, cl, sl, cc, sc]
    aliases = {}
    if prev is not None:
        in_specs.append(pl.BlockSpec(memory_space=pl.ANY))
        args.append(prev)
        aliases = {5: 0}
    return pl.pallas_call(
        _fourier_kernel,
        grid=(nt, b),
        in_specs=in_specs,
        out_specs=pl.BlockSpec((1, tm, w), lambda i, bi: (bi, seq_blk * nt + i, 0)),
        out_shape=jax.ShapeDtypeStruct((b, lt, w), BF16),
        input_output_aliases=aliases,
        compiler_params=_cparams(("parallel", "parallel")),
        name="fourier_mix",
    )(*args)


CHUNK = 64


def _scan_maps(l, lc):
    n_lat, n_ctx = l // CHUNK, lc // CHUNK

    def fwd(n):
        return jnp.where(n < n_ctx, n + n_lat, n - n_ctx)

    def rev(n):
        return n_lat + n_ctx - 1 - n

    return n_lat + n_ctx, fwd, rev


def _order_masks(size, blocks, reverse):
    n = size * blocks
    r = lax.broadcasted_iota(jnp.int32, (n, n), 0)
    c = lax.broadcasted_iota(jnp.int32, (n, n), 1)
    same = (r // size) == (c // size)
    if reverse:
        return same & (c >= r), same & (c > r)
    return same & (c <= r), same & (c < r)


def _dot_exact_lhs(m, x):
    mb = m.astype(BF16)
    hi = x.astype(BF16)
    r1 = x - hi.astype(F32)
    mid = r1.astype(BF16)
    lo = (r1 - mid.astype(F32)).astype(BF16)
    out = jnp.dot(mb, hi, preferred_element_type=F32)
    out += jnp.dot(mb, mid, preferred_element_type=F32)
    return out + jnp.dot(mb, lo, preferred_element_type=F32)


def _gla_kernel(qf, kf, vf, af, qr, kr, vr, ar, w2_ref, b2_ref, of_ref, or_ref, st_ref):
    @pl.when(pl.program_id(1) == 0)
    def _():
        st_ref[...] = jnp.zeros_like(st_ref)

    nq = GLA_HEADS * GLA_DK
    nv = GLA_HEADS * GLA_DV
    lane_q = lax.broadcasted_iota(jnp.int32, (CHUNK, nq), 1)
    bd = (lax.broadcasted_iota(jnp.int32, (nv, nq), 0) // GLA_DV) == (lax.broadcasted_iota(jnp.int32, (nv, nq), 1) // GLA_DK)

    dirs = (0, 1)
    q_refs, k_refs, v_refs, a_refs, o_refs = (qf, qr), (kf, kr), (vf, vr), (af, ar), (of_ref, or_ref)
    inc = [_order_masks(CHUNK, 1, d == 1)[0] for d in dirs]
    last = [CHUNK - 1, 0]
    z = [jnp.dot(a_refs[d][0].astype(F32), w2_ref[d], preferred_element_type=F32,
                 precision=lax.Precision.HIGHEST) + b2_ref[d] for d in dirs]
    log_a = [jax.nn.log_sigmoid(z[d]) * (1.0 / GLA_TAU) for d in dirs]
    bcum = [_dot_exact_lhs(inc[d], log_a[d]) for d in dirs]
    b_last = [bcum[d][last[d]:last[d] + 1, :] for d in dirs]
    q = [q_refs[d][0].astype(F32) * (GLA_DK ** -0.5) for d in dirs]
    k = [k_refs[d][0].astype(F32) for d in dirs]
    v = [v_refs[d][0] for d in dirs]
    q_in = [(q[d] * jnp.exp(bcum[d])).astype(BF16) for d in dirs]
    q_rel = [q[d] * jnp.exp(bcum[d] - b_last[d]) for d in dirs]
    k_rel = [(k[d] * jnp.exp(b_last[d] - bcum[d])).astype(BF16) for d in dirs]
    st = [st_ref[d] for d in dirs]
    o_inter = [lax.dot_general(q_in[d], st[d].astype(BF16), NT_DIMS, preferred_element_type=F32) for d in dirs]
    outs = [[], []]
    for h in range(GLA_HEADS):
        qh = [jnp.where(lane_q // GLA_DK == h, q_rel[d], 0.0).astype(BF16) for d in dirs]
        att = [jnp.where(inc[d], lax.dot_general(qh[d], k_rel[d], NT_DIMS, preferred_element_type=F32), 0.0)
               for d in dirs]
        for d in dirs:
            outs[d].append(jnp.dot(att[d].astype(BF16), v[d][:, h * GLA_DV:(h + 1) * GLA_DV],
                                   preferred_element_type=F32))
    upd = [lax.dot_general(v[d], k_rel[d], TN_DIMS, preferred_element_type=F32) for d in dirs]
    for d in dirs:
        o_refs[d][0] = (jnp.concatenate(outs[d], axis=-1) + o_inter[d]).astype(o_refs[d].dtype)
        st_ref[d] = st[d] * jnp.exp(b_last[d]) + jnp.where(bd, upd[d], 0.0)


def gla_scan(proj, w2, b2, l, lc):
    b, lt, _ = proj.shape
    n_steps, fwd, rev = _scan_maps(l, lc)
    nq, nv = GLA_HEADS * GLA_DK, GLA_HEADS * GLA_DV

    def specs(cmap):
        return [
            pl.BlockSpec((1, CHUNK, nq), lambda bi, n: (bi, cmap(n), C_GLAQ // nq)),
            pl.BlockSpec((1, CHUNK, nq), lambda bi, n: (bi, cmap(n), C_GLAK // nq)),
            pl.BlockSpec((1, CHUNK, nv), lambda bi, n: (bi, cmap(n), C_GLAV // nv)),
            pl.BlockSpec((1, CHUNK, LANES), lambda bi, n: (bi, cmap(n), C_SMALL // LANES)),
        ]

    w2p = jnp.zeros((2, LANES, nq), F32)
    for d in range(2):
        lo = 2 * DN_HEADS * 2 + GLA_RANK * d
        w2p = w2p.at[d, lo:lo + GLA_RANK].set(w2[d].astype(F32))
    out_sd = jax.ShapeDtypeStruct((b, lt, nv), BF16)
    return pl.pallas_call(
        _gla_kernel,
        grid=(b, n_steps),
        in_specs=specs(fwd) + specs(rev) + [
            pl.BlockSpec((2, LANES, nq), lambda bi, n: (0, 0, 0)),
            pl.BlockSpec((2, 1, nq), lambda bi, n: (0, 0, 0)),
        ],
        out_specs=[pl.BlockSpec((1, CHUNK, nv), lambda bi, n: (bi, fwd(n), 0)),
                   pl.BlockSpec((1, CHUNK, nv), lambda bi, n: (bi, rev(n), 0))],
        out_shape=[out_sd, out_sd],
        scratch_shapes=[pltpu.VMEM((2, nv, nq), F32)],
        compiler_params=_cparams(("parallel", "arbitrary")),
        name="gla_scan",
    )(*([proj] * 8), w2p, b2.astype(F32).reshape(2, 1, nq))


DN_TL = 256
DN_W = DN_HEADS * DN_DK


def _dn_conv_kernel(x_ref, p_ref, n_ref, w_ref, o_ref, *, first_tiles, last_tiles):
    i = pl.program_id(1)
    tl = x_ref.shape[1]
    is_first = functools.reduce(jnp.logical_or, [i == t for t in first_tiles])
    is_last = functools.reduce(jnp.logical_or, [i == t for t in last_tiles])
    prev = jnp.where(is_first, 0.0, p_ref[0].astype(F32))
    nxt = jnp.where(is_last, 0.0, n_ref[0].astype(F32))
    ext = jnp.concatenate([prev, x_ref[0].astype(F32), nxt], axis=0)
    pad = DN_CONV // 2
    acc = jnp.zeros((tl, ext.shape[1]), F32)
    for t in range(DN_CONV):
        start = SUBLANES - pad + t
        acc = acc + ext[start:start + tl, :] * w_ref[t:t + 1, :]
    y = acc * jax.nn.sigmoid(acc)
    outs = []
    for h in range(3 * DN_HEADS):
        yh = y[:, h * DN_DK:(h + 1) * DN_DK]
        if h < 2 * DN_HEADS:
            yh = yh * lax.rsqrt(jnp.sum(yh * yh, axis=-1, keepdims=True) + EPS)
            if h < DN_HEADS:
                yh = yh * (DN_DK ** -0.5)
        outs.append(yh)
    o_ref[0] = jnp.concatenate(outs, axis=-1).astype(o_ref.dtype)


def dn_conv_norm(proj, conv_w, l, lc):
    b, lt, _ = proj.shape
    w = 3 * DN_W
    tl = DN_TL
    hb = tl // SUBLANES
    n_halo = lt // SUBLANES
    first_tiles = (0, l // tl)
    last_tiles = (l // tl - 1, lt // tl - 1)
    return pl.pallas_call(
        functools.partial(_dn_conv_kernel, first_tiles=first_tiles, last_tiles=last_tiles),
        grid=(b, lt // tl),
        in_specs=[
            pl.BlockSpec((1, tl, w), lambda bi, i: (bi, i, 0)),
            pl.BlockSpec((1, SUBLANES, w), lambda bi, i: (bi, jnp.maximum(i * hb - 1, 0), 0)),
            pl.BlockSpec((1, SUBLANES, w), lambda bi, i: (bi, jnp.minimum((i + 1) * hb, n_halo - 1), 0)),
            pl.BlockSpec((DN_CONV, w), lambda bi, i: (0, 0)),
        ],
        out_specs=pl.BlockSpec((1, tl, w), lambda bi, i: (bi, i, 0)),
        out_shape=jax.ShapeDtypeStruct((b, lt, w), BF16),
        compiler_params=_cparams(("parallel", "parallel")),
        name="dn_conv",
    )(proj, proj, proj, conv_w.astype(F32))


def _stack_heads(x, width):
    return jnp.concatenate([x[:, h * width:(h + 1) * width] for h in range(DN_HEADS)], axis=0)


DN_PREP_CHUNKS = 2
DN_N = DN_HEADS * CHUNK


BMM_DIMS = (((2,), (1,)), ((0,), (0,)))
BMM_NT_DIMS = (((2,), (2,)), ((0,), (0,)))


def _bmm(a, b, dims=BMM_DIMS):
    return lax.dot_general(a, b, dims, preferred_element_type=F32)


def _bmm_exact_lhs(m, x):
    mb = m.astype(BF16)
    hi = x.astype(BF16)
    r1 = x - hi.astype(F32)
    mid = r1.astype(BF16)
    lo = (r1 - mid.astype(F32)).astype(BF16)
    return _bmm(mb, hi) + _bmm(mb, mid) + _bmm(mb, lo)


def dn_masks():
    n = DN_N
    r, c = np.meshgrid(np.arange(n), np.arange(n), indexing="ij")
    same = (r // CHUNK) == (c // CHUNK)
    kinds = [[r == c] * 2, [same & (c <= r), same & (c >= r)], [same & (c < r), same & (c > r)]]
    size = 1
    while size < CHUNK:
        blk = (r // (2 * size)) == (c // (2 * size))
        lo_r, lo_c = (r % (2 * size)) < size, (c % (2 * size)) < size
        kinds.append([blk & ~lo_r & lo_c, blk & lo_r & ~lo_c])
        size *= 2
    m = np.stack([np.stack([k[it % 2] for it in range(2 * DN_PREP_CHUNKS)]) for k in kinds])
    return jnp.asarray(m.astype(np.float32), BF16)


def _dn_prep_kernel(x_ref, a_ref, gp_ref, m_ref, *out_refs):
    n = DN_N
    nb = 2 * DN_PREP_CHUNKS
    shape = (nb, n, n)
    eye, inc, strict = m_ref[0], m_ref[1], m_ref[2]

    def pair_mask(size):
        return m_ref[3 + int(math.log2(size))]

    qs, ks, vs, g_col, b_col = [], [], [], [], []
    for j in range(DN_PREP_CHUNKS):
        rows = slice(j * CHUNK, (j + 1) * CHUNK)
        x = x_ref[0, rows, :].astype(F32)
        a = a_ref[0, rows, :].astype(F32)
        g8 = gp_ref[0:1, :] * jax.nn.softplus(a + gp_ref[1:2, :])
        beta8 = jax.nn.sigmoid(a)
        for d in range(2):
            qs.append(_stack_heads(x[:, 0:DN_W], DN_DK))
            ks.append(_stack_heads(x[:, DN_W:2 * DN_W], DN_DK))
            vs.append(_stack_heads(x[:, 2 * DN_W:3 * DN_W], DN_DV))
            lo = DN_HEADS * d
            g_col.append(jnp.concatenate([g8[:, lo + h:lo + h + 1] for h in range(DN_HEADS)], axis=0))
            lo = 2 * DN_HEADS + DN_HEADS * d
            b_col.append(jnp.concatenate([beta8[:, lo + h:lo + h + 1] for h in range(DN_HEADS)], axis=0))
    qs, ks, vs = jnp.stack(qs), jnp.stack(ks), jnp.stack(vs)
    g_b = jnp.broadcast_to(jnp.stack(g_col), shape)
    beta_b = jnp.broadcast_to(jnp.stack(b_col), shape)
    diff = _bmm_exact_lhs(inc, g_b * strict.astype(F32))
    decay = jnp.exp(diff) * inc.astype(F32)
    gc = _bmm_exact_lhs(inc, g_b[:, :, :DN_DK])
    kb = ks.astype(BF16)
    kk = _bmm(kb, kb, BMM_NT_DIMS)
    a_mat = beta_b * kk * decay * strict.astype(F32)
    a_b = a_mat.astype(BF16)
    inv = eye.astype(F32) - a_mat * pair_mask(1).astype(F32)
    size = 2
    while size < CHUNK:
        t = _bmm(inv.astype(BF16), a_b * pair_mask(size))
        inv = inv - _bmm(t.astype(BF16), inv.astype(BF16))
        size *= 2
    beta_k = beta_b[:, :, :DN_DK]
    e_gc = jnp.exp(gc)
    invb = inv.astype(BF16)
    u = _bmm(invb, (vs * beta_k).astype(BF16))
    w = _bmm(invb, (ks * beta_k * e_gc).astype(BF16))
    p = (_bmm(qs.astype(BF16), kb, BMM_NT_DIMS) * decay).astype(BF16)
    q_dec = qs * e_gc
    for j in range(DN_PREP_CHUNKS):
        for d in range(2):
            it = 2 * j + d
            last = 0 if d == 1 else CHUNK - 1
            g_end = [gc[it, h * CHUNK + last:h * CHUNK + last + 1, :] for h in range(DN_HEADS)]
            gl = jnp.concatenate([jnp.broadcast_to(g, (CHUNK, DN_DK)) for g in g_end], axis=0)
            k_dec = ks[it] * jnp.exp(gl - gc[it])
            uw_ref, qk_ref, p_ref, e_ref = out_refs[4 * d:4 * d + 4]
            uw_ref[0, j] = jnp.concatenate([u[it], w[it]], axis=-1).astype(BF16)
            qk_ref[0, j] = jnp.concatenate([q_dec[it], k_dec], axis=-1).astype(BF16)
            p_ref[0, j] = p[it]
            e_ref[0, j] = jnp.exp(jnp.concatenate(g_end + [jnp.zeros((SUBLANES - DN_HEADS, DN_DK), F32)], axis=0))


def dn_prep(qkv, proj, a_log, dt_bias):
    b, lt, _ = qkv.shape
    w = 3 * DN_W
    n_chunks = lt // CHUNK
    gp = jnp.zeros((SUBLANES, LANES), F32)
    gp = gp.at[0, :2 * DN_HEADS].set(-jnp.exp(a_log.astype(F32)).reshape(-1))
    gp = gp.at[1, :2 * DN_HEADS].set(dt_bias.astype(F32).reshape(-1))
    rows = DN_PREP_CHUNKS * CHUNK
    big = jax.ShapeDtypeStruct((b, n_chunks, DN_N, 2 * DN_DK), BF16)
    small = jax.ShapeDtypeStruct((b, n_chunks, SUBLANES, DN_DK), F32)
    big_spec = pl.BlockSpec((1, DN_PREP_CHUNKS, DN_N, 2 * DN_DK), lambda bi, i: (bi, i, 0, 0))
    small_spec = pl.BlockSpec((1, DN_PREP_CHUNKS, SUBLANES, DN_DK), lambda bi, i: (bi, i, 0, 0))
    masks = dn_masks()
    return pl.pallas_call(
        _dn_prep_kernel,
        grid=(b, n_chunks // DN_PREP_CHUNKS),
        in_specs=[pl.BlockSpec((1, rows, w), lambda bi, i: (bi, i, 0)),
                  pl.BlockSpec((1, rows, LANES), lambda bi, i: (bi, i, C_SMALL // LANES)),
                  pl.BlockSpec((SUBLANES, LANES), lambda bi, i: (0, 0)),
                  pl.BlockSpec(masks.shape, lambda bi, i: (0, 0, 0, 0))],
        out_specs=[big_spec, big_spec, big_spec, small_spec] * 2,
        out_shape=[big, big, big, small] * 2,
        compiler_params=_cparams(("parallel", "parallel")),
        name="dn_prep",
    )(qkv, proj, gp, masks)


def _dn_state_kernel(*refs):
    (uwf, qkf, pf, ef, uwr, qkr, pr, er, of_ref, or_ref, s_ref) = refs

    @pl.when(pl.program_id(1) == 0)
    def _():
        s_ref[...] = jnp.zeros_like(s_ref)

    for d, (uw_ref, qk_ref, p_ref, e_ref, o_ref) in enumerate(((uwf, qkf, pf, ef, of_ref), (uwr, qkr, pr, er, or_ref))):
        uw = uw_ref[0, 0]
        qk = qk_ref[0, 0]
        v_new, o_state = [], []
        for h in range(DN_HEADS):
            rows = slice(h * CHUNK, (h + 1) * CHUNK)
            sb = s_ref[d, h].astype(BF16)
            v_new.append(uw[rows, :DN_DV].astype(F32) - jnp.dot(uw[rows, DN_DV:], sb, preferred_element_type=F32))
            o_state.append(jnp.dot(qk[rows, :DN_DK], sb, preferred_element_type=F32))
        v_new = jnp.concatenate(v_new, axis=0).astype(BF16)
        o_st = jnp.dot(p_ref[0, 0], v_new, preferred_element_type=F32)
        outs = []
        for h in range(DN_HEADS):
            rows = slice(h * CHUNK, (h + 1) * CHUNK)
            outs.append(o_st[rows] + o_state[h])
            s_ref[d, h] = s_ref[d, h] * e_ref[0, 0, h:h + 1, :] + lax.dot_general(
                qk[rows, DN_DK:], v_new[rows], TN_DIMS, preferred_element_type=F32)
        o_ref[0] = jnp.concatenate(outs, axis=-1).astype(o_ref.dtype)


def _pair_mask(r, c, size, reverse):
    same = (r // (2 * size)) == (c // (2 * size))
    lo_r, lo_c = (r % (2 * size)) < size, (c % (2 * size)) < size
    if reverse:
        return same & lo_r & ~lo_c
    return same & ~lo_r & lo_c


def dn_scan(qkv, proj, a_log, dt_bias, l, lc):
    b, lt, _ = qkv.shape
    n_steps, fwd, rev = _scan_maps(l, lc)
    terms = dn_prep(qkv, proj, a_log, dt_bias)

    def specs(cmap):
        big = pl.BlockSpec((1, 1, DN_N, 2 * DN_DK), lambda bi, n: (bi, cmap(n), 0, 0))
        small = pl.BlockSpec((1, 1, SUBLANES, DN_DK), lambda bi, n: (bi, cmap(n), 0, 0))
        return [big, big, big, small]

    out_sd = jax.ShapeDtypeStruct((b, lt, DN_HEADS * DN_DV), BF16)
    return pl.pallas_call(
        _dn_state_kernel,
        grid=(b, n_steps),
        in_specs=specs(fwd) + specs(rev),
        out_specs=[pl.BlockSpec((1, CHUNK, DN_HEADS * DN_DV), lambda bi, n: (bi, fwd(n), 0)),
                   pl.BlockSpec((1, CHUNK, DN_HEADS * DN_DV), lambda bi, n: (bi, rev(n), 0))],
        out_shape=[out_sd, out_sd],
        scratch_shapes=[pltpu.VMEM((2, DN_HEADS, DN_DK, DN_DV), F32)],
        compiler_params=_cparams(("parallel", "arbitrary")),
        name="dn_scan",
    )(*terms)


CTX_TM = 256


def kernel(x, c, ctx, c_ctx, w_ada, b_ada, w_in, dn_conv, dn_a_log, dn_dt_bias, dn_norm, gla_w2, gla_b2,
           gla_norm, mla_q_norm, mla_w_uq, mla_kv_norm, mla_w_ukv, w_out, peer_wq, peer_keys, peer_u, peer_v,
           final_norm):
    b, l, d = x.shape
    lc = ctx.shape[1]
    lt = l + lc
    depth = w_ada.shape[0]
    lat_tm = min(1024, l)
    mix_tm = min(512, l)
    peer_tm = 512
    assert lc == CTX_TM and l % lat_tm == 0 and l % CTX_TM == 0

    pad = (-(b + 1)) % SUBLANES
    c_all = jnp.concatenate([c, c_ctx[None, :], jnp.zeros((pad, d), F32)], axis=0)
    mod = ada_mod(c_all, w_ada, b_ada)

    cs = rope_table(l, lc)
    cc, sc = channel_dft_tables()
    cl_lat, sl_lat = dft_tables(l, (l * FN_CH) ** -0.5)
    cl_ctx, sl_ctx = dft_tables(lc, (lc * FN_CH) ** -0.5)
    ctx_blk = l // lc

    x_lat, x_lat_off, x_ctx, x_ctx_off = x, 0, ctx, 0
    out = None
    for li in range(depth):
        last = li == depth - 1
        m_lat = [mod[li, :b, k * d:(k + 1) * d].reshape(b, 1, d) for k in range(N_MOD)]
        m_ctx = [mod[li, b:b + 1, k * d:(k + 1) * d].reshape(1, 1, d) for k in range(N_MOD)]

        w_p = prep_w_in(w_in[li])
        proj = proj_in(x_lat, x_lat_off, l // lat_tm, lat_tm, m_lat[0], m_lat[1], w_p, lt, 0, None)
        proj = proj_in(x_ctx, x_ctx_off, 1, CTX_TM, m_ctx[0], m_ctx[1], w_p, lt, l // CTX_TM, proj)

        wq_p, wkv_p = prep_mla_weights(mla_q_norm[li], mla_w_uq[li], mla_kv_norm[li], mla_w_ukv[li])
        q4, k4, v4 = mla_prep(proj, cs, wq_p, wkv_p, CTX_TM)
        y_mla = mla_attend(q4, k4, v4, 0, l // CTX_TM, CTX_TM, lt, 0, None)
        y_fn = fourier_mix(proj, 0, l, cl_lat, sl_lat, cc, sc, None)
        if not last:
            y_mla = mla_attend(q4, k4, v4, l // CTX_TM, lc // CTX_TM, CTX_TM, lc, ctx_blk, y_mla)
            y_fn = fourier_mix(proj, ctx_blk, lc, cl_ctx, sl_ctx, cc, sc, y_fn)
        gla_f, gla_b = gla_scan(proj, gla_w2[li], gla_b2[li], l, lc)
        qkv = dn_conv_norm(proj, dn_conv[li], l, lc)
        dn_f, dn_b = dn_scan(qkv, proj, dn_a_log[li], dn_dt_bias[li], l, lc)

        w_o = w_out[li].astype(BF16)
        rows = l if last else lt
        xn, h = mix_out(dn_f, dn_b, gla_f, gla_b, y_mla, y_fn, proj, 0,
                        x_lat, x_lat_off, l // mix_tm, mix_tm, m_lat[2], m_lat[3], m_lat[4],
                        dn_norm[li], gla_norm[li], w_o, rows, 0, None, None)
        if not last:
            xn, h = mix_out(dn_f, dn_b, gla_f, gla_b, y_mla, y_fn, proj,
                            l // CTX_TM, x_ctx, x_ctx_off, 1, CTX_TM, m_ctx[2], m_ctx[3], m_ctx[4],
                            dn_norm[li], gla_norm[li], w_o, rows, l // CTX_TM, xn, h)

        tn = b * rows
        g2_lat = jnp.broadcast_to(m_lat[5], (b, l // PEER_G2_ROWS, d))
        if last:
            g2_rows = g2_lat.reshape(tn // PEER_G2_ROWS, 1, d)
        else:
            g2_ctx = jnp.broadcast_to(m_ctx[5], (b, lc // PEER_G2_ROWS, d))
            g2_rows = jnp.concatenate([g2_lat, g2_ctx], axis=1).reshape(tn // PEER_G2_ROWS, 1, d)
        hf = h.reshape(tn, d)
        s, aux = peer_scores(hf, peer_wq[li].astype(BF16),
                             peer_keys[li].reshape(2 * PEER_HEADS, PEER_N_KEYS, PEER_HALF).astype(BF16), peer_tm)
        y = peer_experts(hf, s, aux, peer_u[li].astype(BF16), peer_v[li].T.astype(BF16), xn.reshape(tn, d),
                         g2_rows, final_norm, peer_tm, last)
        out = y.reshape(b, rows, d)
        x_lat, x_lat_off, x_ctx, x_ctx_off = out, 0, out, l // CTX_TM
    return out
```

```python
import functools
import math

import numpy as np
import jax
import jax.numpy as jnp
from jax import lax
from jax.experimental import pallas as pl
from jax.experimental.pallas import tpu as pltpu

F32 = jnp.float32
BF16 = jnp.bfloat16

EPS = 1e-6
N_MOD = 6
GRID_W = 64
ROPE_BASE = 10000.0

DN_HEADS, DN_DK, DN_DV, DN_CONV, DN_CHUNK = 4, 128, 128, 5, 64
GLA_HEADS, GLA_DK, GLA_DV, GLA_RANK, GLA_TAU, GLA_CHUNK = 4, 64, 128, 16, 16.0, 64
MLA_HEADS, MLA_NOPE, MLA_ROPE, MLA_V, MLA_Q_RANK, MLA_KV_RANK = 4, 128, 64, 128, 448, 128
FN_GROUPS, FN_CH = 4, 128
PEER_HEADS, PEER_N_KEYS, PEER_HALF, PEER_TOPK = 8, 128, 128, 16

VMEM_LIMIT_BYTES = 56 * 1024 * 1024
LANES = 128
SUBLANES = 8

C_DNQ, C_DNK, C_DNV, C_DNG = 0, 512, 1024, 1536
C_GLAV, C_GLAG, C_FN, C_CQ = 2048, 2560, 3072, 3584
C_GLAQ, C_GLAK = 4096, 4352
C_CKV, C_KR, C_SMALL = 4608, 4736, 4864
NP = 4992
PROJ_TN = NP // 3

NT_DIMS = (((1,), (1,)), ((), ()))
TN_DIMS = (((0,), (0,)), ((), ()))


def _cparams(sem):
    return pltpu.CompilerParams(dimension_semantics=sem, vmem_limit_bytes=VMEM_LIMIT_BYTES)


def _rms(x):
    return x * lax.rsqrt(jnp.mean(x * x, axis=-1, keepdims=True) + EPS)


def _ada_kernel(c_ref, w_ref, b_ref, o_ref):
    c = c_ref[...]
    a = (c * jax.nn.sigmoid(c)).astype(BF16)
    o_ref[0] = jnp.dot(a, w_ref[0].astype(BF16), preferred_element_type=F32) + b_ref[0]


def ada_mod(c_all, w_ada, b_ada):
    depth, d, n = w_ada.shape
    rows = c_all.shape[0]
    tn = 1024
    return pl.pallas_call(
        _ada_kernel,
        grid=(depth, n // tn),
        in_specs=[
            pl.BlockSpec((rows, d), lambda l, j: (0, 0)),
            pl.BlockSpec((1, d, tn), lambda l, j: (l, 0, j)),
            pl.BlockSpec((1, 1, tn), lambda l, j: (l, 0, j)),
        ],
        out_specs=pl.BlockSpec((1, rows, tn), lambda l, j: (l, 0, j)),
        out_shape=jax.ShapeDtypeStruct((depth, rows, n), F32),
        compiler_params=_cparams(("parallel", "parallel")),
        name="ada_mod",
    )(c_all, w_ada, b_ada.reshape(depth, 1, n))


def _proj_kernel(x_ref, sh_ref, sc_ref, w_ref, *rest):
    o_ref, xn_ref = rest[-2], rest[-1]

    @pl.when(pl.program_id(2) == 0)
    def _():
        xn_ref[...] = (_rms(x_ref[0]) * (1.0 + sc_ref[0]) + sh_ref[0]).astype(BF16)

    o_ref[0] = jnp.dot(xn_ref[...], w_ref[...], preferred_element_type=F32).astype(o_ref.dtype)


def proj_in(x, x_off, n_tiles, tm, shift, scale, w, out_rows, out_off, prev):
    b, _, d = x.shape
    bm = shift.shape[0]
    nj = w.shape[1] // PROJ_TN

    def mod_map(bi, i, j):
        return (bi if bm > 1 else 0, 0, 0)

    in_specs = [
        pl.BlockSpec((1, tm, d), lambda bi, i, j: (bi, i + x_off, 0)),
        pl.BlockSpec((1, 1, d), mod_map),
        pl.BlockSpec((1, 1, d), mod_map),
        pl.BlockSpec((d, PROJ_TN), lambda bi, i, j: (0, j)),
    ]
    args = [x, shift, scale, w]
    aliases = {}
    if prev is not None:
        in_specs.append(pl.BlockSpec(memory_space=pl.ANY))
        args.append(prev)
        aliases = {4: 0}
    return pl.pallas_call(
        _proj_kernel,
        grid=(b, n_tiles, nj),
        in_specs=in_specs,
        out_specs=pl.BlockSpec((1, tm, PROJ_TN), lambda bi, i, j: (bi, i + out_off, j)),
        out_shape=jax.ShapeDtypeStruct((b, out_rows, w.shape[1]), BF16),
        scratch_shapes=[pltpu.VMEM((tm, d), BF16)],
        input_output_aliases=aliases,
        compiler_params=_cparams(("parallel", "arbitrary", "arbitrary")),
        name="proj_in",
    )(*args)


def _mix_kernel(odf, odb, dg, ogf, ogb, gg, ym, yf, x_ref, g1, sh2, sc2, dnw, glw, w_ref, *rest):
    xo_ref, h_ref = rest[-2], rest[-1]

    def gated(of, ob, gate, wn):
        o = of[0].astype(F32) + ob[0].astype(F32)
        on = jnp.concatenate(
            [_rms(o[:, hh * LANES:(hh + 1) * LANES]) for hh in range(o.shape[1] // LANES)], axis=-1)
        g = gate[0].astype(F32)
        return (on * wn[...] * (g * jax.nn.sigmoid(g))).astype(BF16)

    q = w_ref.shape[0] // 4
    acc = jnp.dot(gated(odf, odb, dg, dnw), w_ref[0:q], preferred_element_type=F32)
    acc += jnp.dot(gated(ogf, ogb, gg, glw), w_ref[q:2 * q], preferred_element_type=F32)
    acc += jnp.dot(ym[0], w_ref[2 * q:3 * q], preferred_element_type=F32)
    acc += jnp.dot(yf[0], w_ref[3 * q:4 * q], preferred_element_type=F32)
    xn = x_ref[0] + g1[0] * acc
    xo_ref[0] = xn
    h_ref[0] = (_rms(xn) * (1.0 + sc2[0]) + sh2[0]).astype(BF16)


def mix_out(o_dn_f, o_dn_b, o_gla_f, o_gla_b, y_mla, y_fn, proj, off, x, x_off, n_tiles, tm,
            g1, sh2, sc2, dn_norm, gla_norm, w_out, out_rows, out_off, prev_x, prev_h):
    b, _, d = x.shape
    bm = g1.shape[0]
    wq = 512

    def mix_spec(col):
        return pl.BlockSpec((1, tm, wq), lambda bi, i: (bi, i + off, col))

    def mod_map(bi, i):
        return (bi if bm > 1 else 0, 0, 0)

    in_specs = [
        mix_spec(0), mix_spec(0), mix_spec(C_DNG // wq),
        mix_spec(0), mix_spec(0), mix_spec(C_GLAG // wq),
        mix_spec(0), mix_spec(0),
        pl.BlockSpec((1, tm, d), lambda bi, i: (bi, i + x_off, 0)),
        pl.BlockSpec((1, 1, d), mod_map), pl.BlockSpec((1, 1, d), mod_map), pl.BlockSpec((1, 1, d), mod_map),
        pl.BlockSpec((1, wq), lambda bi, i: (0, 0)), pl.BlockSpec((1, wq), lambda bi, i: (0, 0)),
        pl.BlockSpec((4 * wq, d), lambda bi, i: (0, 0)),
    ]
    args = [o_dn_f, o_dn_b, proj, o_gla_f, o_gla_b, proj, y_mla, y_fn, x, g1, sh2, sc2,
            jnp.tile(dn_norm.astype(F32), DN_HEADS).reshape(1, wq),
            jnp.tile(gla_norm.astype(F32), GLA_HEADS).reshape(1, wq), w_out]
    aliases = {}
    if prev_x is not None:
        in_specs += [pl.BlockSpec(memory_space=pl.ANY), pl.BlockSpec(memory_space=pl.ANY)]
        args += [prev_x, prev_h]
        aliases = {len(args) - 2: 0, len(args) - 1: 1}
    out_spec = pl.BlockSpec((1, tm, d), lambda bi, i: (bi, i + out_off, 0))
    return pl.pallas_call(
        _mix_kernel,
        grid=(b, n_tiles),
        in_specs=in_specs,
        out_specs=[out_spec, out_spec],
        out_shape=[jax.ShapeDtypeStruct((b, out_rows, d), F32), jax.ShapeDtypeStruct((b, out_rows, d), BF16)],
        input_output_aliases=aliases,
        compiler_params=_cparams(("parallel", "parallel")),
        name="mix_out",
    )(*args)


N_RANK = PEER_TOPK + 1
VAL_ROWS = 24
NEG = -1e30


def _candidate_tiles():
    runs = []
    for r1 in range(N_RANK):
        n = N_RANK // (r1 + 1)
        for r2_0 in range(0, n, SUBLANES):
            runs.append((r1, r2_0, min(SUBLANES, n - r2_0)))
    runs.sort(key=lambda t: -t[2])
    tiles, used = [], []
    for r1, r2_0, length in runs:
        for i in range(len(tiles)):
            if used[i] + length <= SUBLANES:
                tiles[i].append((used[i], r1, r2_0, length))
                used[i] += length
                break
        else:
            tiles.append([(0, r1, r2_0, length)])
            used.append(length)
    return tiles


_CAND_TILES = _candidate_tiles()


def _top_values(work, n):
    vals = []
    for _ in range(n):
        m = jnp.max(work, axis=0, keepdims=True)
        vals.append(m)
        work = jnp.where(work >= m, NEG, work)
    return vals


def _peer_score_kernel(h_ref, wq_ref, keys_ref, s_ref, aux_ref, vals_ref):
    tm = h_ref.shape[0]
    n_chunks = tm // LANES
    q = jnp.dot(h_ref[...], wq_ref[...], preferred_element_type=F32).astype(BF16)
    for hp in range(2 * PEER_HEADS):
        s_ref[hp] = lax.dot_general(keys_ref[hp], q[:, hp * PEER_HALF:(hp + 1) * PEER_HALF], NT_DIMS,
                                    preferred_element_type=F32)

    def stage1(hp, carry):
        for c in range(n_chunks):
            lanes = slice(c * LANES, (c + 1) * LANES)
            vals = _top_values(s_ref[hp, :, lanes], N_RANK)
            vals += [jnp.full((1, LANES), NEG, F32)] * (VAL_ROWS - N_RANK)
            vals_ref[hp, :, lanes] = jnp.concatenate(vals, axis=0)
        return carry

    lax.fori_loop(0, 2 * PEER_HEADS, stage1, 0)

    row = lax.broadcasted_iota(jnp.int32, (SUBLANES, LANES), 0)

    def stage2(hh, carry):
        for c in range(n_chunks):
            lanes = slice(c * LANES, (c + 1) * LANES)
            v1 = vals_ref[2 * hh, :, lanes]
            v2 = vals_ref[2 * hh + 1, :, lanes]
            cands = []
            for segments in _CAND_TILES:
                tile = jnp.full((SUBLANES, LANES), NEG, F32)
                for row0, r1, r2_0, length in segments:
                    blk, off = r2_0 // SUBLANES, r2_0 % SUBLANES
                    src = v2[blk * SUBLANES:(blk + 1) * SUBLANES]
                    shift = (row0 - off) % SUBLANES
                    if shift:
                        src = pltpu.roll(src, shift, axis=0)
                    tile = jnp.where((row >= row0) & (row < row0 + length), src + v1[r1:r1 + 1], tile)
                cands.append(tile)
            t0 = v1[0:1] + v2[0:1]
            taken = jnp.zeros((1, LANES), F32)
            z = jnp.zeros((1, LANES), F32)
            t_k = jnp.zeros((1, LANES), F32)
            t_k1 = jnp.zeros((1, LANES), F32)
            for _ in range(N_RANK):
                m8 = functools.reduce(jnp.maximum, cands)
                m = jnp.max(m8, axis=0, keepdims=True)
                cnt8 = functools.reduce(lambda a, b: a + b, [jnp.where(cd == m, 1.0, 0.0) for cd in cands])
                cnt = jnp.sum(cnt8, axis=0, keepdims=True)
                after = taken + cnt
                z = z + jnp.clip(PEER_TOPK - taken, 0.0, cnt) * jnp.exp(m - t0)
                t_k = jnp.where((taken < PEER_TOPK) & (after >= PEER_TOPK), m, t_k)
                t_k1 = jnp.where((taken < N_RANK) & (after >= N_RANK), m, t_k1)
                taken = after
                cands = [jnp.where(cd >= m, NEG, cd) for cd in cands]
            tau = 0.5 * (t_k + t_k1)
            aux = jnp.concatenate([tau, v1[0:1], v2[0:1], 1.0 / z, jnp.zeros((4, LANES), F32)], axis=0)
            aux_ref[hh, :, lanes] = aux
        return carry

    lax.fori_loop(0, PEER_HEADS, stage2, 0)


def peer_scores(h, wq, keys, tm):
    tn, d = h.shape
    nk = 2 * PEER_HEADS
    return pl.pallas_call(
        _peer_score_kernel,
        grid=(tn // tm,),
        in_specs=[
            pl.BlockSpec((tm, d), lambda i: (i, 0)),
            pl.BlockSpec(wq.shape, lambda i: (0, 0)),
            pl.BlockSpec(keys.shape, lambda i: (0, 0, 0)),
        ],
        out_specs=[
            pl.BlockSpec((nk, PEER_N_KEYS, tm), lambda i: (0, 0, i)),
            pl.BlockSpec((PEER_HEADS, SUBLANES, tm), lambda i: (0, 0, i)),
        ],
        out_shape=[jax.ShapeDtypeStruct((nk, PEER_N_KEYS, tn), F32),
                   jax.ShapeDtypeStruct((PEER_HEADS, SUBLANES, tn), F32)],
        scratch_shapes=[pltpu.VMEM((nk, VAL_ROWS, tm), F32)],
        compiler_params=_cparams(("parallel",)),
        name="peer_scores",
    )(h, wq, keys)


PEER_TE = 1024
PEER_G2_ROWS = 256


def _gelu(a):
    return 0.5 * a * (1.0 + lax.erf(a * (1.0 / math.sqrt(2.0))))


def _peer_expert_kernel(h_ref, s_ref, aux_ref, u_ref, vt_ref, x_ref, g2_ref, fn_ref, o_ref,
                        acc_ref, e2_ref, act_ref, wt_ref, *, final):
    e = pl.program_id(1)
    tm = h_ref.shape[0]
    n_chunks = tm // LANES
    n_i = PEER_TE // PEER_N_KEYS

    @pl.when(e == 0)
    def _():
        acc_ref[...] = jnp.zeros_like(acc_ref)
        for hh in range(PEER_HEADS):
            e2_ref[hh] = jnp.exp(s_ref[2 * hh + 1] - aux_ref[hh, 2:3, :]) * aux_ref[hh, 3:4, :]

    act_ref[...] = lax.dot_general(u_ref[...], h_ref[...], NT_DIMS, preferred_element_type=F32)

    i0 = pl.multiple_of(e * n_i, n_i)
    thr, e1 = [], []
    for hh in range(PEER_HEADS):
        s1 = s_ref[2 * hh, pl.ds(i0, n_i), :]
        thr.append(aux_ref[hh, 0:1, :] - s1)
        e1.append(jnp.exp(s1 - aux_ref[hh, 1:2, :]))
    for il in range(n_i):
        rows = slice(il * PEER_N_KEYS, (il + 1) * PEER_N_KEYS)
        for c in range(n_chunks):
            lanes = slice(c * LANES, (c + 1) * LANES)
            g = jnp.zeros((PEER_N_KEYS, LANES), F32)
            for hh in range(PEER_HEADS):
                sel = jnp.where(s_ref[2 * hh + 1, :, lanes] >= thr[hh][il:il + 1, lanes], e2_ref[hh, :, lanes], 0.0)
                g = g + sel * e1[hh][il:il + 1, lanes]
            wt_ref[rows, lanes] = (g * _gelu(act_ref[rows, lanes])).astype(BF16)

    acc_ref[...] += jnp.dot(vt_ref[...], wt_ref[...], preferred_element_type=F32)

    @pl.when(e == pl.num_programs(1) - 1)
    def _():
        y = jnp.transpose(acc_ref[...])
        for r in range(tm // PEER_G2_ROWS):
            rows = slice(r * PEER_G2_ROWS, (r + 1) * PEER_G2_ROWS)
            out = x_ref[rows, :] + g2_ref[r] * y[rows, :]
            if final:
                out = _rms(out) * fn_ref[...]
            o_ref[rows, :] = out


def peer_experts(h, s, aux, u, vt, x, g2_rows, final_norm, tm, final):
    tn, d = h.shape
    ne = u.shape[0]
    nk = 2 * PEER_HEADS
    once = pl.Buffered(1)
    return pl.pallas_call(
        functools.partial(_peer_expert_kernel, final=final),
        grid=(tn // tm, ne // PEER_TE),
        in_specs=[
            pl.BlockSpec((tm, d), lambda i, e: (i, 0), pipeline_mode=once),
            pl.BlockSpec((nk, PEER_N_KEYS, tm), lambda i, e: (0, 0, i), pipeline_mode=once),
            pl.BlockSpec((PEER_HEADS, SUBLANES, tm), lambda i, e: (0, 0, i)),
            pl.BlockSpec((PEER_TE, d), lambda i, e: (e, 0)),
            pl.BlockSpec((d, PEER_TE), lambda i, e: (0, e)),
            pl.BlockSpec((tm, d), lambda i, e: (i, 0), pipeline_mode=once),
            pl.BlockSpec((tm // PEER_G2_ROWS, 1, d), lambda i, e: (i, 0, 0)),
            pl.BlockSpec((1, d), lambda i, e: (0, 0)),
        ],
        out_specs=pl.BlockSpec((tm, d), lambda i, e: (i, 0)),
        out_shape=jax.ShapeDtypeStruct((tn, d), F32),
        scratch_shapes=[
            pltpu.VMEM((d, tm), F32),
            pltpu.VMEM((PEER_HEADS, PEER_N_KEYS, tm), F32),
            pltpu.VMEM((PEER_TE, tm), F32),
            pltpu.VMEM((PEER_TE, tm), BF16),
        ],
        compiler_params=_cparams(("parallel", "arbitrary")),
        name="peer_experts",
    )(h, s, aux, u, vt, x, g2_rows, final_norm.reshape(1, d).astype(F32))


IN_LAYOUT = (
    ("dn_qkv", 1536), ("dn_gate", 512), ("dn_a", 8), ("dn_b", 8), ("gla_q", 256), ("gla_k", 256),
    ("gla_v", 512), ("gla_gate", 512), ("gla_alpha", 32), ("mla_cq", 448), ("mla_ckv", 128),
    ("mla_kr", 64), ("fn", 512),
)


def _in_cols():
    offs, o = {}, 0
    for name, w in IN_LAYOUT:
        offs[name] = (o, o + w)
        o += w
    return offs


def _rot_half_cols(w):
    return jnp.concatenate([-w[:, 16:32], w[:, 0:16], -w[:, 48:64], w[:, 32:48]], axis=1)


def prep_w_in(w_in):
    c = _in_cols()
    d = w_in.shape[0]

    def cols(name):
        a, b = c[name]
        return w_in[:, a:b]

    z = lambda n: jnp.zeros((d, n), w_in.dtype)
    kr = cols("mla_kr")
    pieces = [
        cols("dn_qkv"), cols("dn_gate"), cols("gla_v"), cols("gla_gate"), cols("fn"),
        cols("mla_cq"), z(512 - MLA_Q_RANK), cols("gla_q"), cols("gla_k"), cols("mla_ckv"),
        kr, _rot_half_cols(kr), cols("dn_a"), cols("dn_b"), cols("gla_alpha"), z(128 - 48),
    ]
    w = jnp.concatenate(pieces, axis=1)
    assert w.shape[1] == NP
    return w.astype(BF16)


MLA_SLOT = 256


def prep_mla_weights(q_norm, w_uq, kv_norm, w_ukv):
    scale = (MLA_NOPE + MLA_ROPE) ** -0.5
    wq = w_uq * q_norm[:, None] * scale
    per = MLA_NOPE + MLA_ROPE
    cols = []
    for h in range(MLA_HEADS):
        nope = wq[:, h * per:h * per + MLA_NOPE]
        rope = wq[:, h * per + MLA_NOPE:(h + 1) * per]
        cols += [nope, rope, _rot_half_cols(rope)]
    wq_p = jnp.concatenate(cols, axis=1)
    wq_p = jnp.concatenate([wq_p, jnp.zeros((512 - MLA_Q_RANK, wq_p.shape[1]), wq_p.dtype)], axis=0)
    wkv = w_ukv * kv_norm[:, None]
    per = MLA_NOPE + MLA_V
    wkv_p = jnp.concatenate([wkv[:, h * per:h * per + MLA_NOPE] for h in range(MLA_HEADS)]
                            + [wkv[:, h * per + MLA_NOPE:(h + 1) * per] for h in range(MLA_HEADS)], axis=1)
    return wq_p.astype(BF16), wkv_p.astype(BF16)


def rope_table(l, lc):
    rows = l // GRID_W
    row = np.repeat(np.arange(rows), GRID_W).astype(np.float32)
    col = np.tile(np.arange(GRID_W), rows).astype(np.float32)
    half = MLA_ROPE // 2
    inv_freq = jnp.asarray(ROPE_BASE, F32) ** (-jnp.arange(0, half, 2, dtype=F32) / half)
    ang_r = jnp.asarray(row)[:, None] * inv_freq
    ang_c = jnp.asarray(col)[:, None] * inv_freq
    cos = jnp.concatenate([jnp.cos(ang_r)] * 2 + [jnp.cos(ang_c)] * 2, axis=1)
    sin = jnp.concatenate([jnp.sin(ang_r)] * 2 + [jnp.sin(ang_c)] * 2, axis=1)
    lat = jnp.concatenate([cos, sin], axis=1)
    ctx = jnp.concatenate([jnp.ones((lc, MLA_ROPE), F32), jnp.zeros((lc, MLA_ROPE), F32)], axis=1)
    return jnp.concatenate([lat, ctx], axis=0)


def _mla_prep_kernel(cq_ref, ckv_ref, kr_ref, cs_ref, wq_ref, wkv_ref, q_ref, k_ref, v_ref):
    cs = cs_ref[...]
    half = LANES // 2
    lane = lax.broadcasted_iota(jnp.int32, cs.shape, 1)

    def rotary(u):
        return u + pltpu.roll(u, half, axis=1)

    cq = cq_ref[0].astype(F32)
    cqn = cq * lax.rsqrt(jnp.sum(cq * cq, axis=-1, keepdims=True) * (1.0 / MLA_Q_RANK) + EPS)
    q = jnp.dot(cqn.astype(BF16), wq_ref[...], preferred_element_type=F32)
    qs = []
    for h in range(MLA_HEADS):
        qs.append(q[:, h * MLA_SLOT:h * MLA_SLOT + LANES])
        qs.append(rotary(q[:, h * MLA_SLOT + LANES:(h + 1) * MLA_SLOT] * cs))
    q_ref[0] = jnp.concatenate(qs, axis=-1).astype(BF16)

    kv = jnp.dot(_rms(ckv_ref[0].astype(F32)).astype(BF16), wkv_ref[...], preferred_element_type=F32)
    kr = jnp.where(lane < half, rotary(kr_ref[0].astype(F32) * cs), 0.0)
    ks = []
    for h in range(MLA_HEADS):
        ks += [kv[:, h * LANES:(h + 1) * LANES], kr]
    k_ref[0] = jnp.concatenate(ks, axis=-1).astype(BF16)
    v_ref[0] = kv[:, MLA_HEADS * LANES:].astype(BF16)


def mla_prep(proj, cs, wq_p, wkv_p, tm):
    b, lt, _ = proj.shape
    nq = MLA_HEADS * MLA_SLOT
    return pl.pallas_call(
        _mla_prep_kernel,
        grid=(b, lt // tm),
        in_specs=[
            pl.BlockSpec((1, tm, 512), lambda bi, i: (bi, i, C_CQ // 512)),
            pl.BlockSpec((1, tm, LANES), lambda bi, i: (bi, i, C_CKV // LANES)),
            pl.BlockSpec((1, tm, LANES), lambda bi, i: (bi, i, C_KR // LANES)),
            pl.BlockSpec((tm, LANES), lambda bi, i: (i, 0)),
            pl.BlockSpec(wq_p.shape, lambda bi, i: (0, 0)),
            pl.BlockSpec(wkv_p.shape, lambda bi, i: (0, 0)),
        ],
        out_specs=[
            pl.BlockSpec((1, tm, nq), lambda bi, i: (bi, i, 0)),
            pl.BlockSpec((1, tm, nq), lambda bi, i: (bi, i, 0)),
            pl.BlockSpec((1, tm, MLA_HEADS * MLA_V), lambda bi, i: (bi, i, 0)),
        ],
        out_shape=[jax.ShapeDtypeStruct((b, lt, nq), BF16), jax.ShapeDtypeStruct((b, lt, nq), BF16),
                   jax.ShapeDtypeStruct((b, lt, MLA_HEADS * MLA_V), BF16)],
        compiler_params=_cparams(("parallel", "parallel")),
        name="mla_prep",
    )(proj, proj, proj, cs, wq_p, wkv_p)


def _mla_attn_kernel(q_ref, k_ref, v_ref, *rest):
    o_ref = rest[-1]
    s = lax.dot_general(q_ref[0], k_ref[0], NT_DIMS, preferred_element_type=F32)
    p = jnp.exp(s - jnp.max(s, axis=-1, keepdims=True))
    o = jnp.dot(p.astype(BF16), v_ref[0], preferred_element_type=F32)
    o_ref[0] = (o / jnp.sum(p, axis=-1, keepdims=True)).astype(o_ref.dtype)


def mla_attend(q4, k4, v4, q_off, n_tiles, tq, k_rows, k_blk, prev):
    b, lt, _ = q4.shape
    in_specs = [
        pl.BlockSpec((1, tq, MLA_SLOT), lambda bi, h, i: (bi, i + q_off, h)),
        pl.BlockSpec((1, k_rows, MLA_SLOT), lambda bi, h, i: (bi, k_blk, h)),
        pl.BlockSpec((1, k_rows, MLA_V), lambda bi, h, i: (bi, k_blk, h)),
    ]
    args = [q4, k4, v4]
    aliases = {}
    if prev is not None:
        in_specs.append(pl.BlockSpec(memory_space=pl.ANY))
        args.append(prev)
        aliases = {3: 0}
    return pl.pallas_call(
        _mla_attn_kernel,
        grid=(b, MLA_HEADS, n_tiles),
        in_specs=in_specs,
        out_specs=pl.BlockSpec((1, tq, MLA_V), lambda bi, h, i: (bi, i + q_off, h)),
        out_shape=jax.ShapeDtypeStruct((b, lt, MLA_HEADS * MLA_V), BF16),
        input_output_aliases=aliases,
        compiler_params=_cparams(("parallel", "parallel", "parallel")),
        name="mla_attend",
    )(*args)


def dft_tables(n, scale):
    j = np.arange(n, dtype=np.int64)
    root = int(round(math.sqrt(n))) if int(round(math.sqrt(n))) ** 2 == n else 1
    if root == 1:
        ph = 2.0 * np.pi * ((j[:, None] * j[None, :]) % n) / n
        return jnp.asarray(np.cos(ph) * scale, BF16), jnp.asarray(np.sin(ph) * scale, BF16)
    u = np.arange(root, dtype=np.int64)
    pa = 2.0 * np.pi * ((j[:, None] * (root * u)[None, :]) % n) / n
    pb = 2.0 * np.pi * ((j[:, None] * u[None, :]) % n) / n
    ca, sa, cb, sb = (jnp.asarray(t, F32) for t in (np.cos(pa), np.sin(pa), np.cos(pb) * scale, np.sin(pb) * scale))
    cos = ca[:, :, None] * cb[:, None, :] - sa[:, :, None] * sb[:, None, :]
    sin = sa[:, :, None] * cb[:, None, :] + ca[:, :, None] * sb[:, None, :]
    return cos.reshape(n, n).astype(BF16), sin.reshape(n, n).astype(BF16)


def channel_dft_tables():
    c, s = dft_tables(FN_CH, 1.0)
    eye = jnp.eye(FN_GROUPS, dtype=F32)
    return (jnp.kron(eye, c.astype(F32)).astype(BF16), jnp.kron(eye, s.astype(F32)).astype(BF16))


def _fourier_kernel(z_ref, cl_ref, sl_ref, cc_ref, sc_ref, *rest):
    o_ref = rest[-1]
    z = z_ref[0]
    re = jnp.dot(cl_ref[...], z, preferred_element_type=F32).astype(BF16)
    im = jnp.dot(sl_ref[...], z, preferred_element_type=F32).astype(BF16)
    y = jnp.dot(re, cc_ref[...], preferred_element_type=F32) - jnp.dot(im, sc_ref[...], preferred_element_type=F32)
    o_ref[0] = y.astype(o_ref.dtype)


def fourier_mix(proj, seq_blk, n, cl, sl, cc, sc, prev):
    b, lt, _ = proj.shape
    w = FN_GROUPS * FN_CH
    tm = min(512, n)
    nt = n // tm
    in_specs = [
        pl.BlockSpec((1, n, w), lambda i, bi: (bi, seq_blk, C_FN // w)),
        pl.BlockSpec((tm, n), lambda i, bi: (i, 0)),
        pl.BlockSpec((tm, n), lambda i, bi: (i, 0)),
        pl.BlockSpec((w, w), lambda i, bi: (0, 0)),
        pl.BlockSpec((w, w), lambda i, bi: (0, 0)),
    ]
    args = [proj, cl, sl, cc, sc]
    aliases = {}
    if prev is not None:
        in_specs.append(pl.BlockSpec(memory_space=pl.ANY))
        args.append(prev)
        aliases = {5: 0}
    return pl.pallas_call(
        _fourier_kernel,
        grid=(nt, b),
        in_specs=in_specs,
        out_specs=pl.BlockSpec((1, tm, w), lambda i, bi: (bi, seq_blk * nt + i, 0)),
        out_shape=jax.ShapeDtypeStruct((b, lt, w), BF16),
        input_output_aliases=aliases,
        compiler_params=_cparams(("parallel", "parallel")),
        name="fourier_mix",
    )(*args)


CHUNK = 64
BMM_DIMS = (((2,), (1,)), ((0,), (0,)))
BMM_NT_DIMS = (((2,), (2,)), ((0,), (0,)))


def _scan_maps(l, lc):
    n_lat, n_ctx = l // CHUNK, lc // CHUNK

    def fwd(n):
        return jnp.where(n < n_ctx, n + n_lat, n - n_ctx)

    def rev(n):
        return n_lat + n_ctx - 1 - n

    return n_lat + n_ctx, fwd, rev


def _bmm(a, b, dims=BMM_DIMS):
    return lax.dot_general(a, b, dims, preferred_element_type=F32)


def _bmm_exact_lhs(m, x):
    mb = m.astype(BF16)
    hi = x.astype(BF16)
    r1 = x - hi.astype(F32)
    mid = r1.astype(BF16)
    lo = (r1 - mid.astype(F32)).astype(BF16)
    return _bmm(mb, hi) + _bmm(mb, mid) + _bmm(mb, lo)


GLA_PREP_CHUNKS = 4
GLA_NQ = GLA_HEADS * GLA_DK
GLA_NV = GLA_HEADS * GLA_DV


def _gla_prep_kernel(q_ref, k_ref, v_ref, a_ref, w2_ref, b2_ref, m_ref, *out_refs):
    nb = 2 * GLA_PREP_CHUNKS
    inc = m_ref[...]
    lane_q = lax.broadcasted_iota(jnp.int32, (nb, CHUNK, GLA_NQ), 2)
    a, q, k, v = [], [], [], []
    for j in range(GLA_PREP_CHUNKS):
        rows = slice(j * CHUNK, (j + 1) * CHUNK)
        for d in range(2):
            a.append(a_ref[0, rows, :].astype(F32))
            q.append(q_ref[0, rows, :].astype(F32) * (GLA_DK ** -0.5))
            k.append(k_ref[0, rows, :].astype(F32))
            v.append(v_ref[0, rows, :])
    a, q, k, v = jnp.stack(a), jnp.stack(q), jnp.stack(k), jnp.stack(v)
    w2 = jnp.stack([w2_ref[it % 2] for it in range(nb)])
    b2 = jnp.stack([jnp.broadcast_to(b2_ref[it % 2], (CHUNK, GLA_NQ)) for it in range(nb)])
    z = lax.dot_general(a, w2, BMM_DIMS, preferred_element_type=F32, precision=lax.Precision.HIGHEST) + b2
    log_a = jax.nn.log_sigmoid(z) * (1.0 / GLA_TAU)
    bcum = _bmm_exact_lhs(inc, log_a)
    b_rows = [bcum[it, (0 if it % 2 else CHUNK - 1):(1 if it % 2 else CHUNK), :] for it in range(nb)]
    b_last = jnp.stack([jnp.broadcast_to(row, (CHUNK, GLA_NQ)) for row in b_rows])
    q_in = (q * jnp.exp(bcum)).astype(BF16)
    q_rel = q * jnp.exp(bcum - b_last)
    k_rel = (k * jnp.exp(b_last - bcum)).astype(BF16)
    inc_f = inc.astype(F32)
    outs = []
    for h in range(GLA_HEADS):
        qh = jnp.where(lane_q // GLA_DK == h, q_rel, 0.0).astype(BF16)
        att = (_bmm(qh, k_rel, BMM_NT_DIMS) * inc_f).astype(BF16)
        outs.append(_bmm(att, v[:, :, h * GLA_DV:(h + 1) * GLA_DV]))
    o_intra = jnp.concatenate(outs, axis=-1).astype(BF16)
    for j in range(GLA_PREP_CHUNKS):
        rows = slice(j * CHUNK, (j + 1) * CHUNK)
        for d in range(2):
            it = 2 * j + d
            o_ref, qi_ref, kr_ref, e_ref = out_refs[4 * d:4 * d + 4]
            o_ref[0, rows, :] = o_intra[it]
            qi_ref[0, rows, :] = q_in[it]
            kr_ref[0, rows, :] = k_rel[it]
            e_ref[0, j] = jnp.exp(jnp.broadcast_to(b_rows[it], (SUBLANES, GLA_NQ)))


def _gla_state_kernel(*refs):
    (oif, qif, krf, ef, vf, oir, qir, krr, er, vr, of_ref, or_ref, st_ref) = refs

    @pl.when(pl.program_id(1) == 0)
    def _():
        st_ref[...] = jnp.zeros_like(st_ref)

    bd = (lax.broadcasted_iota(jnp.int32, (GLA_NV, GLA_NQ), 0) // GLA_DV) == (
        lax.broadcasted_iota(jnp.int32, (GLA_NV, GLA_NQ), 1) // GLA_DK)
    groups = ((oif, qif, krf, ef, vf, of_ref), (oir, qir, krr, er, vr, or_ref))
    for d, (oi_ref, qi_ref, kr_ref, e_ref, v_ref, o_ref) in enumerate(groups):
        st = st_ref[d]
        o_inter = lax.dot_general(qi_ref[0], st.astype(BF16), NT_DIMS, preferred_element_type=F32)
        o_ref[0] = (oi_ref[0].astype(F32) + o_inter).astype(o_ref.dtype)
        upd = lax.dot_general(v_ref[0], kr_ref[0], TN_DIMS, preferred_element_type=F32)
        st_ref[d] = st * e_ref[0, 0, 0:1, :] + jnp.where(bd, upd, 0.0)


def gla_scan(proj, w2, b2, l, lc):
    b, lt, _ = proj.shape
    n_steps, fwd, rev = _scan_maps(l, lc)
    nq, nv = GLA_NQ, GLA_NV
    n_chunks = lt // CHUNK
    rows = GLA_PREP_CHUNKS * CHUNK
    w2p = jnp.zeros((2, LANES, nq), F32)
    for d in range(2):
        lo = 2 * DN_HEADS * 2 + GLA_RANK * d
        w2p = w2p.at[d, lo:lo + GLA_RANK].set(w2[d].astype(F32))
    r, c = np.meshgrid(np.arange(CHUNK), np.arange(CHUNK), indexing="ij")
    inc = jnp.asarray(np.stack([(c >= r) if it % 2 else (c <= r) for it in range(2 * GLA_PREP_CHUNKS)])
                      .astype(np.float32), BF16)

    wide = jax.ShapeDtypeStruct((b, lt, nv), BF16)
    mid = jax.ShapeDtypeStruct((b, lt, nq), BF16)
    small = jax.ShapeDtypeStruct((b, n_chunks, SUBLANES, nq), F32)
    wide_spec = pl.BlockSpec((1, rows, nv), lambda bi, i: (bi, i, 0))
    mid_spec = pl.BlockSpec((1, rows, nq), lambda bi, i: (bi, i, 0))
    small_spec = pl.BlockSpec((1, GLA_PREP_CHUNKS, SUBLANES, nq), lambda bi, i: (bi, i, 0, 0))
    terms = pl.pallas_call(
        _gla_prep_kernel,
        grid=(b, n_chunks // GLA_PREP_CHUNKS),
        in_specs=[
            pl.BlockSpec((1, rows, nq), lambda bi, i: (bi, i, C_GLAQ // nq)),
            pl.BlockSpec((1, rows, nq), lambda bi, i: (bi, i, C_GLAK // nq)),
            pl.BlockSpec((1, rows, nv), lambda bi, i: (bi, i, C_GLAV // nv)),
            pl.BlockSpec((1, rows, LANES), lambda bi, i: (bi, i, C_SMALL // LANES)),
            pl.BlockSpec((2, LANES, nq), lambda bi, i: (0, 0, 0)),
            pl.BlockSpec((2, 1, nq), lambda bi, i: (0, 0, 0)),
            pl.BlockSpec(inc.shape, lambda bi, i: (0, 0, 0)),
        ],
        out_specs=[wide_spec, mid_spec, mid_spec, small_spec] * 2,
        out_shape=[wide, mid, mid, small] * 2,
        compiler_params=_cparams(("parallel", "parallel")),
        name="gla_prep",
    )(proj, proj, proj, proj, w2p, b2.astype(F32).reshape(2, 1, nq), inc)

    def specs(cmap):
        return [pl.BlockSpec((1, CHUNK, nv), lambda bi, n: (bi, cmap(n), 0)),
                pl.BlockSpec((1, CHUNK, nq), lambda bi, n: (bi, cmap(n), 0)),
                pl.BlockSpec((1, CHUNK, nq), lambda bi, n: (bi, cmap(n), 0)),
                pl.BlockSpec((1, 1, SUBLANES, nq), lambda bi, n: (bi, cmap(n), 0, 0)),
                pl.BlockSpec((1, CHUNK, nv), lambda bi, n: (bi, cmap(n), C_GLAV // nv))]

    return pl.pallas_call(
        _gla_state_kernel,
        grid=(b, n_steps),
        in_specs=specs(fwd) + specs(rev),
        out_specs=[pl.BlockSpec((1, CHUNK, nv), lambda bi, n: (bi, fwd(n), 0)),
                   pl.BlockSpec((1, CHUNK, nv), lambda bi, n: (bi, rev(n), 0))],
        out_shape=[wide, wide],
        scratch_shapes=[pltpu.VMEM((2, nv, nq), F32)],
        compiler_params=_cparams(("parallel", "arbitrary")),
        name="gla_state",
    )(*terms[0:4], proj, *terms[4:8], proj)


DN_TL = 256
DN_W = DN_HEADS * DN_DK


def _dn_conv_kernel(x_ref, p_ref, n_ref, w_ref, o_ref, *, first_tiles, last_tiles):
    i = pl.program_id(1)
    tl = x_ref.shape[1]
    is_first = functools.reduce(jnp.logical_or, [i == t for t in first_tiles])
    is_last = functools.reduce(jnp.logical_or, [i == t for t in last_tiles])
    prev = jnp.where(is_first, 0.0, p_ref[0].astype(F32))
    nxt = jnp.where(is_last, 0.0, n_ref[0].astype(F32))
    ext = jnp.concatenate([prev, x_ref[0].astype(F32), nxt], axis=0)
    pad = DN_CONV // 2
    acc = jnp.zeros((tl, ext.shape[1]), F32)
    for t in range(DN_CONV):
        start = SUBLANES - pad + t
        acc = acc + ext[start:start + tl, :] * w_ref[t:t + 1, :]
    y = acc * jax.nn.sigmoid(acc)
    outs = []
    for h in range(3 * DN_HEADS):
        yh = y[:, h * DN_DK:(h + 1) * DN_DK]
        if h < 2 * DN_HEADS:
            yh = yh * lax.rsqrt(jnp.sum(yh * yh, axis=-1, keepdims=True) + EPS)
            if h < DN_HEADS:
                yh = yh * (DN_DK ** -0.5)
        outs.append(yh)
    o_ref[0] = jnp.concatenate(outs, axis=-1).astype(o_ref.dtype)


def dn_conv_norm(proj, conv_w, l, lc):
    b, lt, _ = proj.shape
    w = 3 * DN_W
    tl = DN_TL
    hb = tl // SUBLANES
    n_halo = lt // SUBLANES
    first_tiles = (0, l // tl)
    last_tiles = (l // tl - 1, lt // tl - 1)
    return pl.pallas_call(
        functools.partial(_dn_conv_kernel, first_tiles=first_tiles, last_tiles=last_tiles),
        grid=(b, lt // tl),
        in_specs=[
            pl.BlockSpec((1, tl, w), lambda bi, i: (bi, i, 0)),
            pl.BlockSpec((1, SUBLANES, w), lambda bi, i: (bi, jnp.maximum(i * hb - 1, 0), 0)),
            pl.BlockSpec((1, SUBLANES, w), lambda bi, i: (bi, jnp.minimum((i + 1) * hb, n_halo - 1), 0)),
            pl.BlockSpec((DN_CONV, w), lambda bi, i: (0, 0)),
        ],
        out_specs=pl.BlockSpec((1, tl, w), lambda bi, i: (bi, i, 0)),
        out_shape=jax.ShapeDtypeStruct((b, lt, w), BF16),
        compiler_params=_cparams(("parallel", "parallel")),
        name="dn_conv",
    )(proj, proj, proj, conv_w.astype(F32))


def _stack_heads(x, width):
    return jnp.concatenate([x[:, h * width:(h + 1) * width] for h in range(DN_HEADS)], axis=0)


DN_PREP_CHUNKS = 2
DN_N = DN_HEADS * CHUNK


def dn_masks():
    n = DN_N
    r, c = np.meshgrid(np.arange(n), np.arange(n), indexing="ij")
    same = (r // CHUNK) == (c // CHUNK)
    kinds = [[r == c] * 2, [same & (c <= r), same & (c >= r)], [same & (c < r), same & (c > r)]]
    size = 1
    while size < CHUNK:
        blk = (r // (2 * size)) == (c // (2 * size))
        lo_r, lo_c = (r % (2 * size)) < size, (c % (2 * size)) < size
        kinds.append([blk & ~lo_r & lo_c, blk & lo_r & ~lo_c])
        size *= 2
    m = np.stack([np.stack([k[it % 2] for it in range(2 * DN_PREP_CHUNKS)]) for k in kinds])
    return jnp.asarray(m.astype(np.float32), BF16)


def _dn_prep_kernel(x_ref, a_ref, gp_ref, m_ref, *out_refs):
    n = DN_N
    nb = 2 * DN_PREP_CHUNKS
    shape = (nb, n, n)
    eye, inc, strict = m_ref[0], m_ref[1], m_ref[2]

    def pair_mask(size):
        return m_ref[3 + int(math.log2(size))]

    qs, ks, vs, g_col, b_col = [], [], [], [], []
    for j in range(DN_PREP_CHUNKS):
        rows = slice(j * CHUNK, (j + 1) * CHUNK)
        x = x_ref[0, rows, :].astype(F32)
        a = a_ref[0, rows, :].astype(F32)
        g8 = gp_ref[0:1, :] * jax.nn.softplus(a + gp_ref[1:2, :])
        beta8 = jax.nn.sigmoid(a)
        for d in range(2):
            qs.append(_stack_heads(x[:, 0:DN_W], DN_DK))
            ks.append(_stack_heads(x[:, DN_W:2 * DN_W], DN_DK))
            vs.append(_stack_heads(x[:, 2 * DN_W:3 * DN_W], DN_DV))
            lo = DN_HEADS * d
            g_col.append(jnp.concatenate([g8[:, lo + h:lo + h + 1] for h in range(DN_HEADS)], axis=0))
            lo = 2 * DN_HEADS + DN_HEADS * d
            b_col.append(jnp.concatenate([beta8[:, lo + h:lo + h + 1] for h in range(DN_HEADS)], axis=0))
    qs, ks, vs = jnp.stack(qs), jnp.stack(ks), jnp.stack(vs)
    g_b = jnp.broadcast_to(jnp.stack(g_col), shape)
    beta_b = jnp.broadcast_to(jnp.stack(b_col), shape)
    diff = _bmm_exact_lhs(inc, g_b * strict.astype(F32))
    decay = jnp.exp(diff) * inc.astype(F32)
    gc = _bmm_exact_lhs(inc, g_b[:, :, :DN_DK])
    kb = ks.astype(BF16)
    kk = _bmm(kb, kb, BMM_NT_DIMS)
    a_mat = beta_b * kk * decay * strict.astype(F32)
    a_b = a_mat.astype(BF16)
    inv = eye.astype(F32) - a_mat * pair_mask(1).astype(F32)
    size = 2
    while size < CHUNK:
        t = _bmm(inv.astype(BF16), a_b * pair_mask(size))
        inv = inv - _bmm(t.astype(BF16), inv.astype(BF16))
        size *= 2
    beta_k = beta_b[:, :, :DN_DK]
    e_gc = jnp.exp(gc)
    invb = inv.astype(BF16)
    u = _bmm(invb, (vs * beta_k).astype(BF16))
    w = _bmm(invb, (ks * beta_k * e_gc).astype(BF16))
    p = (_bmm(qs.astype(BF16), kb, BMM_NT_DIMS) * decay).astype(BF16)
    q_dec = qs * e_gc
    for j in range(DN_PREP_CHUNKS):
        for d in range(2):
            it = 2 * j + d
            last = 0 if d == 1 else CHUNK - 1
            g_end = [gc[it, h * CHUNK + last:h * CHUNK + last + 1, :] for h in range(DN_HEADS)]
            gl = jnp.concatenate([jnp.broadcast_to(g, (CHUNK, DN_DK)) for g in g_end], axis=0)
            k_dec = ks[it] * jnp.exp(gl - gc[it])
            uw_ref, qk_ref, p_ref, e_ref = out_refs[4 * d:4 * d + 4]
            uw_ref[0, j] = jnp.concatenate([u[it], w[it]], axis=-1).astype(BF16)
            qk_ref[0, j] = jnp.concatenate([q_dec[it], k_dec], axis=-1).astype(BF16)
            p_ref[0, j] = p[it]
            e_ref[0, j] = jnp.exp(jnp.concatenate(g_end + [jnp.zeros((SUBLANES - DN_HEADS, DN_DK), F32)], axis=0))


def dn_prep(qkv, proj, a_log, dt_bias):
    b, lt, _ = qkv.shape
    w = 3 * DN_W
    n_chunks = lt // CHUNK
    gp = jnp.zeros((SUBLANES, LANES), F32)
    gp = gp.at[0, :2 * DN_HEADS].set(-jnp.exp(a_log.astype(F32)).reshape(-1))
    gp = gp.at[1, :2 * DN_HEADS].set(dt_bias.astype(F32).reshape(-1))
    rows = DN_PREP_CHUNKS * CHUNK
    big = jax.ShapeDtypeStruct((b, n_chunks, DN_N, 2 * DN_DK), BF16)
    small = jax.ShapeDtypeStruct((b, n_chunks, SUBLANES, DN_DK), F32)
    big_spec = pl.BlockSpec((1, DN_PREP_CHUNKS, DN_N, 2 * DN_DK), lambda bi, i: (bi, i, 0, 0))
    small_spec = pl.BlockSpec((1, DN_PREP_CHUNKS, SUBLANES, DN_DK), lambda bi, i: (bi, i, 0, 0))
    masks = dn_masks()
    return pl.pallas_call(
        _dn_prep_kernel,
        grid=(b, n_chunks // DN_PREP_CHUNKS),
        in_specs=[pl.BlockSpec((1, rows, w), lambda bi, i: (bi, i, 0)),
                  pl.BlockSpec((1, rows, LANES), lambda bi, i: (bi, i, C_SMALL // LANES)),
                  pl.BlockSpec((SUBLANES, LANES), lambda bi, i: (0, 0)),
                  pl.BlockSpec(masks.shape, lambda bi, i: (0, 0, 0, 0))],
        out_specs=[big_spec, big_spec, big_spec, small_spec] * 2,
        out_shape=[big, big, big, small] * 2,
        compiler_params=_cparams(("parallel", "parallel")),
        name="dn_prep",
    )(qkv, proj, gp, masks)


def _dn_state_kernel(*refs):
    (uwf, qkf, pf, ef, uwr, qkr, pr, er, of_ref, or_ref, s_ref) = refs

    @pl.when(pl.program_id(1) == 0)
    def _():
        s_ref[...] = jnp.zeros_like(s_ref)

    for d, (uw_ref, qk_ref, p_ref, e_ref, o_ref) in enumerate(((uwf, qkf, pf, ef, of_ref), (uwr, qkr, pr, er, or_ref))):
        uw = uw_ref[0, 0]
        qk = qk_ref[0, 0]
        v_new, o_state = [], []
        for h in range(DN_HEADS):
            rows = slice(h * CHUNK, (h + 1) * CHUNK)
            sb = s_ref[d, h].astype(BF16)
            v_new.append(uw[rows, :DN_DV].astype(F32) - jnp.dot(uw[rows, DN_DV:], sb, preferred_element_type=F32))
            o_state.append(jnp.dot(qk[rows, :DN_DK], sb, preferred_element_type=F32))
        v_new = jnp.concatenate(v_new, axis=0).astype(BF16)
        o_st = jnp.dot(p_ref[0, 0], v_new, preferred_element_type=F32)
        outs = []
        for h in range(DN_HEADS):
            rows = slice(h * CHUNK, (h + 1) * CHUNK)
            outs.append(o_st[rows] + o_state[h])
            s_ref[d, h] = s_ref[d, h] * e_ref[0, 0, h:h + 1, :] + lax.dot_general(
                qk[rows, DN_DK:], v_new[rows], TN_DIMS, preferred_element_type=F32)
        o_ref[0] = jnp.concatenate(outs, axis=-1).astype(o_ref.dtype)


def _pair_mask(r, c, size, reverse):
    same = (r // (2 * size)) == (c // (2 * size))
    lo_r, lo_c = (r % (2 * size)) < size, (c % (2 * size)) < size
    if reverse:
        return same & lo_r & ~lo_c
    return same & ~lo_r & lo_c


def dn_scan(qkv, proj, a_log, dt_bias, l, lc):
    b, lt, _ = qkv.shape
    n_steps, fwd, rev = _scan_maps(l, lc)
    terms = dn_prep(qkv, proj, a_log, dt_bias)

    def specs(cmap):
        big = pl.BlockSpec((1, 1, DN_N, 2 * DN_DK), lambda bi, n: (bi, cmap(n), 0, 0))
        small = pl.BlockSpec((1, 1, SUBLANES, DN_DK), lambda bi, n: (bi, cmap(n), 0, 0))
        return [big, big, big, small]

    out_sd = jax.ShapeDtypeStruct((b, lt, DN_HEADS * DN_DV), BF16)
    return pl.pallas_call(
        _dn_state_kernel,
        grid=(b, n_steps),
        in_specs=specs(fwd) + specs(rev),
        out_specs=[pl.BlockSpec((1, CHUNK, DN_HEADS * DN_DV), lambda bi, n: (bi, fwd(n), 0)),
                   pl.BlockSpec((1, CHUNK, DN_HEADS * DN_DV), lambda bi, n: (bi, rev(n), 0))],
        out_shape=[out_sd, out_sd],
        scratch_shapes=[pltpu.VMEM((2, DN_HEADS, DN_DK, DN_DV), F32)],
        compiler_params=_cparams(("parallel", "arbitrary")),
        name="dn_scan",
    )(*terms)


CTX_TM = 256


def kernel(x, c, ctx, c_ctx, w_ada, b_ada, w_in, dn_conv, dn_a_log, dn_dt_bias, dn_norm, gla_w2, gla_b2,
           gla_norm, mla_q_norm, mla_w_uq, mla_kv_norm, mla_w_ukv, w_out, peer_wq, peer_keys, peer_u, peer_v,
           final_norm):
    b, l, d = x.shape
    lc = ctx.shape[1]
    lt = l + lc
    depth = w_ada.shape[0]
    lat_tm = min(1024, l)
    mix_tm = min(512, l)
    peer_tm = 512
    assert lc == CTX_TM and l % lat_tm == 0 and l % CTX_TM == 0

    pad = (-(b + 1)) % SUBLANES
    c_all = jnp.concatenate([c, c_ctx[None, :], jnp.zeros((pad, d), F32)], axis=0)
    mod = ada_mod(c_all, w_ada, b_ada)

    cs = rope_table(l, lc)
    cc, sc = channel_dft_tables()
    cl_lat, sl_lat = dft_tables(l, (l * FN_CH) ** -0.5)
    cl_ctx, sl_ctx = dft_tables(lc, (lc * FN_CH) ** -0.5)
    ctx_blk = l // lc

    x_lat, x_lat_off, x_ctx, x_ctx_off = x, 0, ctx, 0
    out = None
    for li in range(depth):
        last = li == depth - 1
        m_lat = [mod[li, :b, k * d:(k + 1) * d].reshape(b, 1, d) for k in range(N_MOD)]
        m_ctx = [mod[li, b:b + 1, k * d:(k + 1) * d].reshape(1, 1, d) for k in range(N_MOD)]

        w_p = prep_w_in(w_in[li])
        proj = proj_in(x_lat, x_lat_off, l // lat_tm, lat_tm, m_lat[0], m_lat[1], w_p, lt, 0, None)
        proj = proj_in(x_ctx, x_ctx_off, 1, CTX_TM, m_ctx[0], m_ctx[1], w_p, lt, l // CTX_TM, proj)

        wq_p, wkv_p = prep_mla_weights(mla_q_norm[li], mla_w_uq[li], mla_kv_norm[li], mla_w_ukv[li])
        q4, k4, v4 = mla_prep(proj, cs, wq_p, wkv_p, CTX_TM)
        y_mla = mla_attend(q4, k4, v4, 0, l // CTX_TM, CTX_TM, lt, 0, None)
        y_fn = fourier_mix(proj, 0, l, cl_lat, sl_lat, cc, sc, None)
        if not last:
            y_mla = mla_attend(q4, k4, v4, l // CTX_TM, lc // CTX_TM, CTX_TM, lc, ctx_blk, y_mla)
            y_fn = fourier_mix(proj, ctx_blk, lc, cl_ctx, sl_ctx, cc, sc, y_fn)
        gla_f, gla_b = gla_scan(proj, gla_w2[li], gla_b2[li], l, lc)
        qkv = dn_conv_norm(proj, dn_conv[li], l, lc)
        dn_f, dn_b = dn_scan(qkv, proj, dn_a_log[li], dn_dt_bias[li], l, lc)

        w_o = w_out[li].astype(BF16)
        rows = l if last else lt
        xn, h = mix_out(dn_f, dn_b, gla_f, gla_b, y_mla, y_fn, proj, 0,
                        x_lat, x_lat_off, l // mix_tm, mix_tm, m_lat[2], m_lat[3], m_lat[4],
                        dn_norm[li], gla_norm[li], w_o, rows, 0, None, None)
        if not last:
            xn, h = mix_out(dn_f, dn_b, gla_f, gla_b, y_mla, y_fn, proj,
                            l // CTX_TM, x_ctx, x_ctx_off, 1, CTX_TM, m_ctx[2], m_ctx[3], m_ctx[4],
                            dn_norm[li], gla_norm[li], w_o, rows, l // CTX_TM, xn, h)

        tn = b * rows
        g2_lat = jnp.broadcast_to(m_lat[5], (b, l // PEER_G2_ROWS, d))
        if last:
            g2_rows = g2_lat.reshape(tn // PEER_G2_ROWS, 1, d)
        else:
            g2_ctx = jnp.broadcast_to(m_ctx[5], (b, lc // PEER_G2_ROWS, d))
            g2_rows = jnp.concatenate([g2_lat, g2_ctx], axis=1).reshape(tn // PEER_G2_ROWS, 1, d)
        hf = h.reshape(tn, d)
        s, aux = peer_scores(hf, peer_wq[li].astype(BF16),
                             peer_keys[li].reshape(2 * PEER_HEADS, PEER_N_KEYS, PEER_HALF).astype(BF16), peer_tm)
        y = peer_experts(hf, s, aux, peer_u[li].astype(BF16), peer_v[li].T.astype(BF16), xn.reshape(tn, d),
                         g2_rows, final_norm, peer_tm, last)
        out = y.reshape(b, rows, d)
        x_lat, x_lat_off, x_ctx, x_ctx_off = out, 0, out, l // CTX_TM
    return out
```

```python
import functools
import math

import numpy as np
import jax
import jax.numpy as jnp
from jax import lax
from jax.experimental import pallas as pl
from jax.experimental.pallas import tpu as pltpu

F32 = jnp.float32
BF16 = jnp.bfloat16

EPS = 1e-6
N_MOD = 6
GRID_W = 64
ROPE_BASE = 10000.0

DN_HEADS, DN_DK, DN_DV, DN_CONV, DN_CHUNK = 4, 128, 128, 5, 64
GLA_HEADS, GLA_DK, GLA_DV, GLA_RANK, GLA_TAU, GLA_CHUNK = 4, 64, 128, 16, 16.0, 64
MLA_HEADS, MLA_NOPE, MLA_ROPE, MLA_V, MLA_Q_RANK, MLA_KV_RANK = 4, 128, 64, 128, 448, 128
FN_GROUPS, FN_CH = 4, 128
PEER_HEADS, PEER_N_KEYS, PEER_HALF, PEER_TOPK = 8, 128, 128, 16

VMEM_LIMIT_BYTES = 56 * 1024 * 1024
LANES = 128
SUBLANES = 8

C_DNQ, C_DNK, C_DNV, C_DNG = 0, 512, 1024, 1536
C_GLAV, C_GLAG, C_FN, C_CQ = 2048, 2560, 3072, 3584
C_GLAQ, C_GLAK = 4096, 4352
C_CKV, C_KR, C_SMALL = 4608, 4736, 4864
NP = 4992
PROJ_TN = NP // 3

NT_DIMS = (((1,), (1,)), ((), ()))
TN_DIMS = (((0,), (0,)), ((), ()))


def _cparams(sem):
    return pltpu.CompilerParams(dimension_semantics=sem, vmem_limit_bytes=VMEM_LIMIT_BYTES)


def _rms(x):
    return x * lax.rsqrt(jnp.mean(x * x, axis=-1, keepdims=True) + EPS)


def _ada_kernel(c_ref, w_ref, b_ref, o_ref):
    c = c_ref[...]
    a = (c * jax.nn.sigmoid(c)).astype(BF16)
    o_ref[0] = jnp.dot(a, w_ref[0].astype(BF16), preferred_element_type=F32) + b_ref[0]


def ada_mod(c_all, w_ada, b_ada):
    depth, d, n = w_ada.shape
    rows = c_all.shape[0]
    tn = 1024
    return pl.pallas_call(
        _ada_kernel,
        grid=(depth, n // tn),
        in_specs=[
            pl.BlockSpec((rows, d), lambda l, j: (0, 0)),
            pl.BlockSpec((1, d, tn), lambda l, j: (l, 0, j)),
            pl.BlockSpec((1, 1, tn), lambda l, j: (l, 0, j)),
        ],
        out_specs=pl.BlockSpec((1, rows, tn), lambda l, j: (l, 0, j)),
        out_shape=jax.ShapeDtypeStruct((depth, rows, n), F32),
        compiler_params=_cparams(("parallel", "parallel")),
        name="ada_mod",
    )(c_all, w_ada, b_ada.reshape(depth, 1, n))


def _proj_kernel(x_ref, sh_ref, sc_ref, w_ref, *rest):
    o_ref, xn_ref = rest[-2], rest[-1]

    @pl.when(pl.program_id(2) == 0)
    def _():
        xn_ref[...] = (_rms(x_ref[0]) * (1.0 + sc_ref[0]) + sh_ref[0]).astype(BF16)

    o_ref[0] = jnp.dot(xn_ref[...], w_ref[...], preferred_element_type=F32).astype(o_ref.dtype)


def proj_in(x, x_off, n_tiles, tm, shift, scale, w, out_rows, out_off, prev):
    b, _, d = x.shape
    bm = shift.shape[0]
    nj = w.shape[1] // PROJ_TN

    def mod_map(bi, i, j):
        return (bi if bm > 1 else 0, 0, 0)

    in_specs = [
        pl.BlockSpec((1, tm, d), lambda bi, i, j: (bi, i + x_off, 0)),
        pl.BlockSpec((1, 1, d), mod_map),
        pl.BlockSpec((1, 1, d), mod_map),
        pl.BlockSpec((d, PROJ_TN), lambda bi, i, j: (0, j)),
    ]
    args = [x, shift, scale, w]
    aliases = {}
    if prev is not None:
        in_specs.append(pl.BlockSpec(memory_space=pl.ANY))
        args.append(prev)
        aliases = {4: 0}
    return pl.pallas_call(
        _proj_kernel,
        grid=(b, n_tiles, nj),
        in_specs=in_specs,
        out_specs=pl.BlockSpec((1, tm, PROJ_TN), lambda bi, i, j: (bi, i + out_off, j)),
        out_shape=jax.ShapeDtypeStruct((b, out_rows, w.shape[1]), BF16),
        scratch_shapes=[pltpu.VMEM((tm, d), BF16)],
        input_output_aliases=aliases,
        compiler_params=_cparams(("parallel", "arbitrary", "arbitrary")),
        name="proj_in",
    )(*args)


def _mix_kernel(odf, odb, dg, ogf, ogb, gg, ym, yf, x_ref, g1, sh2, sc2, dnw, glw, w_ref, *rest):
    xo_ref, h_ref = rest[-2], rest[-1]

    def gated(of, ob, gate, wn):
        o = of[0].astype(F32) + ob[0].astype(F32)
        on = jnp.concatenate(
            [_rms(o[:, hh * LANES:(hh + 1) * LANES]) for hh in range(o.shape[1] // LANES)], axis=-1)
        g = gate[0].astype(F32)
        return (on * wn[...] * (g * jax.nn.sigmoid(g))).astype(BF16)

    q = w_ref.shape[0] // 4
    acc = jnp.dot(gated(odf, odb, dg, dnw), w_ref[0:q], preferred_element_type=F32)
    acc += jnp.dot(gated(ogf, ogb, gg, glw), w_ref[q:2 * q], preferred_element_type=F32)
    acc += jnp.dot(ym[0], w_ref[2 * q:3 * q], preferred_element_type=F32)
    acc += jnp.dot(yf[0], w_ref[3 * q:4 * q], preferred_element_type=F32)
    xn = x_ref[0] + g1[0] * acc
    xo_ref[0] = xn
    h_ref[0] = (_rms(xn) * (1.0 + sc2[0]) + sh2[0]).astype(BF16)


def mix_out(o_dn_f, o_dn_b, o_gla_f, o_gla_b, y_mla, y_fn, proj, off, x, x_off, n_tiles, tm,
            g1, sh2, sc2, dn_norm, gla_norm, w_out, out_rows, out_off, prev_x, prev_h):
    b, _, d = x.shape
    bm = g1.shape[0]
    wq = 512

    def mix_spec(col):
        return pl.BlockSpec((1, tm, wq), lambda bi, i: (bi, i + off, col))

    def mod_map(bi, i):
        return (bi if bm > 1 else 0, 0, 0)

    in_specs = [
        mix_spec(0), mix_spec(0), mix_spec(C_DNG // wq),
        mix_spec(0), mix_spec(0), mix_spec(C_GLAG // wq),
        mix_spec(0), mix_spec(0),
        pl.BlockSpec((1, tm, d), lambda bi, i: (bi, i + x_off, 0)),
        pl.BlockSpec((1, 1, d), mod_map), pl.BlockSpec((1, 1, d), mod_map), pl.BlockSpec((1, 1, d), mod_map),
        pl.BlockSpec((1, wq), lambda bi, i: (0, 0)), pl.BlockSpec((1, wq), lambda bi, i: (0, 0)),
        pl.BlockSpec((4 * wq, d), lambda bi, i: (0, 0)),
    ]
    args = [o_dn_f, o_dn_b, proj, o_gla_f, o_gla_b, proj, y_mla, y_fn, x, g1, sh2, sc2,
            jnp.tile(dn_norm.astype(F32), DN_HEADS).reshape(1, wq),
            jnp.tile(gla_norm.astype(F32), GLA_HEADS).reshape(1, wq), w_out]
    aliases = {}
    if prev_x is not None:
        in_specs += [pl.BlockSpec(memory_space=pl.ANY), pl.BlockSpec(memory_space=pl.ANY)]
        args += [prev_x, prev_h]
        aliases = {len(args) - 2: 0, len(args) - 1: 1}
    out_spec = pl.BlockSpec((1, tm, d), lambda bi, i: (bi, i + out_off, 0))
    return pl.pallas_call(
        _mix_kernel,
        grid=(b, n_tiles),
        in_specs=in_specs,
        out_specs=[out_spec, out_spec],
        out_shape=[jax.ShapeDtypeStruct((b, out_rows, d), F32), jax.ShapeDtypeStruct((b, out_rows, d), BF16)],
        input_output_aliases=aliases,
        compiler_params=_cparams(("parallel", "parallel")),
        name="mix_out",
    )(*args)


N_RANK = PEER_TOPK + 1
VAL_ROWS = 24
NEG = -1e30


def _candidate_tiles():
    runs = []
    for r1 in range(N_RANK):
        n = N_RANK // (r1 + 1)
        for r2_0 in range(0, n, SUBLANES):
            runs.append((r1, r2_0, min(SUBLANES, n - r2_0)))
    runs.sort(key=lambda t: -t[2])
    tiles, used = [], []
    for r1, r2_0, length in runs:
        for i in range(len(tiles)):
            if used[i] + length <= SUBLANES:
                tiles[i].append((used[i], r1, r2_0, length))
                used[i] += length
                break
        else:
            tiles.append([(0, r1, r2_0, length)])
            used.append(length)
    return tiles


_CAND_TILES = _candidate_tiles()


def _sorting_network(n):
    pairs = []
    p = 1
    while p < n:
        k = p
        while k >= 1:
            for j in range(k % p, n - k, 2 * k):
                for i in range(min(k, n - j - k)):
                    if (i + j) // (2 * p) == (i + j + k) // (2 * p):
                        pairs.append((i + j, i + j + k))
            k //= 2
        p *= 2
    return pairs


def _top_values(work, n):
    levels = [work[t * SUBLANES:(t + 1) * SUBLANES] for t in range(work.shape[0] // SUBLANES)]
    for a, b in _sorting_network(len(levels)):
        levels[a], levels[b] = jnp.maximum(levels[a], levels[b]), jnp.minimum(levels[a], levels[b])
    neg = jnp.full((SUBLANES, LANES), NEG, F32)
    vals = []
    for r in range(n):
        m = jnp.max(levels[0], axis=0, keepdims=True)
        vals.append(m)
        pop = levels[0] >= m
        depth = min(len(levels), n - r)
        for t in range(depth):
            below = levels[t + 1] if t + 1 < len(levels) else neg
            levels[t] = jnp.where(pop, below, levels[t])
    return vals


def _peer_score_kernel(h_ref, wq_ref, keys_ref, s_ref, aux_ref, vals_ref):
    tm = h_ref.shape[0]
    n_chunks = tm // LANES
    q = jnp.dot(h_ref[...], wq_ref[...], preferred_element_type=F32).astype(BF16)
    for hp in range(2 * PEER_HEADS):
        s_ref[hp] = lax.dot_general(keys_ref[hp], q[:, hp * PEER_HALF:(hp + 1) * PEER_HALF], NT_DIMS,
                                    preferred_element_type=F32)

    def stage1(hp, carry):
        for c in range(n_chunks):
            lanes = slice(c * LANES, (c + 1) * LANES)
            vals = _top_values(s_ref[hp, :, lanes], N_RANK)
            vals += [jnp.full((1, LANES), NEG, F32)] * (VAL_ROWS - N_RANK)
            vals_ref[hp, :, lanes] = jnp.concatenate(vals, axis=0)
        return carry

    lax.fori_loop(0, 2 * PEER_HEADS, stage1, 0)

    row = lax.broadcasted_iota(jnp.int32, (SUBLANES, LANES), 0)

    def stage2(hh, carry):
        for c in range(n_chunks):
            lanes = slice(c * LANES, (c + 1) * LANES)
            v1 = vals_ref[2 * hh, :, lanes]
            v2 = vals_ref[2 * hh + 1, :, lanes]
            cands = []
            for segments in _CAND_TILES:
                tile = jnp.full((SUBLANES, LANES), NEG, F32)
                for row0, r1, r2_0, length in segments:
                    blk, off = r2_0 // SUBLANES, r2_0 % SUBLANES
                    src = v2[blk * SUBLANES:(blk + 1) * SUBLANES]
                    shift = (row0 - off) % SUBLANES
                    if shift:
                        src = pltpu.roll(src, shift, axis=0)
                    tile = jnp.where((row >= row0) & (row < row0 + length), src + v1[r1:r1 + 1], tile)
                cands.append(tile)
            t0 = v1[0:1] + v2[0:1]
            taken = jnp.zeros((1, LANES), F32)
            z = jnp.zeros((1, LANES), F32)
            t_k = jnp.zeros((1, LANES), F32)
            t_k1 = jnp.zeros((1, LANES), F32)
            neg = jnp.full((SUBLANES, LANES), NEG, F32)
            levels = cands + [neg] * (SUBLANES - len(cands))
            for a, b in _sorting_network(len(levels)):
                levels[a], levels[b] = jnp.maximum(levels[a], levels[b]), jnp.minimum(levels[a], levels[b])
            for r in range(N_RANK):
                m = jnp.max(levels[0], axis=0, keepdims=True)
                pop = levels[0] >= m
                cnt = jnp.sum(jnp.where(pop, 1.0, 0.0), axis=0, keepdims=True)
                after = taken + cnt
                z = z + jnp.clip(PEER_TOPK - taken, 0.0, cnt) * jnp.exp(m - t0)
                t_k = jnp.where((taken < PEER_TOPK) & (after >= PEER_TOPK), m, t_k)
                t_k1 = jnp.where((taken < N_RANK) & (after >= N_RANK), m, t_k1)
                taken = after
                for t in range(min(len(levels), N_RANK - r)):
                    below = levels[t + 1] if t + 1 < len(levels) else neg
                    levels[t] = jnp.where(pop, below, levels[t])
            tau = 0.5 * (t_k + t_k1)
            aux = jnp.concatenate([tau, v1[0:1], v2[0:1], 1.0 / z, jnp.zeros((4, LANES), F32)], axis=0)
            aux_ref[hh, :, lanes] = aux
        return carry

    lax.fori_loop(0, PEER_HEADS, stage2, 0)


def peer_scores(h, wq, keys, tm):
    tn, d = h.shape
    nk = 2 * PEER_HEADS
    return pl.pallas_call(
        _peer_score_kernel,
        grid=(tn // tm,),
        in_specs=[
            pl.BlockSpec((tm, d), lambda i: (i, 0)),
            pl.BlockSpec(wq.shape, lambda i: (0, 0)),
            pl.BlockSpec(keys.shape, lambda i: (0, 0, 0)),
        ],
        out_specs=[
            pl.BlockSpec((nk, PEER_N_KEYS, tm), lambda i: (0, 0, i)),
            pl.BlockSpec((PEER_HEADS, SUBLANES, tm), lambda i: (0, 0, i)),
        ],
        out_shape=[jax.ShapeDtypeStruct((nk, PEER_N_KEYS, tn), F32),
                   jax.ShapeDtypeStruct((PEER_HEADS, SUBLANES, tn), F32)],
        scratch_shapes=[pltpu.VMEM((nk, VAL_ROWS, tm), F32)],
        compiler_params=_cparams(("parallel",)),
        name="peer_scores",
    )(h, wq, keys)


PEER_TE = 1024
PEER_G2_ROWS = 256


def _gelu(a):
    return 0.5 * a * (1.0 + lax.erf(a * (1.0 / math.sqrt(2.0))))


def _peer_expert_kernel(h_ref, s_ref, aux_ref, u_ref, vt_ref, x_ref, g2_ref, fn_ref, o_ref,
                        acc_ref, e2_ref, act_ref, wt_ref, *, final):
    e = pl.program_id(1)
    tm = h_ref.shape[0]
    n_chunks = tm // LANES
    n_i = PEER_TE // PEER_N_KEYS

    @pl.when(e == 0)
    def _():
        acc_ref[...] = jnp.zeros_like(acc_ref)
        for hh in range(PEER_HEADS):
            e2_ref[hh] = jnp.exp(s_ref[2 * hh + 1] - aux_ref[hh, 2:3, :]) * aux_ref[hh, 3:4, :]

    act_ref[...] = lax.dot_general(u_ref[...], h_ref[...], NT_DIMS, preferred_element_type=F32)

    i0 = pl.multiple_of(e * n_i, n_i)
    thr, e1 = [], []
    for hh in range(PEER_HEADS):
        s1 = s_ref[2 * hh, pl.ds(i0, n_i), :]
        thr.append(aux_ref[hh, 0:1, :] - s1)
        e1.append(jnp.exp(s1 - aux_ref[hh, 1:2, :]))
    for il in range(n_i):
        rows = slice(il * PEER_N_KEYS, (il + 1) * PEER_N_KEYS)
        for c in range(n_chunks):
            lanes = slice(c * LANES, (c + 1) * LANES)
            g = jnp.zeros((PEER_N_KEYS, LANES), F32)
            for hh in range(PEER_HEADS):
                sel = jnp.where(s_ref[2 * hh + 1, :, lanes] >= thr[hh][il:il + 1, lanes], e2_ref[hh, :, lanes], 0.0)
                g = g + sel * e1[hh][il:il + 1, lanes]
            wt_ref[rows, lanes] = (g * _gelu(act_ref[rows, lanes])).astype(BF16)

    acc_ref[...] += jnp.dot(vt_ref[...], wt_ref[...], preferred_element_type=F32)

    @pl.when(e == pl.num_programs(1) - 1)
    def _():
        y = jnp.transpose(acc_ref[...])
        for r in range(tm // PEER_G2_ROWS):
            rows = slice(r * PEER_G2_ROWS, (r + 1) * PEER_G2_ROWS)
            out = x_ref[rows, :] + g2_ref[r] * y[rows, :]
            if final:
                out = _rms(out) * fn_ref[...]
            o_ref[rows, :] = out


def peer_experts(h, s, aux, u, vt, x, g2_rows, final_norm, tm, final):
    tn, d = h.shape
    ne = u.shape[0]
    nk = 2 * PEER_HEADS
    once = pl.Buffered(1)
    return pl.pallas_call(
        functools.partial(_peer_expert_kernel, final=final),
        grid=(tn // tm, ne // PEER_TE),
        in_specs=[
            pl.BlockSpec((tm, d), lambda i, e: (i, 0), pipeline_mode=once),
            pl.BlockSpec((nk, PEER_N_KEYS, tm), lambda i, e: (0, 0, i), pipeline_mode=once),
            pl.BlockSpec((PEER_HEADS, SUBLANES, tm), lambda i, e: (0, 0, i)),
            pl.BlockSpec((PEER_TE, d), lambda i, e: (e, 0)),
            pl.BlockSpec((d, PEER_TE), lambda i, e: (0, e)),
            pl.BlockSpec((tm, d), lambda i, e: (i, 0), pipeline_mode=once),
            pl.BlockSpec((tm // PEER_G2_ROWS, 1, d), lambda i, e: (i, 0, 0)),
            pl.BlockSpec((1, d), lambda i, e: (0, 0)),
        ],
        out_specs=pl.BlockSpec((tm, d), lambda i, e: (i, 0)),
        out_shape=jax.ShapeDtypeStruct((tn, d), F32),
        scratch_shapes=[
            pltpu.VMEM((d, tm), F32),
            pltpu.VMEM((PEER_HEADS, PEER_N_KEYS, tm), F32),
            pltpu.VMEM((PEER_TE, tm), F32),
            pltpu.VMEM((PEER_TE, tm), BF16),
        ],
        compiler_params=_cparams(("parallel", "arbitrary")),
        name="peer_experts",
    )(h, s, aux, u, vt, x, g2_rows, final_norm.reshape(1, d).astype(F32))


IN_LAYOUT = (
    ("dn_qkv", 1536), ("dn_gate", 512), ("dn_a", 8), ("dn_b", 8), ("gla_q", 256), ("gla_k", 256),
    ("gla_v", 512), ("gla_gate", 512), ("gla_alpha", 32), ("mla_cq", 448), ("mla_ckv", 128),
    ("mla_kr", 64), ("fn", 512),
)


def _in_cols():
    offs, o = {}, 0
    for name, w in IN_LAYOUT:
        offs[name] = (o, o + w)
        o += w
    return offs


def _rot_half_cols(w):
    return jnp.concatenate([-w[:, 16:32], w[:, 0:16], -w[:, 48:64], w[:, 32:48]], axis=1)


def prep_w_in(w_in):
    c = _in_cols()
    d = w_in.shape[0]

    def cols(name):
        a, b = c[name]
        return w_in[:, a:b]

    z = lambda n: jnp.zeros((d, n), w_in.dtype)
    kr = cols("mla_kr")
    pieces = [
        cols("dn_qkv"), cols("dn_gate"), cols("gla_v"), cols("gla_gate"), cols("fn"),
        cols("mla_cq"), z(512 - MLA_Q_RANK), cols("gla_q"), cols("gla_k"), cols("mla_ckv"),
        kr, _rot_half_cols(kr), cols("dn_a"), cols("dn_b"), cols("gla_alpha"), z(128 - 48),
    ]
    w = jnp.concatenate(pieces, axis=1)
    assert w.shape[1] == NP
    return w.astype(BF16)


MLA_SLOT = 256


def prep_mla_weights(q_norm, w_uq, kv_norm, w_ukv):
    scale = (MLA_NOPE + MLA_ROPE) ** -0.5
    wq = w_uq * q_norm[:, None] * scale
    per = MLA_NOPE + MLA_ROPE
    cols = []
    for h in range(MLA_HEADS):
        nope = wq[:, h * per:h * per + MLA_NOPE]
        rope = wq[:, h * per + MLA_NOPE:(h + 1) * per]
        cols += [nope, rope, _rot_half_cols(rope)]
    wq_p = jnp.concatenate(cols, axis=1)
    wq_p = jnp.concatenate([wq_p, jnp.zeros((512 - MLA_Q_RANK, wq_p.shape[1]), wq_p.dtype)], axis=0)
    wkv = w_ukv * kv_norm[:, None]
    per = MLA_NOPE + MLA_V
    wkv_p = jnp.concatenate([wkv[:, h * per:h * per + MLA_NOPE] for h in range(MLA_HEADS)]
                            + [wkv[:, h * per + MLA_NOPE:(h + 1) * per] for h in range(MLA_HEADS)], axis=1)
    return wq_p.astype(BF16), wkv_p.astype(BF16)


def rope_table(l, lc):
    rows = l // GRID_W
    row = np.repeat(np.arange(rows), GRID_W).astype(np.float32)
    col = np.tile(np.arange(GRID_W), rows).astype(np.float32)
    half = MLA_ROPE // 2
    inv_freq = jnp.asarray(ROPE_BASE, F32) ** (-jnp.arange(0, half, 2, dtype=F32) / half)
    ang_r = jnp.asarray(row)[:, None] * inv_freq
    ang_c = jnp.asarray(col)[:, None] * inv_freq
    cos = jnp.concatenate([jnp.cos(ang_r)] * 2 + [jnp.cos(ang_c)] * 2, axis=1)
    sin = jnp.concatenate([jnp.sin(ang_r)] * 2 + [jnp.sin(ang_c)] * 2, axis=1)
    lat = jnp.concatenate([cos, sin], axis=1)
    ctx = jnp.concatenate([jnp.ones((lc, MLA_ROPE), F32), jnp.zeros((lc, MLA_ROPE), F32)], axis=1)
    return jnp.concatenate([lat, ctx], axis=0)


def _mla_prep_kernel(cq_ref, ckv_ref, kr_ref, cs_ref, wq_ref, wkv_ref, q_ref, k_ref, v_ref):
    cs = cs_ref[...]
    half = LANES // 2
    lane = lax.broadcasted_iota(jnp.int32, cs.shape, 1)

    def rotary(u):
        return u + pltpu.roll(u, half, axis=1)

    cq = cq_ref[0].astype(F32)
    cqn = cq * lax.rsqrt(jnp.sum(cq * cq, axis=-1, keepdims=True) * (1.0 / MLA_Q_RANK) + EPS)
    q = jnp.dot(cqn.astype(BF16), wq_ref[...], preferred_element_type=F32)
    qs = []
    for h in range(MLA_HEADS):
        qs.append(q[:, h * MLA_SLOT:h * MLA_SLOT + LANES])
        qs.append(rotary(q[:, h * MLA_SLOT + LANES:(h + 1) * MLA_SLOT] * cs))
    q_ref[0] = jnp.concatenate(qs, axis=-1).astype(BF16)

    kv = jnp.dot(_rms(ckv_ref[0].astype(F32)).astype(BF16), wkv_ref[...], preferred_element_type=F32)
    kr = jnp.where(lane < half, rotary(kr_ref[0].astype(F32) * cs), 0.0)
    ks = []
    for h in range(MLA_HEADS):
        ks += [kv[:, h * LANES:(h + 1) * LANES], kr]
    k_ref[0] = jnp.concatenate(ks, axis=-1).astype(BF16)
    v_ref[0] = kv[:, MLA_HEADS * LANES:].astype(BF16)


def mla_prep(proj, cs, wq_p, wkv_p, tm):
    b, lt, _ = proj.shape
    nq = MLA_HEADS * MLA_SLOT
    return pl.pallas_call(
        _mla_prep_kernel,
        grid=(b, lt // tm),
        in_specs=[
            pl.BlockSpec((1, tm, 512), lambda bi, i: (bi, i, C_CQ // 512)),
            pl.BlockSpec((1, tm, LANES), lambda bi, i: (bi, i, C_CKV // LANES)),
            pl.BlockSpec((1, tm, LANES), lambda bi, i: (bi, i, C_KR // LANES)),
            pl.BlockSpec((tm, LANES), lambda bi, i: (i, 0)),
            pl.BlockSpec(wq_p.shape, lambda bi, i: (0, 0)),
            pl.BlockSpec(wkv_p.shape, lambda bi, i: (0, 0)),
        ],
        out_specs=[
            pl.BlockSpec((1, tm, nq), lambda bi, i: (bi, i, 0)),
            pl.BlockSpec((1, tm, nq), lambda bi, i: (bi, i, 0)),
            pl.BlockSpec((1, tm, MLA_HEADS * MLA_V), lambda bi, i: (bi, i, 0)),
        ],
        out_shape=[jax.ShapeDtypeStruct((b, lt, nq), BF16), jax.ShapeDtypeStruct((b, lt, nq), BF16),
                   jax.ShapeDtypeStruct((b, lt, MLA_HEADS * MLA_V), BF16)],
        compiler_params=_cparams(("parallel", "parallel")),
        name="mla_prep",
    )(proj, proj, proj, cs, wq_p, wkv_p)


def _mla_attn_kernel(q_ref, k_ref, v_ref, *rest):
    o_ref = rest[-1]
    s = lax.dot_general(q_ref[0], k_ref[0], NT_DIMS, preferred_element_type=F32)
    p = jnp.exp(s - jnp.max(s, axis=-1, keepdims=True))
    o = jnp.dot(p.astype(BF16), v_ref[0], preferred_element_type=F32)
    o_ref[0] = (o / jnp.sum(p, axis=-1, keepdims=True)).astype(o_ref.dtype)


def mla_attend(q4, k4, v4, q_off, n_tiles, tq, k_rows, k_blk, prev):
    b, lt, _ = q4.shape
    in_specs = [
        pl.BlockSpec((1, tq, MLA_SLOT), lambda bi, h, i: (bi, i + q_off, h)),
        pl.BlockSpec((1, k_rows, MLA_SLOT), lambda bi, h, i: (bi, k_blk, h)),
        pl.BlockSpec((1, k_rows, MLA_V), lambda bi, h, i: (bi, k_blk, h)),
    ]
    args = [q4, k4, v4]
    aliases = {}
    if prev is not None:
        in_specs.append(pl.BlockSpec(memory_space=pl.ANY))
        args.append(prev)
        aliases = {3: 0}
    return pl.pallas_call(
        _mla_attn_kernel,
        grid=(b, MLA_HEADS, n_tiles),
        in_specs=in_specs,
        out_specs=pl.BlockSpec((1, tq, MLA_V), lambda bi, h, i: (bi, i + q_off, h)),
        out_shape=jax.ShapeDtypeStruct((b, lt, MLA_HEADS * MLA_V), BF16),
        input_output_aliases=aliases,
        compiler_params=_cparams(("parallel", "parallel", "parallel")),
        name="mla_attend",
    )(*args)


def dft_tables(n, scale):
    j = np.arange(n, dtype=np.int64)
    root = int(round(math.sqrt(n))) if int(round(math.sqrt(n))) ** 2 == n else 1
    if root == 1:
        ph = 2.0 * np.pi * ((j[:, None] * j[None, :]) % n) / n
        return jnp.asarray(np.cos(ph) * scale, BF16), jnp.asarray(np.sin(ph) * scale, BF16)
    u = np.arange(root, dtype=np.int64)
    pa = 2.0 * np.pi * ((j[:, None] * (root * u)[None, :]) % n) / n
    pb = 2.0 * np.pi * ((j[:, None] * u[None, :]) % n) / n
    ca, sa, cb, sb = (jnp.asarray(t, F32) for t in (np.cos(pa), np.sin(pa), np.cos(pb) * scale, np.sin(pb) * scale))
    cos = ca[:, :, None] * cb[:, None, :] - sa[:, :, None] * sb[:, None, :]
    sin = sa[:, :, None] * cb[:, None, :] + ca[:, :, None] * sb[:, None, :]
    return cos.reshape(n, n).astype(BF16), sin.reshape(n, n).astype(BF16)


def channel_dft_tables():
    c, s = dft_tables(FN_CH, 1.0)
    eye = jnp.eye(FN_GROUPS, dtype=F32)
    return (jnp.kron(eye, c.astype(F32)).astype(BF16), jnp.kron(eye, s.astype(F32)).astype(BF16))


def _fourier_kernel(z_ref, cl_ref, sl_ref, cc_ref, sc_ref, *rest):
    o_ref = rest[-1]
    z = z_ref[0]
    re = jnp.dot(cl_ref[...], z, preferred_element_type=F32).astype(BF16)
    im = jnp.dot(sl_ref[...], z, preferred_element_type=F32).astype(BF16)
    y = jnp.dot(re, cc_ref[...], preferred_element_type=F32) - jnp.dot(im, sc_ref[...], preferred_element_type=F32)
    o_ref[0] = y.astype(o_ref.dtype)


def fourier_mix(proj, seq_blk, n, cl, sl, cc, sc, prev):
    b, lt, _ = proj.shape
    w = FN_GROUPS * FN_CH
    tm = min(512, n)
    nt = n // tm
    in_specs = [
        pl.BlockSpec((1, n, w), lambda i, bi: (bi, seq_blk, C_FN // w)),
        pl.BlockSpec((tm, n), lambda i, bi: (i, 0)),
        pl.BlockSpec((tm, n), lambda i, bi: (i, 0)),
        pl.BlockSpec((w, w), lambda i, bi: (0, 0)),
        pl.BlockSpec((w, w), lambda i, bi: (0, 0)),
    ]
    args = [proj, cl, sl, cc, sc]
    aliases = {}
    if prev is not None:
        in_specs.append(pl.BlockSpec(memory_space=pl.ANY))
        args.append(prev)
        aliases = {5: 0}
    return pl.pallas_call(
        _fourier_kernel,
        grid=(nt, b),
        in_specs=in_specs,
        out_specs=pl.BlockSpec((1, tm, w), lambda i, bi: (bi, seq_blk * nt + i, 0)),
        out_shape=jax.ShapeDtypeStruct((b, lt, w), BF16),
        input_output_aliases=aliases,
        compiler_params=_cparams(("parallel", "parallel")),
        name="fourier_mix",
    )(*args)


CHUNK = 64
BMM_DIMS = (((2,), (1,)), ((0,), (0,)))
BMM_NT_DIMS = (((2,), (2,)), ((0,), (0,)))


STATE_GROUP = 4


def _scan_maps(l, lc):
    span = STATE_GROUP * CHUNK
    assert l % span == 0 and lc % span == 0
    n_lat, n_ctx = l // span, lc // span

    def fwd(n):
        return jnp.where(n < n_ctx, n + n_lat, n - n_ctx)

    def rev(n):
        return n_lat + n_ctx - 1 - n

    return n_lat + n_ctx, fwd, rev


def _bmm(a, b, dims=BMM_DIMS):
    return lax.dot_general(a, b, dims, preferred_element_type=F32)


def _bmm_exact_lhs(m, x):
    mb = m.astype(BF16)
    hi = x.astype(BF16)
    r1 = x - hi.astype(F32)
    mid = r1.astype(BF16)
    lo = (r1 - mid.astype(F32)).astype(BF16)
    return _bmm(mb, hi) + _bmm(mb, mid) + _bmm(mb, lo)


GLA_PREP_CHUNKS = 4
GLA_NQ = GLA_HEADS * GLA_DK
GLA_NV = GLA_HEADS * GLA_DV


def _gla_prep_kernel(q_ref, k_ref, v_ref, a_ref, w2_ref, b2_ref, m_ref, *out_refs):
    nb = 2 * GLA_PREP_CHUNKS
    inc = m_ref[...]
    lane_q = lax.broadcasted_iota(jnp.int32, (nb, CHUNK, GLA_NQ), 2)
    a, q, k, v = [], [], [], []
    for j in range(GLA_PREP_CHUNKS):
        rows = slice(j * CHUNK, (j + 1) * CHUNK)
        for d in range(2):
            a.append(a_ref[0, rows, :].astype(F32))
            q.append(q_ref[0, rows, :].astype(F32) * (GLA_DK ** -0.5))
            k.append(k_ref[0, rows, :].astype(F32))
            v.append(v_ref[0, rows, :])
    a, q, k, v = jnp.stack(a), jnp.stack(q), jnp.stack(k), jnp.stack(v)
    w2 = jnp.stack([w2_ref[it % 2] for it in range(nb)])
    b2 = jnp.stack([jnp.broadcast_to(b2_ref[it % 2], (CHUNK, GLA_NQ)) for it in range(nb)])
    z = lax.dot_general(a, w2, BMM_DIMS, preferred_element_type=F32, precision=lax.Precision.HIGHEST) + b2
    log_a = jax.nn.log_sigmoid(z) * (1.0 / GLA_TAU)
    bcum = _bmm_exact_lhs(inc, log_a)
    b_rows = [bcum[it, (0 if it % 2 else CHUNK - 1):(1 if it % 2 else CHUNK), :] for it in range(nb)]
    b_last = jnp.stack([jnp.broadcast_to(row, (CHUNK, GLA_NQ)) for row in b_rows])
    q_in = (q * jnp.exp(bcum)).astype(BF16)
    q_rel = q * jnp.exp(bcum - b_last)
    k_rel = (k * jnp.exp(b_last - bcum)).astype(BF16)
    inc_f = inc.astype(F32)
    outs = []
    for h in range(GLA_HEADS):
        qh = jnp.where(lane_q // GLA_DK == h, q_rel, 0.0).astype(BF16)
        att = (_bmm(qh, k_rel, BMM_NT_DIMS) * inc_f).astype(BF16)
        outs.append(_bmm(att, v[:, :, h * GLA_DV:(h + 1) * GLA_DV]))
    o_intra = jnp.concatenate(outs, axis=-1).astype(BF16)
    for j in range(GLA_PREP_CHUNKS):
        rows = slice(j * CHUNK, (j + 1) * CHUNK)
        for d in range(2):
            it = 2 * j + d
            o_ref, qi_ref, kr_ref, e_ref = out_refs[4 * d:4 * d + 4]
            o_ref[0, rows, :] = o_intra[it]
            qi_ref[0, rows, :] = q_in[it]
            kr_ref[0, rows, :] = k_rel[it]
            e_ref[0, j] = jnp.exp(jnp.broadcast_to(b_rows[it], (SUBLANES, GLA_NQ)))


def _gla_state_kernel(*refs):
    (oif, qif, krf, ef, vf, oir, qir, krr, er, vr, of_ref, or_ref, st_ref) = refs

    @pl.when(pl.program_id(1) == 0)
    def _():
        st_ref[...] = jnp.zeros_like(st_ref)

    bd = (lax.broadcasted_iota(jnp.int32, (GLA_NV, GLA_NQ), 0) // GLA_DV) == (
        lax.broadcasted_iota(jnp.int32, (GLA_NV, GLA_NQ), 1) // GLA_DK)
    groups = ((oif, qif, krf, ef, vf, of_ref), (oir, qir, krr, er, vr, or_ref))
    for jj in range(STATE_GROUP):
        for d, (oi_ref, qi_ref, kr_ref, e_ref, v_ref, o_ref) in enumerate(groups):
            j = STATE_GROUP - 1 - jj if d == 1 else jj
            rows = slice(j * CHUNK, (j + 1) * CHUNK)
            st = st_ref[d]
            o_inter = lax.dot_general(qi_ref[0, rows, :], st.astype(BF16), NT_DIMS, preferred_element_type=F32)
            o_ref[0, rows, :] = (oi_ref[0, rows, :].astype(F32) + o_inter).astype(o_ref.dtype)
            upd = lax.dot_general(v_ref[0, rows, :], kr_ref[0, rows, :], TN_DIMS, preferred_element_type=F32)
            st_ref[d] = st * e_ref[0, j, 0:1, :] + jnp.where(bd, upd, 0.0)


def gla_scan(proj, w2, b2, l, lc):
    b, lt, _ = proj.shape
    n_steps, fwd, rev = _scan_maps(l, lc)
    nq, nv = GLA_NQ, GLA_NV
    n_chunks = lt // CHUNK
    rows = GLA_PREP_CHUNKS * CHUNK
    w2p = jnp.zeros((2, LANES, nq), F32)
    for d in range(2):
        lo = 2 * DN_HEADS * 2 + GLA_RANK * d
        w2p = w2p.at[d, lo:lo + GLA_RANK].set(w2[d].astype(F32))
    r, c = np.meshgrid(np.arange(CHUNK), np.arange(CHUNK), indexing="ij")
    inc = jnp.asarray(np.stack([(c >= r) if it % 2 else (c <= r) for it in range(2 * GLA_PREP_CHUNKS)])
                      .astype(np.float32), BF16)

    wide = jax.ShapeDtypeStruct((b, lt, nv), BF16)
    mid = jax.ShapeDtypeStruct((b, lt, nq), BF16)
    small = jax.ShapeDtypeStruct((b, n_chunks, SUBLANES, nq), F32)
    wide_spec = pl.BlockSpec((1, rows, nv), lambda bi, i: (bi, i, 0))
    mid_spec = pl.BlockSpec((1, rows, nq), lambda bi, i: (bi, i, 0))
    small_spec = pl.BlockSpec((1, GLA_PREP_CHUNKS, SUBLANES, nq), lambda bi, i: (bi, i, 0, 0))
    terms = pl.pallas_call(
        _gla_prep_kernel,
        grid=(b, n_chunks // GLA_PREP_CHUNKS),
        in_specs=[
            pl.BlockSpec((1, rows, nq), lambda bi, i: (bi, i, C_GLAQ // nq)),
            pl.BlockSpec((1, rows, nq), lambda bi, i: (bi, i, C_GLAK // nq)),
            pl.BlockSpec((1, rows, nv), lambda bi, i: (bi, i, C_GLAV // nv)),
            pl.BlockSpec((1, rows, LANES), lambda bi, i: (bi, i, C_SMALL // LANES)),
            pl.BlockSpec((2, LANES, nq), lambda bi, i: (0, 0, 0)),
            pl.BlockSpec((2, 1, nq), lambda bi, i: (0, 0, 0)),
            pl.BlockSpec(inc.shape, lambda bi, i: (0, 0, 0)),
        ],
        out_specs=[wide_spec, mid_spec, mid_spec, small_spec] * 2,
        out_shape=[wide, mid, mid, small] * 2,
        compiler_params=_cparams(("parallel", "parallel")),
        name="gla_prep",
    )(proj, proj, proj, proj, w2p, b2.astype(F32).reshape(2, 1, nq), inc)

    span = STATE_GROUP * CHUNK

    def specs(cmap):
        return [pl.BlockSpec((1, span, nv), lambda bi, n: (bi, cmap(n), 0)),
                pl.BlockSpec((1, span, nq), lambda bi, n: (bi, cmap(n), 0)),
                pl.BlockSpec((1, span, nq), lambda bi, n: (bi, cmap(n), 0)),
                pl.BlockSpec((1, STATE_GROUP, SUBLANES, nq), lambda bi, n: (bi, cmap(n), 0, 0)),
                pl.BlockSpec((1, span, nv), lambda bi, n: (bi, cmap(n), C_GLAV // nv))]

    return pl.pallas_call(
        _gla_state_kernel,
        grid=(b, n_steps),
        in_specs=specs(fwd) + specs(rev),
        out_specs=[pl.BlockSpec((1, span, nv), lambda bi, n: (bi, fwd(n), 0)),
                   pl.BlockSpec((1, span, nv), lambda bi, n: (bi, rev(n), 0))],
        out_shape=[wide, wide],
        scratch_shapes=[pltpu.VMEM((2, nv, nq), F32)],
        compiler_params=_cparams(("parallel", "arbitrary")),
        name="gla_state",
    )(*terms[0:4], proj, *terms[4:8], proj)


DN_TL = 256
DN_W = DN_HEADS * DN_DK


def _dn_conv_kernel(x_ref, p_ref, n_ref, w_ref, o_ref, *, first_tiles, last_tiles):
    i = pl.program_id(1)
    tl = x_ref.shape[1]
    is_first = functools.reduce(jnp.logical_or, [i == t for t in first_tiles])
    is_last = functools.reduce(jnp.logical_or, [i == t for t in last_tiles])
    prev = jnp.where(is_first, 0.0, p_ref[0].astype(F32))
    nxt = jnp.where(is_last, 0.0, n_ref[0].astype(F32))
    ext = jnp.concatenate([prev, x_ref[0].astype(F32), nxt], axis=0)
    pad = DN_CONV // 2
    acc = jnp.zeros((tl, ext.shape[1]), F32)
    for t in range(DN_CONV):
        start = SUBLANES - pad + t
        acc = acc + ext[start:start + tl, :] * w_ref[t:t + 1, :]
    y = acc * jax.nn.sigmoid(acc)
    outs = []
    for h in range(3 * DN_HEADS):
        yh = y[:, h * DN_DK:(h + 1) * DN_DK]
        if h < 2 * DN_HEADS:
            yh = yh * lax.rsqrt(jnp.sum(yh * yh, axis=-1, keepdims=True) + EPS)
            if h < DN_HEADS:
                yh = yh * (DN_DK ** -0.5)
        outs.append(yh)
    o_ref[0] = jnp.concatenate(outs, axis=-1).astype(o_ref.dtype)


def dn_conv_norm(proj, conv_w, l, lc):
    b, lt, _ = proj.shape
    w = 3 * DN_W
    tl = DN_TL
    hb = tl // SUBLANES
    n_halo = lt // SUBLANES
    first_tiles = (0, l // tl)
    last_tiles = (l // tl - 1, lt // tl - 1)
    return pl.pallas_call(
        functools.partial(_dn_conv_kernel, first_tiles=first_tiles, last_tiles=last_tiles),
        grid=(b, lt // tl),
        in_specs=[
            pl.BlockSpec((1, tl, w), lambda bi, i: (bi, i, 0)),
            pl.BlockSpec((1, SUBLANES, w), lambda bi, i: (bi, jnp.maximum(i * hb - 1, 0), 0)),
            pl.BlockSpec((1, SUBLANES, w), lambda bi, i: (bi, jnp.minimum((i + 1) * hb, n_halo - 1), 0)),
            pl.BlockSpec((DN_CONV, w), lambda bi, i: (0, 0)),
        ],
        out_specs=pl.BlockSpec((1, tl, w), lambda bi, i: (bi, i, 0)),
        out_shape=jax.ShapeDtypeStruct((b, lt, w), BF16),
        compiler_params=_cparams(("parallel", "parallel")),
        name="dn_conv",
    )(proj, proj, proj, conv_w.astype(F32))


def _stack_heads(x, width):
    return jnp.concatenate([x[:, h * width:(h + 1) * width] for h in range(DN_HEADS)], axis=0)


DN_PREP_CHUNKS = 2
DN_N = DN_HEADS * CHUNK


def dn_masks():
    n = DN_N
    r, c = np.meshgrid(np.arange(n), np.arange(n), indexing="ij")
    same = (r // CHUNK) == (c // CHUNK)
    kinds = [[r == c] * 2, [same & (c <= r), same & (c >= r)], [same & (c < r), same & (c > r)]]
    size = 1
    while size < CHUNK:
        blk = (r // (2 * size)) == (c // (2 * size))
        lo_r, lo_c = (r % (2 * size)) < size, (c % (2 * size)) < size
        kinds.append([blk & ~lo_r & lo_c, blk & lo_r & ~lo_c])
        size *= 2
    m = np.stack([np.stack([k[it % 2] for it in range(2 * DN_PREP_CHUNKS)]) for k in kinds])
    return jnp.asarray(m.astype(np.float32), BF16)


def _dn_prep_kernel(x_ref, a_ref, gp_ref, m_ref, *out_refs):
    n = DN_N
    nb = 2 * DN_PREP_CHUNKS
    shape = (nb, n, n)
    eye, inc, strict = m_ref[0], m_ref[1], m_ref[2]

    def pair_mask(size):
        return m_ref[3 + int(math.log2(size))]

    qs, ks, vs, g_col, b_col = [], [], [], [], []
    for j in range(DN_PREP_CHUNKS):
        rows = slice(j * CHUNK, (j + 1) * CHUNK)
        x = x_ref[0, rows, :].astype(F32)
        a = a_ref[0, rows, :].astype(F32)
        g8 = gp_ref[0:1, :] * jax.nn.softplus(a + gp_ref[1:2, :])
        beta8 = jax.nn.sigmoid(a)
        for d in range(2):
            qs.append(_stack_heads(x[:, 0:DN_W], DN_DK))
            ks.append(_stack_heads(x[:, DN_W:2 * DN_W], DN_DK))
            vs.append(_stack_heads(x[:, 2 * DN_W:3 * DN_W], DN_DV))
            lo = DN_HEADS * d
            g_col.append(jnp.concatenate([g8[:, lo + h:lo + h + 1] for h in range(DN_HEADS)], axis=0))
            lo = 2 * DN_HEADS + DN_HEADS * d
            b_col.append(jnp.concatenate([beta8[:, lo + h:lo + h + 1] for h in range(DN_HEADS)], axis=0))
    qs, ks, vs = jnp.stack(qs), jnp.stack(ks), jnp.stack(vs)
    g_b = jnp.broadcast_to(jnp.stack(g_col), shape)
    beta_b = jnp.broadcast_to(jnp.stack(b_col), shape)
    diff = _bmm_exact_lhs(inc, g_b * strict.astype(F32))
    decay = jnp.exp(diff) * inc.astype(F32)
    gc = _bmm_exact_lhs(inc, g_b[:, :, :DN_DK])
    kb = ks.astype(BF16)
    kk = _bmm(kb, kb, BMM_NT_DIMS)
    a_mat = beta_b * kk * decay * strict.astype(F32)
    a_b = a_mat.astype(BF16)
    inv = eye.astype(F32) - a_mat * pair_mask(1).astype(F32)
    size = 2
    while size < CHUNK:
        t = _bmm(inv.astype(BF16), a_b * pair_mask(size))
        inv = inv - _bmm(t.astype(BF16), inv.astype(BF16))
        size *= 2
    beta_k = beta_b[:, :, :DN_DK]
    e_gc = jnp.exp(gc)
    invb = inv.astype(BF16)
    u = _bmm(invb, (vs * beta_k).astype(BF16))
    w = _bmm(invb, (ks * beta_k * e_gc).astype(BF16))
    p = (_bmm(qs.astype(BF16), kb, BMM_NT_DIMS) * decay).astype(BF16)
    q_dec = qs * e_gc
    for j in range(DN_PREP_CHUNKS):
        for d in range(2):
            it = 2 * j + d
            last = 0 if d == 1 else CHUNK - 1
            g_end = [gc[it, h * CHUNK + last:h * CHUNK + last + 1, :] for h in range(DN_HEADS)]
            gl = jnp.concatenate([jnp.broadcast_to(g, (CHUNK, DN_DK)) for g in g_end], axis=0)
            k_dec = ks[it] * jnp.exp(gl - gc[it])
            uw_ref, qk_ref, p_ref, e_ref = out_refs[4 * d:4 * d + 4]
            uw_ref[0, j] = jnp.concatenate([u[it], w[it]], axis=-1).astype(BF16)
            qk_ref[0, j] = jnp.concatenate([q_dec[it], k_dec], axis=-1).astype(BF16)
            p_ref[0, j] = p[it]
            e_ref[0, j] = jnp.exp(jnp.concatenate(g_end + [jnp.zeros((SUBLANES - DN_HEADS, DN_DK), F32)], axis=0))


def dn_prep(qkv, proj, a_log, dt_bias):
    b, lt, _ = qkv.shape
    w = 3 * DN_W
    n_chunks = lt // CHUNK
    gp = jnp.zeros((SUBLANES, LANES), F32)
    gp = gp.at[0, :2 * DN_HEADS].set(-jnp.exp(a_log.astype(F32)).reshape(-1))
    gp = gp.at[1, :2 * DN_HEADS].set(dt_bias.astype(F32).reshape(-1))
    rows = DN_PREP_CHUNKS * CHUNK
    big = jax.ShapeDtypeStruct((b, n_chunks, DN_N, 2 * DN_DK), BF16)
    small = jax.ShapeDtypeStruct((b, n_chunks, SUBLANES, DN_DK), F32)
    big_spec = pl.BlockSpec((1, DN_PREP_CHUNKS, DN_N, 2 * DN_DK), lambda bi, i: (bi, i, 0, 0))
    small_spec = pl.BlockSpec((1, DN_PREP_CHUNKS, SUBLANES, DN_DK), lambda bi, i: (bi, i, 0, 0))
    masks = dn_masks()
    return pl.pallas_call(
        _dn_prep_kernel,
        grid=(b, n_chunks // DN_PREP_CHUNKS),
        in_specs=[pl.BlockSpec((1, rows, w), lambda bi, i: (bi, i, 0)),
                  pl.BlockSpec((1, rows, LANES), lambda bi, i: (bi, i, C_SMALL // LANES)),
                  pl.BlockSpec((SUBLANES, LANES), lambda bi, i: (0, 0)),
                  pl.BlockSpec(masks.shape, lambda bi, i: (0, 0, 0, 0))],
        out_specs=[big_spec, big_spec, big_spec, small_spec] * 2,
        out_shape=[big, big, big, small] * 2,
        compiler_params=_cparams(("parallel", "parallel")),
        name="dn_prep",
    )(qkv, proj, gp, masks)


def _dn_state_kernel(*refs):
    (uwf, qkf, pf, ef, uwr, qkr, pr, er, of_ref, or_ref, s_ref) = refs

    @pl.when(pl.program_id(1) == 0)
    def _():
        s_ref[...] = jnp.zeros_like(s_ref)

    groups = ((uwf, qkf, pf, ef, of_ref), (uwr, qkr, pr, er, or_ref))
    for jj in range(STATE_GROUP):
        for d, (uw_ref, qk_ref, p_ref, e_ref, o_ref) in enumerate(groups):
            j = STATE_GROUP - 1 - jj if d == 1 else jj
            uw = uw_ref[0, j]
            qk = qk_ref[0, j]
            v_new, o_state = [], []
            for h in range(DN_HEADS):
                rows = slice(h * CHUNK, (h + 1) * CHUNK)
                sb = s_ref[d, h].astype(BF16)
                v_new.append(uw[rows, :DN_DV].astype(F32)
                             - jnp.dot(uw[rows, DN_DV:], sb, preferred_element_type=F32))
                o_state.append(jnp.dot(qk[rows, :DN_DK], sb, preferred_element_type=F32))
            v_new = jnp.concatenate(v_new, axis=0).astype(BF16)
            o_st = jnp.dot(p_ref[0, j], v_new, preferred_element_type=F32)
            outs = []
            for h in range(DN_HEADS):
                rows = slice(h * CHUNK, (h + 1) * CHUNK)
                outs.append(o_st[rows] + o_state[h])
                s_ref[d, h] = s_ref[d, h] * e_ref[0, j, h:h + 1, :] + lax.dot_general(
                    qk[rows, DN_DK:], v_new[rows], TN_DIMS, preferred_element_type=F32)
            o_ref[0, j * CHUNK:(j + 1) * CHUNK, :] = jnp.concatenate(outs, axis=-1).astype(o_ref.dtype)


def dn_scan(qkv, proj, a_log, dt_bias, l, lc):
    b, lt, _ = qkv.shape
    n_steps, fwd, rev = _scan_maps(l, lc)
    terms = dn_prep(qkv, proj, a_log, dt_bias)
    g = STATE_GROUP

    def specs(cmap):
        big = pl.BlockSpec((1, g, DN_N, 2 * DN_DK), lambda bi, n: (bi, cmap(n), 0, 0))
        small = pl.BlockSpec((1, g, SUBLANES, DN_DK), lambda bi, n: (bi, cmap(n), 0, 0))
        return [big, big, big, small]

    out_sd = jax.ShapeDtypeStruct((b, lt, DN_HEADS * DN_DV), BF16)
    return pl.pallas_call(
        _dn_state_kernel,
        grid=(b, n_steps),
        in_specs=specs(fwd) + specs(rev),
        out_specs=[pl.BlockSpec((1, g * CHUNK, DN_HEADS * DN_DV), lambda bi, n: (bi, fwd(n), 0)),
                   pl.BlockSpec((1, g * CHUNK, DN_HEADS * DN_DV), lambda bi, n: (bi, rev(n), 0))],
        out_shape=[out_sd, out_sd],
        scratch_shapes=[pltpu.VMEM((2, DN_HEADS, DN_DK, DN_DV), F32)],
        compiler_params=_cparams(("parallel", "arbitrary")),
        name="dn_scan",
    )(*terms)


CTX_TM = 256


def kernel(x, c, ctx, c_ctx, w_ada, b_ada, w_in, dn_conv, dn_a_log, dn_dt_bias, dn_norm, gla_w2, gla_b2,
           gla_norm, mla_q_norm, mla_w_uq, mla_kv_norm, mla_w_ukv, w_out, peer_wq, peer_keys, peer_u, peer_v,
           final_norm):
    b, l, d = x.shape
    lc = ctx.shape[1]
    lt = l + lc
    depth = w_ada.shape[0]
    lat_tm = min(1024, l)
    mix_tm = min(512, l)
    peer_tm = 512
    assert lc == CTX_TM and l % lat_tm == 0 and l % CTX_TM == 0

    pad = (-(b + 1)) % SUBLANES
    c_all = jnp.concatenate([c, c_ctx[None, :], jnp.zeros((pad, d), F32)], axis=0)
    mod = ada_mod(c_all, w_ada, b_ada)

    cs = rope_table(l, lc)
    cc, sc = channel_dft_tables()
    cl_lat, sl_lat = dft_tables(l, (l * FN_CH) ** -0.5)
    cl_ctx, sl_ctx = dft_tables(lc, (lc * FN_CH) ** -0.5)
    ctx_blk = l // lc

    x_lat, x_lat_off, x_ctx, x_ctx_off = x, 0, ctx, 0
    out = None
    for li in range(depth):
        last = li == depth - 1
        m_lat = [mod[li, :b, k * d:(k + 1) * d].reshape(b, 1, d) for k in range(N_MOD)]
        m_ctx = [mod[li, b:b + 1, k * d:(k + 1) * d].reshape(1, 1, d) for k in range(N_MOD)]

        w_p = prep_w_in(w_in[li])
        proj = proj_in(x_lat, x_lat_off, l // lat_tm, lat_tm, m_lat[0], m_lat[1], w_p, lt, 0, None)
        proj = proj_in(x_ctx, x_ctx_off, 1, CTX_TM, m_ctx[0], m_ctx[1], w_p, lt, l // CTX_TM, proj)

        wq_p, wkv_p = prep_mla_weights(mla_q_norm[li], mla_w_uq[li], mla_kv_norm[li], mla_w_ukv[li])
        q4, k4, v4 = mla_prep(proj, cs, wq_p, wkv_p, CTX_TM)
        y_mla = mla_attend(q4, k4, v4, 0, l // CTX_TM, CTX_TM, lt, 0, None)
        y_fn = fourier_mix(proj, 0, l, cl_lat, sl_lat, cc, sc, None)
        if not last:
            y_mla = mla_attend(q4, k4, v4, l // CTX_TM, lc // CTX_TM, CTX_TM, lc, ctx_blk, y_mla)
            y_fn = fourier_mix(proj, ctx_blk, lc, cl_ctx, sl_ctx, cc, sc, y_fn)
        gla_f, gla_b = gla_scan(proj, gla_w2[li], gla_b2[li], l, lc)
        qkv = dn_conv_norm(proj, dn_conv[li], l, lc)
        dn_f, dn_b = dn_scan(qkv, proj, dn_a_log[li], dn_dt_bias[li], l, lc)

        w_o = w_out[li].astype(BF16)
        rows = l if last else lt
        xn, h = mix_out(dn_f, dn_b, gla_f, gla_b, y_mla, y_fn, proj, 0,
                        x_lat, x_lat_off, l // mix_tm, mix_tm, m_lat[2], m_lat[3], m_lat[4],
                        dn_norm[li], gla_norm[li], w_o, rows, 0, None, None)
        if not last:
            xn, h = mix_out(dn_f, dn_b, gla_f, gla_b, y_mla, y_fn, proj,
                            l // CTX_TM, x_ctx, x_ctx_off, 1, CTX_TM, m_ctx[2], m_ctx[3], m_ctx[4],
                            dn_norm[li], gla_norm[li], w_o, rows, l // CTX_TM, xn, h)

        tn = b * rows
        g2_lat = jnp.broadcast_to(m_lat[5], (b, l // PEER_G2_ROWS, d))
        if last:
            g2_rows = g2_lat.reshape(tn // PEER_G2_ROWS, 1, d)
        else:
            g2_ctx = jnp.broadcast_to(m_ctx[5], (b, lc // PEER_G2_ROWS, d))
            g2_rows = jnp.concatenate([g2_lat, g2_ctx], axis=1).reshape(tn // PEER_G2_ROWS, 1, d)
        hf = h.reshape(tn, d)
        s, aux = peer_scores(hf, peer_wq[li].astype(BF16),
                             peer_keys[li].reshape(2 * PEER_HEADS, PEER_N_KEYS, PEER_HALF).astype(BF16), peer_tm)
        y = peer_experts(hf, s, aux, peer_u[li].astype(BF16), peer_v[li].T.astype(BF16), xn.reshape(tn, d),
                         g2_rows, final_norm, peer_tm, last)
        out = y.reshape(b, rows, d)
        x_lat, x_lat_off, x_ctx, x_ctx_off = out, 0, out, l // CTX_TM
    return out
```

```python
import functools
import math

import numpy as np
import jax
import jax.numpy as jnp
from jax import lax
from jax.experimental import pallas as pl
from jax.experimental.pallas import tpu as pltpu

F32 = jnp.float32
BF16 = jnp.bfloat16

EPS = 1e-6
N_MOD = 6
GRID_W = 64
ROPE_BASE = 10000.0

DN_HEADS, DN_DK, DN_DV, DN_CONV, DN_CHUNK = 4, 128, 128, 5, 64
GLA_HEADS, GLA_DK, GLA_DV, GLA_RANK, GLA_TAU, GLA_CHUNK = 4, 64, 128, 16, 16.0, 64
MLA_HEADS, MLA_NOPE, MLA_ROPE, MLA_V, MLA_Q_RANK, MLA_KV_RANK = 4, 128, 64, 128, 448, 128
FN_GROUPS, FN_CH = 4, 128
PEER_HEADS, PEER_N_KEYS, PEER_HALF, PEER_TOPK = 8, 128, 128, 16

VMEM_LIMIT_BYTES = 56 * 1024 * 1024
LANES = 128
SUBLANES = 8

C_DNQ, C_DNK, C_DNV, C_DNG = 0, 512, 1024, 1536
C_GLAV, C_GLAG, C_FN, C_CQ = 2048, 2560, 3072, 3584
C_GLAQ, C_GLAK = 4096, 4352
C_CKV, C_KR, C_SMALL = 4608, 4736, 4864
NP = 4992
PROJ_TN = NP // 3

NT_DIMS = (((1,), (1,)), ((), ()))
TN_DIMS = (((0,), (0,)), ((), ()))


def _cparams(sem):
    return pltpu.CompilerParams(dimension_semantics=sem, vmem_limit_bytes=VMEM_LIMIT_BYTES)


def _rms(x):
    return x * lax.rsqrt(jnp.mean(x * x, axis=-1, keepdims=True) + EPS)


def _ada_kernel(c_ref, w_ref, b_ref, o_ref):
    c = c_ref[...]
    a = (c * jax.nn.sigmoid(c)).astype(BF16)
    o_ref[0] = jnp.dot(a, w_ref[0].astype(BF16), preferred_element_type=F32) + b_ref[0]


def ada_mod(c_all, w_ada, b_ada):
    depth, d, n = w_ada.shape
    rows = c_all.shape[0]
    tn = 1024
    return pl.pallas_call(
        _ada_kernel,
        grid=(depth, n // tn),
        in_specs=[
            pl.BlockSpec((rows, d), lambda l, j: (0, 0)),
            pl.BlockSpec((1, d, tn), lambda l, j: (l, 0, j)),
            pl.BlockSpec((1, 1, tn), lambda l, j: (l, 0, j)),
        ],
        out_specs=pl.BlockSpec((1, rows, tn), lambda l, j: (l, 0, j)),
        out_shape=jax.ShapeDtypeStruct((depth, rows, n), F32),
        compiler_params=_cparams(("parallel", "parallel")),
        name="ada_mod",
    )(c_all, w_ada, b_ada.reshape(depth, 1, n))


def _proj_kernel(x_ref, sh_ref, sc_ref, w_ref, *rest):
    o_ref, xn_ref = rest[-2], rest[-1]

    @pl.when(pl.program_id(2) == 0)
    def _():
        xn_ref[...] = (_rms(x_ref[0]) * (1.0 + sc_ref[0]) + sh_ref[0]).astype(BF16)

    o_ref[0] = jnp.dot(xn_ref[...], w_ref[...], preferred_element_type=F32).astype(o_ref.dtype)


def proj_in(x, x_off, n_tiles, tm, shift, scale, w, out_rows, out_off, prev):
    b, _, d = x.shape
    bm = shift.shape[0]
    nj = w.shape[1] // PROJ_TN

    def mod_map(bi, i, j):
        return (bi if bm > 1 else 0, 0, 0)

    in_specs = [
        pl.BlockSpec((1, tm, d), lambda bi, i, j: (bi, i + x_off, 0)),
        pl.BlockSpec((1, 1, d), mod_map),
        pl.BlockSpec((1, 1, d), mod_map),
        pl.BlockSpec((d, PROJ_TN), lambda bi, i, j: (0, j)),
    ]
    args = [x, shift, scale, w]
    aliases = {}
    if prev is not None:
        in_specs.append(pl.BlockSpec(memory_space=pl.ANY))
        args.append(prev)
        aliases = {4: 0}
    return pl.pallas_call(
        _proj_kernel,
        grid=(b, n_tiles, nj),
        in_specs=in_specs,
        out_specs=pl.BlockSpec((1, tm, PROJ_TN), lambda bi, i, j: (bi, i + out_off, j)),
        out_shape=jax.ShapeDtypeStruct((b, out_rows, w.shape[1]), BF16),
        scratch_shapes=[pltpu.VMEM((tm, d), BF16)],
        input_output_aliases=aliases,
        compiler_params=_cparams(("parallel", "arbitrary", "arbitrary")),
        name="proj_in",
    )(*args)


def _mix_kernel(odf, odb, dg, ogf, ogb, gg, ym, yf, x_ref, g1, sh2, sc2, dnw, glw, w_ref, *rest):
    xo_ref, h_ref = rest[-2], rest[-1]

    def gated(of, ob, gate, wn):
        o = of[0].astype(F32) + ob[0].astype(F32)
        on = jnp.concatenate(
            [_rms(o[:, hh * LANES:(hh + 1) * LANES]) for hh in range(o.shape[1] // LANES)], axis=-1)
        g = gate[0].astype(F32)
        return (on * wn[...] * (g * jax.nn.sigmoid(g))).astype(BF16)

    q = w_ref.shape[0] // 4
    acc = jnp.dot(gated(odf, odb, dg, dnw), w_ref[0:q], preferred_element_type=F32)
    acc += jnp.dot(gated(ogf, ogb, gg, glw), w_ref[q:2 * q], preferred_element_type=F32)
    acc += jnp.dot(ym[0], w_ref[2 * q:3 * q], preferred_element_type=F32)
    acc += jnp.dot(yf[0], w_ref[3 * q:4 * q], preferred_element_type=F32)
    xn = x_ref[0] + g1[0] * acc
    xo_ref[0] = xn
    h_ref[0] = (_rms(xn) * (1.0 + sc2[0]) + sh2[0]).astype(BF16)


def mix_out(o_dn_f, o_dn_b, o_gla_f, o_gla_b, y_mla, y_fn, proj, off, x, x_off, n_tiles, tm,
            g1, sh2, sc2, dn_norm, gla_norm, w_out, out_rows, out_off, prev_x, prev_h):
    b, _, d = x.shape
    bm = g1.shape[0]
    wq = 512

    def mix_spec(col):
        return pl.BlockSpec((1, tm, wq), lambda bi, i: (bi, i + off, col))

    def mod_map(bi, i):
        return (bi if bm > 1 else 0, 0, 0)

    in_specs = [
        mix_spec(0), mix_spec(0), mix_spec(C_DNG // wq),
        mix_spec(0), mix_spec(0), mix_spec(C_GLAG // wq),
        mix_spec(0), mix_spec(0),
        pl.BlockSpec((1, tm, d), lambda bi, i: (bi, i + x_off, 0)),
        pl.BlockSpec((1, 1, d), mod_map), pl.BlockSpec((1, 1, d), mod_map), pl.BlockSpec((1, 1, d), mod_map),
        pl.BlockSpec((1, wq), lambda bi, i: (0, 0)), pl.BlockSpec((1, wq), lambda bi, i: (0, 0)),
        pl.BlockSpec((4 * wq, d), lambda bi, i: (0, 0)),
    ]
    args = [o_dn_f, o_dn_b, proj, o_gla_f, o_gla_b, proj, y_mla, y_fn, x, g1, sh2, sc2,
            jnp.tile(dn_norm.astype(F32), DN_HEADS).reshape(1, wq),
            jnp.tile(gla_norm.astype(F32), GLA_HEADS).reshape(1, wq), w_out]
    aliases = {}
    if prev_x is not None:
        in_specs += [pl.BlockSpec(memory_space=pl.ANY), pl.BlockSpec(memory_space=pl.ANY)]
        args += [prev_x, prev_h]
        aliases = {len(args) - 2: 0, len(args) - 1: 1}
    out_spec = pl.BlockSpec((1, tm, d), lambda bi, i: (bi, i + out_off, 0))
    return pl.pallas_call(
        _mix_kernel,
        grid=(b, n_tiles),
        in_specs=in_specs,
        out_specs=[out_spec, out_spec],
        out_shape=[jax.ShapeDtypeStruct((b, out_rows, d), F32), jax.ShapeDtypeStruct((b, out_rows, d), BF16)],
        input_output_aliases=aliases,
        compiler_params=_cparams(("parallel", "parallel")),
        name="mix_out",
    )(*args)


N_RANK = PEER_TOPK + 1
VAL_ROWS = 24
NEG = -1e30


def _candidate_tiles():
    runs = []
    for r1 in range(N_RANK):
        n = N_RANK // (r1 + 1)
        for r2_0 in range(0, n, SUBLANES):
            runs.append((r1, r2_0, min(SUBLANES, n - r2_0)))
    runs.sort(key=lambda t: -t[2])
    tiles, used = [], []
    for r1, r2_0, length in runs:
        for i in range(len(tiles)):
            if used[i] + length <= SUBLANES:
                tiles[i].append((used[i], r1, r2_0, length))
                used[i] += length
                break
        else:
            tiles.append([(0, r1, r2_0, length)])
            used.append(length)
    return tiles


_CAND_TILES = _candidate_tiles()


def _sorting_network(n):
    pairs = []
    p = 1
    while p < n:
        k = p
        while k >= 1:
            for j in range(k % p, n - k, 2 * k):
                for i in range(min(k, n - j - k)):
                    if (i + j) // (2 * p) == (i + j + k) // (2 * p):
                        pairs.append((i + j, i + j + k))
            k //= 2
        p *= 2
    return pairs


def _top_values(work, n):
    levels = [work[t * SUBLANES:(t + 1) * SUBLANES] for t in range(work.shape[0] // SUBLANES)]
    for a, b in _sorting_network(len(levels)):
        levels[a], levels[b] = jnp.maximum(levels[a], levels[b]), jnp.minimum(levels[a], levels[b])
    neg = jnp.full((SUBLANES, LANES), NEG, F32)
    vals = []
    for r in range(n):
        m = jnp.max(levels[0], axis=0, keepdims=True)
        vals.append(m)
        pop = levels[0] >= m
        depth = min(len(levels), n - r)
        for t in range(depth):
            below = levels[t + 1] if t + 1 < len(levels) else neg
            levels[t] = jnp.where(pop, below, levels[t])
    return vals


def _peer_score_kernel(h_ref, wq_ref, keys_ref, s_ref, aux_ref, vals_ref):
    tm = h_ref.shape[0]
    n_chunks = tm // LANES
    q = jnp.dot(h_ref[...], wq_ref[...], preferred_element_type=F32).astype(BF16)
    for hp in range(2 * PEER_HEADS):
        s_ref[hp] = lax.dot_general(keys_ref[hp], q[:, hp * PEER_HALF:(hp + 1) * PEER_HALF], NT_DIMS,
                                    preferred_element_type=F32)

    def stage1(hp, carry):
        for c in range(n_chunks):
            lanes = slice(c * LANES, (c + 1) * LANES)
            vals = _top_values(s_ref[hp, :, lanes], N_RANK)
            vals += [jnp.full((1, LANES), NEG, F32)] * (VAL_ROWS - N_RANK)
            vals_ref[hp, :, lanes] = jnp.concatenate(vals, axis=0)
        return carry

    lax.fori_loop(0, 2 * PEER_HEADS, stage1, 0)

    row = lax.broadcasted_iota(jnp.int32, (SUBLANES, LANES), 0)

    def stage2(hh, carry):
        for c in range(n_chunks):
            lanes = slice(c * LANES, (c + 1) * LANES)
            v1 = vals_ref[2 * hh, :, lanes]
            v2 = vals_ref[2 * hh + 1, :, lanes]
            cands = []
            for segments in _CAND_TILES:
                tile = jnp.full((SUBLANES, LANES), NEG, F32)
                for row0, r1, r2_0, length in segments:
                    blk, off = r2_0 // SUBLANES, r2_0 % SUBLANES
                    src = v2[blk * SUBLANES:(blk + 1) * SUBLANES]
                    shift = (row0 - off) % SUBLANES
                    if shift:
                        src = pltpu.roll(src, shift, axis=0)
                    tile = jnp.where((row >= row0) & (row < row0 + length), src + v1[r1:r1 + 1], tile)
                cands.append(tile)
            t0 = v1[0:1] + v2[0:1]
            taken = jnp.zeros((1, LANES), F32)
            z = jnp.zeros((1, LANES), F32)
            t_k = jnp.zeros((1, LANES), F32)
            t_k1 = jnp.zeros((1, LANES), F32)
            neg = jnp.full((SUBLANES, LANES), NEG, F32)
            levels = cands + [neg] * (SUBLANES - len(cands))
            for a, b in _sorting_network(len(levels)):
                levels[a], levels[b] = jnp.maximum(levels[a], levels[b]), jnp.minimum(levels[a], levels[b])
            for r in range(N_RANK):
                m = jnp.max(levels[0], axis=0, keepdims=True)
                pop = levels[0] >= m
                cnt = jnp.sum(jnp.where(pop, 1.0, 0.0), axis=0, keepdims=True)
                after = taken + cnt
                z = z + jnp.clip(PEER_TOPK - taken, 0.0, cnt) * jnp.exp(m - t0)
                t_k = jnp.where((taken < PEER_TOPK) & (after >= PEER_TOPK), m, t_k)
                t_k1 = jnp.where((taken < N_RANK) & (after >= N_RANK), m, t_k1)
                taken = after
                for t in range(min(len(levels), N_RANK - r)):
                    below = levels[t + 1] if t + 1 < len(levels) else neg
                    levels[t] = jnp.where(pop, below, levels[t])
            tau = 0.5 * (t_k + t_k1)
            aux = jnp.concatenate([tau, v1[0:1], v2[0:1], 1.0 / z, jnp.zeros((4, LANES), F32)], axis=0)
            aux_ref[hh, :, lanes] = aux
        return carry

    lax.fori_loop(0, PEER_HEADS, stage2, 0)


def peer_scores(h, wq, keys, tm):
    tn, d = h.shape
    nk = 2 * PEER_HEADS
    return pl.pallas_call(
        _peer_score_kernel,
        grid=(tn // tm,),
        in_specs=[
            pl.BlockSpec((tm, d), lambda i: (i, 0)),
            pl.BlockSpec(wq.shape, lambda i: (0, 0)),
            pl.BlockSpec(keys.shape, lambda i: (0, 0, 0)),
        ],
        out_specs=[
            pl.BlockSpec((nk, PEER_N_KEYS, tm), lambda i: (0, 0, i)),
            pl.BlockSpec((PEER_HEADS, SUBLANES, tm), lambda i: (0, 0, i)),
        ],
        out_shape=[jax.ShapeDtypeStruct((nk, PEER_N_KEYS, tn), F32),
                   jax.ShapeDtypeStruct((PEER_HEADS, SUBLANES, tn), F32)],
        scratch_shapes=[pltpu.VMEM((nk, VAL_ROWS, tm), F32)],
        compiler_params=_cparams(("parallel",)),
        name="peer_scores",
    )(h, wq, keys)


PEER_TE = 1024
PEER_G2_ROWS = 256


def _gelu_x2(a):
    return a * (1.0 + lax.erf(a * (1.0 / math.sqrt(2.0))))


def _peer_expert_kernel(h_ref, s_ref, aux_ref, u_ref, vt_ref, x_ref, g2_ref, fn_ref, o_ref,
                        acc_ref, e2_ref, act_ref, wt_ref, *, final):
    e = pl.program_id(1)
    tm = h_ref.shape[0]
    n_chunks = tm // LANES
    n_i = PEER_TE // PEER_N_KEYS

    @pl.when(e == 0)
    def _():
        acc_ref[...] = jnp.zeros_like(acc_ref)
        for hh in range(PEER_HEADS):
            e2_ref[hh] = jnp.exp(s_ref[2 * hh + 1] - aux_ref[hh, 2:3, :]) * (0.5 * aux_ref[hh, 3:4, :])

    act_ref[...] = lax.dot_general(u_ref[...], h_ref[...], NT_DIMS, preferred_element_type=F32)

    i0 = pl.multiple_of(e * n_i, n_i)
    thr, e1 = [], []
    for hh in range(PEER_HEADS):
        s1 = s_ref[2 * hh, pl.ds(i0, n_i), :]
        thr.append(aux_ref[hh, 0:1, :] - s1)
        e1.append(jnp.exp(s1 - aux_ref[hh, 1:2, :]))
    for il in range(n_i):
        rows = slice(il * PEER_N_KEYS, (il + 1) * PEER_N_KEYS)
        for c in range(n_chunks):
            lanes = slice(c * LANES, (c + 1) * LANES)
            g = jnp.zeros((PEER_N_KEYS, LANES), F32)
            for hh in range(PEER_HEADS):
                sel = jnp.where(s_ref[2 * hh + 1, :, lanes] >= thr[hh][il:il + 1, lanes], e2_ref[hh, :, lanes], 0.0)
                g = g + sel * e1[hh][il:il + 1, lanes]
            wt_ref[rows, lanes] = (g * _gelu_x2(act_ref[rows, lanes])).astype(BF16)

    acc_ref[...] += jnp.dot(vt_ref[...], wt_ref[...], preferred_element_type=F32)

    @pl.when(e == pl.num_programs(1) - 1)
    def _():
        y = jnp.transpose(acc_ref[...])
        for r in range(tm // PEER_G2_ROWS):
            rows = slice(r * PEER_G2_ROWS, (r + 1) * PEER_G2_ROWS)
            out = x_ref[rows, :] + g2_ref[r] * y[rows, :]
            if final:
                out = _rms(out) * fn_ref[...]
            o_ref[rows, :] = out


def peer_experts(h, s, aux, u, vt, x, g2_rows, final_norm, tm, final):
    tn, d = h.shape
    ne = u.shape[0]
    nk = 2 * PEER_HEADS
    once = pl.Buffered(1)
    return pl.pallas_call(
        functools.partial(_peer_expert_kernel, final=final),
        grid=(tn // tm, ne // PEER_TE),
        in_specs=[
            pl.BlockSpec((tm, d), lambda i, e: (i, 0), pipeline_mode=once),
            pl.BlockSpec((nk, PEER_N_KEYS, tm), lambda i, e: (0, 0, i), pipeline_mode=once),
            pl.BlockSpec((PEER_HEADS, SUBLANES, tm), lambda i, e: (0, 0, i)),
            pl.BlockSpec((PEER_TE, d), lambda i, e: (e, 0)),
            pl.BlockSpec((d, PEER_TE), lambda i, e: (0, e)),
            pl.BlockSpec((tm, d), lambda i, e: (i, 0), pipeline_mode=once),
            pl.BlockSpec((tm // PEER_G2_ROWS, 1, d), lambda i, e: (i, 0, 0)),
            pl.BlockSpec((1, d), lambda i, e: (0, 0)),
        ],
        out_specs=pl.BlockSpec((tm, d), lambda i, e: (i, 0)),
        out_shape=jax.ShapeDtypeStruct((tn, d), F32),
        scratch_shapes=[
            pltpu.VMEM((d, tm), F32),
            pltpu.VMEM((PEER_HEADS, PEER_N_KEYS, tm), F32),
            pltpu.VMEM((PEER_TE, tm), F32),
            pltpu.VMEM((PEER_TE, tm), BF16),
        ],
        compiler_params=_cparams(("parallel", "arbitrary")),
        name="peer_experts",
    )(h, s, aux, u, vt, x, g2_rows, final_norm.reshape(1, d).astype(F32))


IN_LAYOUT = (
    ("dn_qkv", 1536), ("dn_gate", 512), ("dn_a", 8), ("dn_b", 8), ("gla_q", 256), ("gla_k", 256),
    ("gla_v", 512), ("gla_gate", 512), ("gla_alpha", 32), ("mla_cq", 448), ("mla_ckv", 128),
    ("mla_kr", 64), ("fn", 512),
)


def _in_cols():
    offs, o = {}, 0
    for name, w in IN_LAYOUT:
        offs[name] = (o, o + w)
        o += w
    return offs


def _rot_half_cols(w):
    return jnp.concatenate([-w[:, 16:32], w[:, 0:16], -w[:, 48:64], w[:, 32:48]], axis=1)


def prep_w_in(w_in):
    c = _in_cols()
    d = w_in.shape[0]

    def cols(name):
        a, b = c[name]
        return w_in[:, a:b]

    z = lambda n: jnp.zeros((d, n), w_in.dtype)
    kr = cols("mla_kr")
    pieces = [
        cols("dn_qkv"), cols("dn_gate"), cols("gla_v"), cols("gla_gate"), cols("fn"),
        cols("mla_cq"), z(512 - MLA_Q_RANK), cols("gla_q"), cols("gla_k"), cols("mla_ckv"),
        kr, _rot_half_cols(kr), cols("dn_a"), cols("dn_b"), cols("gla_alpha"), z(128 - 48),
    ]
    w = jnp.concatenate(pieces, axis=1)
    assert w.shape[1] == NP
    return w.astype(BF16)


MLA_SLOT = 256


def prep_mla_weights(q_norm, w_uq, kv_norm, w_ukv):
    scale = (MLA_NOPE + MLA_ROPE) ** -0.5
    wq = w_uq * q_norm[:, None] * scale
    per = MLA_NOPE + MLA_ROPE
    cols = []
    for h in range(MLA_HEADS):
        nope = wq[:, h * per:h * per + MLA_NOPE]
        rope = wq[:, h * per + MLA_NOPE:(h + 1) * per]
        cols += [nope, rope, _rot_half_cols(rope)]
    wq_p = jnp.concatenate(cols, axis=1)
    wq_p = jnp.concatenate([wq_p, jnp.zeros((512 - MLA_Q_RANK, wq_p.shape[1]), wq_p.dtype)], axis=0)
    wkv = w_ukv * kv_norm[:, None]
    per = MLA_NOPE + MLA_V
    wkv_p = jnp.concatenate([wkv[:, h * per:h * per + MLA_NOPE] for h in range(MLA_HEADS)]
                            + [wkv[:, h * per + MLA_NOPE:(h + 1) * per] for h in range(MLA_HEADS)], axis=1)
    return wq_p.astype(BF16), wkv_p.astype(BF16)


def rope_table(l, lc):
    rows = l // GRID_W
    row = np.repeat(np.arange(rows), GRID_W).astype(np.float32)
    col = np.tile(np.arange(GRID_W), rows).astype(np.float32)
    half = MLA_ROPE // 2
    inv_freq = jnp.asarray(ROPE_BASE, F32) ** (-jnp.arange(0, half, 2, dtype=F32) / half)
    ang_r = jnp.asarray(row)[:, None] * inv_freq
    ang_c = jnp.asarray(col)[:, None] * inv_freq
    cos = jnp.concatenate([jnp.cos(ang_r)] * 2 + [jnp.cos(ang_c)] * 2, axis=1)
    sin = jnp.concatenate([jnp.sin(ang_r)] * 2 + [jnp.sin(ang_c)] * 2, axis=1)
    lat = jnp.concatenate([cos, sin], axis=1)
    ctx = jnp.concatenate([jnp.ones((lc, MLA_ROPE), F32), jnp.zeros((lc, MLA_ROPE), F32)], axis=1)
    return jnp.concatenate([lat, ctx], axis=0)


def _mla_prep_kernel(cq_ref, ckv_ref, kr_ref, cs_ref, wq_ref, wkv_ref, q_ref, k_ref, v_ref):
    cs = cs_ref[...]
    half = LANES // 2
    lane = lax.broadcasted_iota(jnp.int32, cs.shape, 1)

    def rotary(u):
        return u + pltpu.roll(u, half, axis=1)

    cq = cq_ref[0].astype(F32)
    cqn = cq * lax.rsqrt(jnp.sum(cq * cq, axis=-1, keepdims=True) * (1.0 / MLA_Q_RANK) + EPS)
    q = jnp.dot(cqn.astype(BF16), wq_ref[...], preferred_element_type=F32)
    qs = []
    for h in range(MLA_HEADS):
        qs.append(q[:, h * MLA_SLOT:h * MLA_SLOT + LANES])
        qs.append(rotary(q[:, h * MLA_SLOT + LANES:(h + 1) * MLA_SLOT] * cs))
    q_ref[0] = jnp.concatenate(qs, axis=-1).astype(BF16)

    kv = jnp.dot(_rms(ckv_ref[0].astype(F32)).astype(BF16), wkv_ref[...], preferred_element_type=F32)
    kr = jnp.where(lane < half, rotary(kr_ref[0].astype(F32) * cs), 0.0)
    ks = []
    for h in range(MLA_HEADS):
        ks += [kv[:, h * LANES:(h + 1) * LANES], kr]
    k_ref[0] = jnp.concatenate(ks, axis=-1).astype(BF16)
    v_ref[0] = kv[:, MLA_HEADS * LANES:].astype(BF16)


def mla_prep(proj, cs, wq_p, wkv_p, tm):
    b, lt, _ = proj.shape
    nq = MLA_HEADS * MLA_SLOT
    return pl.pallas_call(
        _mla_prep_kernel,
        grid=(b, lt // tm),
        in_specs=[
            pl.BlockSpec((1, tm, 512), lambda bi, i: (bi, i, C_CQ // 512)),
            pl.BlockSpec((1, tm, LANES), lambda bi, i: (bi, i, C_CKV // LANES)),
            pl.BlockSpec((1, tm, LANES), lambda bi, i: (bi, i, C_KR // LANES)),
            pl.BlockSpec((tm, LANES), lambda bi, i: (i, 0)),
            pl.BlockSpec(wq_p.shape, lambda bi, i: (0, 0)),
            pl.BlockSpec(wkv_p.shape, lambda bi, i: (0, 0)),
        ],
        out_specs=[
            pl.BlockSpec((1, tm, nq), lambda bi, i: (bi, i, 0)),
            pl.BlockSpec((1, tm, nq), lambda bi, i: (bi, i, 0)),
            pl.BlockSpec((1, tm, MLA_HEADS * MLA_V), lambda bi, i: (bi, i, 0)),
        ],
        out_shape=[jax.ShapeDtypeStruct((b, lt, nq), BF16), jax.ShapeDtypeStruct((b, lt, nq), BF16),
                   jax.ShapeDtypeStruct((b, lt, MLA_HEADS * MLA_V), BF16)],
        compiler_params=_cparams(("parallel", "parallel")),
        name="mla_prep",
    )(proj, proj, proj, cs, wq_p, wkv_p)


def _mla_attn_kernel(q_ref, k_ref, v_ref, *rest):
    o_ref = rest[-1]
    s = lax.dot_general(q_ref[0], k_ref[0], NT_DIMS, preferred_element_type=F32)
    p = jnp.exp(s - jnp.max(s, axis=-1, keepdims=True))
    o = jnp.dot(p.astype(BF16), v_ref[0], preferred_element_type=F32)
    o_ref[0] = (o / jnp.sum(p, axis=-1, keepdims=True)).astype(o_ref.dtype)


def mla_attend(q4, k4, v4, q_off, n_tiles, tq, k_rows, k_blk, prev):
    b, lt, _ = q4.shape
    in_specs = [
        pl.BlockSpec((1, tq, MLA_SLOT), lambda bi, h, i: (bi, i + q_off, h)),
        pl.BlockSpec((1, k_rows, MLA_SLOT), lambda bi, h, i: (bi, k_blk, h)),
        pl.BlockSpec((1, k_rows, MLA_V), lambda bi, h, i: (bi, k_blk, h)),
    ]
    args = [q4, k4, v4]
    aliases = {}
    if prev is not None:
        in_specs.append(pl.BlockSpec(memory_space=pl.ANY))
        args.append(prev)
        aliases = {3: 0}
    return pl.pallas_call(
        _mla_attn_kernel,
        grid=(b, MLA_HEADS, n_tiles),
        in_specs=in_specs,
        out_specs=pl.BlockSpec((1, tq, MLA_V), lambda bi, h, i: (bi, i + q_off, h)),
        out_shape=jax.ShapeDtypeStruct((b, lt, MLA_HEADS * MLA_V), BF16),
        input_output_aliases=aliases,
        compiler_params=_cparams(("parallel", "parallel", "parallel")),
        name="mla_attend",
    )(*args)


def dft_tables(n, scale):
    j = np.arange(n, dtype=np.int64)
    root = int(round(math.sqrt(n))) if int(round(math.sqrt(n))) ** 2 == n else 1
    if root == 1:
        ph = 2.0 * np.pi * ((j[:, None] * j[None, :]) % n) / n
        return jnp.asarray(np.cos(ph) * scale, BF16), jnp.asarray(np.sin(ph) * scale, BF16)
    u = np.arange(root, dtype=np.int64)
    pa = 2.0 * np.pi * ((j[:, None] * (root * u)[None, :]) % n) / n
    pb = 2.0 * np.pi * ((j[:, None] * u[None, :]) % n) / n
    ca, sa, cb, sb = (jnp.asarray(t, F32) for t in (np.cos(pa), np.sin(pa), np.cos(pb) * scale, np.sin(pb) * scale))
    cos = ca[:, :, None] * cb[:, None, :] - sa[:, :, None] * sb[:, None, :]
    sin = sa[:, :, None] * cb[:, None, :] + ca[:, :, None] * sb[:, None, :]
    return cos.reshape(n, n).astype(BF16), sin.reshape(n, n).astype(BF16)


def channel_dft_tables():
    c, s = dft_tables(FN_CH, 1.0)
    eye = jnp.eye(FN_GROUPS, dtype=F32)
    return (jnp.kron(eye, c.astype(F32)).astype(BF16), jnp.kron(eye, s.astype(F32)).astype(BF16))


def _fourier_kernel(z_ref, cl_ref, sl_ref, cc_ref, sc_ref, *rest):
    o_ref = rest[-1]
    z = z_ref[0]
    re = jnp.dot(cl_ref[...], z, preferred_element_type=F32).astype(BF16)
    im = jnp.dot(sl_ref[...], z, preferred_element_type=F32).astype(BF16)
    y = jnp.dot(re, cc_ref[...], preferred_element_type=F32) - jnp.dot(im, sc_ref[...], preferred_element_type=F32)
    o_ref[0] = y.astype(o_ref.dtype)


def fourier_mix(proj, seq_blk, n, cl, sl, cc, sc, prev):
    b, lt, _ = proj.shape
    w = FN_GROUPS * FN_CH
    tm = min(512, n)
    nt = n // tm
    in_specs = [
        pl.BlockSpec((1, n, w), lambda i, bi: (bi, seq_blk, C_FN // w)),
        pl.BlockSpec((tm, n), lambda i, bi: (i, 0)),
        pl.BlockSpec((tm, n), lambda i, bi: (i, 0)),
        pl.BlockSpec((w, w), lambda i, bi: (0, 0)),
        pl.BlockSpec((w, w), lambda i, bi: (0, 0)),
    ]
    args = [proj, cl, sl, cc, sc]
    aliases = {}
    if prev is not None:
        in_specs.append(pl.BlockSpec(memory_space=pl.ANY))
        args.append(prev)
        aliases = {5: 0}
    return pl.pallas_call(
        _fourier_kernel,
        grid=(nt, b),
        in_specs=in_specs,
        out_specs=pl.BlockSpec((1, tm, w), lambda i, bi: (bi, seq_blk * nt + i, 0)),
        out_shape=jax.ShapeDtypeStruct((b, lt, w), BF16),
        input_output_aliases=aliases,
        compiler_params=_cparams(("parallel", "parallel")),
        name="fourier_mix",
    )(*args)


CHUNK = 64
BMM_DIMS = (((2,), (1,)), ((0,), (0,)))
BMM_NT_DIMS = (((2,), (2,)), ((0,), (0,)))


STATE_GROUP = 4


def _scan_maps(l, lc):
    span = STATE_GROUP * CHUNK
    assert l % span == 0 and lc % span == 0
    n_lat, n_ctx = l // span, lc // span

    def fwd(n):
        return jnp.where(n < n_ctx, n + n_lat, n - n_ctx)

    def rev(n):
        return n_lat + n_ctx - 1 - n

    return n_lat + n_ctx, fwd, rev


def _bmm(a, b, dims=BMM_DIMS):
    return lax.dot_general(a, b, dims, preferred_element_type=F32)


def _bmm_exact_lhs(m, x):
    mb = m.astype(BF16)
    hi = x.astype(BF16)
    r1 = x - hi.astype(F32)
    mid = r1.astype(BF16)
    lo = (r1 - mid.astype(F32)).astype(BF16)
    return _bmm(mb, hi) + _bmm(mb, mid) + _bmm(mb, lo)


GLA_PREP_CHUNKS = 4
GLA_NQ = GLA_HEADS * GLA_DK
GLA_NV = GLA_HEADS * GLA_DV


def _gla_prep_kernel(q_ref, k_ref, v_ref, a_ref, w2_ref, b2_ref, m_ref, *out_refs):
    nb = 2 * GLA_PREP_CHUNKS
    inc = m_ref[...]
    lane_q = lax.broadcasted_iota(jnp.int32, (nb, CHUNK, GLA_NQ), 2)
    a, q, k, v = [], [], [], []
    for j in range(GLA_PREP_CHUNKS):
        rows = slice(j * CHUNK, (j + 1) * CHUNK)
        for d in range(2):
            a.append(a_ref[0, rows, :].astype(F32))
            q.append(q_ref[0, rows, :].astype(F32) * (GLA_DK ** -0.5))
            k.append(k_ref[0, rows, :].astype(F32))
            v.append(v_ref[0, rows, :])
    a, q, k, v = jnp.stack(a), jnp.stack(q), jnp.stack(k), jnp.stack(v)
    w2 = jnp.stack([w2_ref[it % 2] for it in range(nb)])
    b2 = jnp.stack([jnp.broadcast_to(b2_ref[it % 2], (CHUNK, GLA_NQ)) for it in range(nb)])
    z = lax.dot_general(a, w2, BMM_DIMS, preferred_element_type=F32, precision=lax.Precision.HIGHEST) + b2
    log_a = jax.nn.log_sigmoid(z) * (1.0 / GLA_TAU)
    bcum = _bmm_exact_lhs(inc, log_a)
    b_rows = [bcum[it, (0 if it % 2 else CHUNK - 1):(1 if it % 2 else CHUNK), :] for it in range(nb)]
    b_last = jnp.stack([jnp.broadcast_to(row, (CHUNK, GLA_NQ)) for row in b_rows])
    q_in = (q * jnp.exp(bcum)).astype(BF16)
    q_rel = q * jnp.exp(bcum - b_last)
    k_rel = (k * jnp.exp(b_last - bcum)).astype(BF16)
    inc_f = inc.astype(F32)
    outs = []
    for h in range(GLA_HEADS):
        qh = jnp.where(lane_q // GLA_DK == h, q_rel, 0.0).astype(BF16)
        att = (_bmm(qh, k_rel, BMM_NT_DIMS) * inc_f).astype(BF16)
        outs.append(_bmm(att, v[:, :, h * GLA_DV:(h + 1) * GLA_DV]))
    o_intra = jnp.concatenate(outs, axis=-1).astype(BF16)
    for j in range(GLA_PREP_CHUNKS):
        rows = slice(j * CHUNK, (j + 1) * CHUNK)
        for d in range(2):
            it = 2 * j + d
            o_ref, qi_ref, kr_ref, e_ref = out_refs[4 * d:4 * d + 4]
            o_ref[0, rows, :] = o_intra[it]
            qi_ref[0, rows, :] = q_in[it]
            kr_ref[0, rows, :] = k_rel[it]
            e_ref[0, j] = jnp.exp(jnp.broadcast_to(b_rows[it], (SUBLANES, GLA_NQ)))


def _gla_state_kernel(*refs):
    (oif, qif, krf, ef, vf, oir, qir, krr, er, vr, of_ref, or_ref, st_ref) = refs

    @pl.when(pl.program_id(1) == 0)
    def _():
        st_ref[...] = jnp.zeros_like(st_ref)

    bd = (lax.broadcasted_iota(jnp.int32, (GLA_NV, GLA_NQ), 0) // GLA_DV) == (
        lax.broadcasted_iota(jnp.int32, (GLA_NV, GLA_NQ), 1) // GLA_DK)
    groups = ((oif, qif, krf, ef, vf, of_ref), (oir, qir, krr, er, vr, or_ref))
    for jj in range(STATE_GROUP):
        for d, (oi_ref, qi_ref, kr_ref, e_ref, v_ref, o_ref) in enumerate(groups):
            j = STATE_GROUP - 1 - jj if d == 1 else jj
            rows = slice(j * CHUNK, (j + 1) * CHUNK)
            st = st_ref[d]
            o_inter = lax.dot_general(qi_ref[0, rows, :], st.astype(BF16), NT_DIMS, preferred_element_type=F32)
            o_ref[0, rows, :] = (oi_ref[0, rows, :].astype(F32) + o_inter).astype(o_ref.dtype)
            upd = lax.dot_general(v_ref[0, rows, :], kr_ref[0, rows, :], TN_DIMS, preferred_element_type=F32)
            st_ref[d] = st * e_ref[0, j, 0:1, :] + jnp.where(bd, upd, 0.0)


def gla_scan(proj, w2, b2, l, lc):
    b, lt, _ = proj.shape
    n_steps, fwd, rev = _scan_maps(l, lc)
    nq, nv = GLA_NQ, GLA_NV
    n_chunks = lt // CHUNK
    rows = GLA_PREP_CHUNKS * CHUNK
    w2p = jnp.zeros((2, LANES, nq), F32)
    for d in range(2):
        lo = 2 * DN_HEADS * 2 + GLA_RANK * d
        w2p = w2p.at[d, lo:lo + GLA_RANK].set(w2[d].astype(F32))
    r, c = np.meshgrid(np.arange(CHUNK), np.arange(CHUNK), indexing="ij")
    inc = jnp.asarray(np.stack([(c >= r) if it % 2 else (c <= r) for it in range(2 * GLA_PREP_CHUNKS)])
                      .astype(np.float32), BF16)

    wide = jax.ShapeDtypeStruct((b, lt, nv), BF16)
    mid = jax.ShapeDtypeStruct((b, lt, nq), BF16)
    small = jax.ShapeDtypeStruct((b, n_chunks, SUBLANES, nq), F32)
    wide_spec = pl.BlockSpec((1, rows, nv), lambda bi, i: (bi, i, 0))
    mid_spec = pl.BlockSpec((1, rows, nq), lambda bi, i: (bi, i, 0))
    small_spec = pl.BlockSpec((1, GLA_PREP_CHUNKS, SUBLANES, nq), lambda bi, i: (bi, i, 0, 0))
    terms = pl.pallas_call(
        _gla_prep_kernel,
        grid=(b, n_chunks // GLA_PREP_CHUNKS),
        in_specs=[
            pl.BlockSpec((1, rows, nq), lambda bi, i: (bi, i, C_GLAQ // nq)),
            pl.BlockSpec((1, rows, nq), lambda bi, i: (bi, i, C_GLAK // nq)),
            pl.BlockSpec((1, rows, nv), lambda bi, i: (bi, i, C_GLAV // nv)),
            pl.BlockSpec((1, rows, LANES), lambda bi, i: (bi, i, C_SMALL // LANES)),
            pl.BlockSpec((2, LANES, nq), lambda bi, i: (0, 0, 0)),
            pl.BlockSpec((2, 1, nq), lambda bi, i: (0, 0, 0)),
            pl.BlockSpec(inc.shape, lambda bi, i: (0, 0, 0)),
        ],
        out_specs=[wide_spec, mid_spec, mid_spec, small_spec] * 2,
        out_shape=[wide, mid, mid, small] * 2,
        compiler_params=_cparams(("parallel", "parallel")),
        name="gla_prep",
    )(proj, proj, proj, proj, w2p, b2.astype(F32).reshape(2, 1, nq), inc)

    span = STATE_GROUP * CHUNK

    def specs(cmap):
        return [pl.BlockSpec((1, span, nv), lambda bi, n: (bi, cmap(n), 0)),
                pl.BlockSpec((1, span, nq), lambda bi, n: (bi, cmap(n), 0)),
                pl.BlockSpec((1, span, nq), lambda bi, n: (bi, cmap(n), 0)),
                pl.BlockSpec((1, STATE_GROUP, SUBLANES, nq), lambda bi, n: (bi, cmap(n), 0, 0)),
                pl.BlockSpec((1, span, nv), lambda bi, n: (bi, cmap(n), C_GLAV // nv))]

    return pl.pallas_call(
        _gla_state_kernel,
        grid=(b, n_steps),
        in_specs=specs(fwd) + specs(rev),
        out_specs=[pl.BlockSpec((1, span, nv), lambda bi, n: (bi, fwd(n), 0)),
                   pl.BlockSpec((1, span, nv), lambda bi, n: (bi, rev(n), 0))],
        out_shape=[wide, wide],
        scratch_shapes=[pltpu.VMEM((2, nv, nq), F32)],
        compiler_params=_cparams(("parallel", "arbitrary")),
        name="gla_state",
    )(*terms[0:4], proj, *terms[4:8], proj)


DN_TL = 256
DN_W = DN_HEADS * DN_DK


def _dn_conv_kernel(x_ref, p_ref, n_ref, w_ref, o_ref, *, first_tiles, last_tiles):
    i = pl.program_id(1)
    tl = x_ref.shape[1]
    is_first = functools.reduce(jnp.logical_or, [i == t for t in first_tiles])
    is_last = functools.reduce(jnp.logical_or, [i == t for t in last_tiles])
    prev = jnp.where(is_first, 0.0, p_ref[0].astype(F32))
    nxt = jnp.where(is_last, 0.0, n_ref[0].astype(F32))
    ext = jnp.concatenate([prev, x_ref[0].astype(F32), nxt], axis=0)
    pad = DN_CONV // 2
    acc = jnp.zeros((tl, ext.shape[1]), F32)
    for t in range(DN_CONV):
        start = SUBLANES - pad + t
        acc = acc + ext[start:start + tl, :] * w_ref[t:t + 1, :]
    y = acc * jax.nn.sigmoid(acc)
    outs = []
    for h in range(3 * DN_HEADS):
        yh = y[:, h * DN_DK:(h + 1) * DN_DK]
        if h < 2 * DN_HEADS:
            yh = yh * lax.rsqrt(jnp.sum(yh * yh, axis=-1, keepdims=True) + EPS)
            if h < DN_HEADS:
                yh = yh * (DN_DK ** -0.5)
        outs.append(yh)
    o_ref[0] = jnp.concatenate(outs, axis=-1).astype(o_ref.dtype)


def dn_conv_norm(proj, conv_w, l, lc):
    b, lt, _ = proj.shape
    w = 3 * DN_W
    tl = DN_TL
    hb = tl // SUBLANES
    n_halo = lt // SUBLANES
    first_tiles = (0, l // tl)
    last_tiles = (l // tl - 1, lt // tl - 1)
    return pl.pallas_call(
        functools.partial(_dn_conv_kernel, first_tiles=first_tiles, last_tiles=last_tiles),
        grid=(b, lt // tl),
        in_specs=[
            pl.BlockSpec((1, tl, w), lambda bi, i: (bi, i, 0)),
            pl.BlockSpec((1, SUBLANES, w), lambda bi, i: (bi, jnp.maximum(i * hb - 1, 0), 0)),
            pl.BlockSpec((1, SUBLANES, w), lambda bi, i: (bi, jnp.minimum((i + 1) * hb, n_halo - 1), 0)),
            pl.BlockSpec((DN_CONV, w), lambda bi, i: (0, 0)),
        ],
        out_specs=pl.BlockSpec((1, tl, w), lambda bi, i: (bi, i, 0)),
        out_shape=jax.ShapeDtypeStruct((b, lt, w), BF16),
        compiler_params=_cparams(("parallel", "parallel")),
        name="dn_conv",
    )(proj, proj, proj, conv_w.astype(F32))


def _stack_heads(x, width):
    return jnp.concatenate([x[:, h * width:(h + 1) * width] for h in range(DN_HEADS)], axis=0)


DN_PREP_CHUNKS = 2
DN_N = DN_HEADS * CHUNK


def dn_masks():
    n = DN_N
    r, c = np.meshgrid(np.arange(n), np.arange(n), indexing="ij")
    same = (r // CHUNK) == (c // CHUNK)
    kinds = [[r == c] * 2, [same & (c <= r), same & (c >= r)], [same & (c < r), same & (c > r)]]
    size = 1
    while size < CHUNK:
        blk = (r // (2 * size)) == (c // (2 * size))
        lo_r, lo_c = (r % (2 * size)) < size, (c % (2 * size)) < size
        kinds.append([blk & ~lo_r & lo_c, blk & lo_r & ~lo_c])
        size *= 2
    m = np.stack([np.stack([k[it % 2] for it in range(2 * DN_PREP_CHUNKS)]) for k in kinds])
    return jnp.asarray(m.astype(np.float32), BF16)


def _dn_prep_kernel(x_ref, a_ref, gp_ref, m_ref, *out_refs):
    n = DN_N
    nb = 2 * DN_PREP_CHUNKS
    shape = (nb, n, n)
    eye, inc, strict = m_ref[0], m_ref[1], m_ref[2]

    def pair_mask(size):
        return m_ref[3 + int(math.log2(size))]

    qs, ks, vs, g_col, b_col = [], [], [], [], []
    for j in range(DN_PREP_CHUNKS):
        rows = slice(j * CHUNK, (j + 1) * CHUNK)
        x = x_ref[0, rows, :].astype(F32)
        a = a_ref[0, rows, :].astype(F32)
        g8 = gp_ref[0:1, :] * jax.nn.softplus(a + gp_ref[1:2, :])
        beta8 = jax.nn.sigmoid(a)
        for d in range(2):
            qs.append(_stack_heads(x[:, 0:DN_W], DN_DK))
            ks.append(_stack_heads(x[:, DN_W:2 * DN_W], DN_DK))
            vs.append(_stack_heads(x[:, 2 * DN_W:3 * DN_W], DN_DV))
            lo = DN_HEADS * d
            g_col.append(jnp.concatenate([g8[:, lo + h:lo + h + 1] for h in range(DN_HEADS)], axis=0))
            lo = 2 * DN_HEADS + DN_HEADS * d
            b_col.append(jnp.concatenate([beta8[:, lo + h:lo + h + 1] for h in range(DN_HEADS)], axis=0))
    qs, ks, vs = jnp.stack(qs), jnp.stack(ks), jnp.stack(vs)
    g_b = jnp.broadcast_to(jnp.stack(g_col), shape)
    beta_b = jnp.broadcast_to(jnp.stack(b_col), shape)
    diff = _bmm_exact_lhs(inc, g_b * strict.astype(F32))
    decay = jnp.exp(diff) * inc.astype(F32)
    gc = []
    for it in range(nb):
        first = CHUNK - 1 if it % 2 else 0
        blocks = []
        for h in range(DN_HEADS):
            col = h * CHUNK + first
            cum = diff[it, h * CHUNK:(h + 1) * CHUNK, col:col + 1] + g_col[it][col:col + 1, :]
            blocks.append(jnp.broadcast_to(cum, (CHUNK, DN_DK)))
        gc.append(jnp.concatenate(blocks, axis=0))
    gc = jnp.stack(gc)
    kb = ks.astype(BF16)
    kk = _bmm(kb, kb, BMM_NT_DIMS)
    a_mat = beta_b * kk * decay * strict.astype(F32)
    a_b = a_mat.astype(BF16)
    inv = eye.astype(F32) - a_mat * pair_mask(1).astype(F32)
    size = 2
    while size < CHUNK:
        t = _bmm(inv.astype(BF16), a_b * pair_mask(size))
        inv = inv - _bmm(t.astype(BF16), inv.astype(BF16))
        size *= 2
    beta_k = beta_b[:, :, :DN_DK]
    e_gc = jnp.exp(gc)
    invb = inv.astype(BF16)
    u = _bmm(invb, (vs * beta_k).astype(BF16))
    w = _bmm(invb, (ks * beta_k * e_gc).astype(BF16))
    p = (_bmm(qs.astype(BF16), kb, BMM_NT_DIMS) * decay).astype(BF16)
    q_dec = qs * e_gc
    for j in range(DN_PREP_CHUNKS):
        for d in range(2):
            it = 2 * j + d
            last = 0 if d == 1 else CHUNK - 1
            g_end = [gc[it, h * CHUNK + last:h * CHUNK + last + 1, :] for h in range(DN_HEADS)]
            gl = jnp.concatenate([jnp.broadcast_to(g, (CHUNK, DN_DK)) for g in g_end], axis=0)
            k_dec = ks[it] * jnp.exp(gl - gc[it])
            uw_ref, qk_ref, p_ref, e_ref = out_refs[4 * d:4 * d + 4]
            uw_ref[0, j] = jnp.concatenate([u[it], w[it]], axis=-1).astype(BF16)
            qk_ref[0, j] = jnp.concatenate([q_dec[it], k_dec], axis=-1).astype(BF16)
            p_ref[0, j] = p[it]
            e_ref[0, j] = jnp.exp(jnp.concatenate(g_end + [jnp.zeros((SUBLANES - DN_HEADS, DN_DK), F32)], axis=0))


def dn_prep(qkv, proj, a_log, dt_bias):
    b, lt, _ = qkv.shape
    w = 3 * DN_W
    n_chunks = lt // CHUNK
    gp = jnp.zeros((SUBLANES, LANES), F32)
    gp = gp.at[0, :2 * DN_HEADS].set(-jnp.exp(a_log.astype(F32)).reshape(-1))
    gp = gp.at[1, :2 * DN_HEADS].set(dt_bias.astype(F32).reshape(-1))
    rows = DN_PREP_CHUNKS * CHUNK
    big = jax.ShapeDtypeStruct((b, n_chunks, DN_N, 2 * DN_DK), BF16)
    small = jax.ShapeDtypeStruct((b, n_chunks, SUBLANES, DN_DK), F32)
    big_spec = pl.BlockSpec((1, DN_PREP_CHUNKS, DN_N, 2 * DN_DK), lambda bi, i: (bi, i, 0, 0))
    small_spec = pl.BlockSpec((1, DN_PREP_CHUNKS, SUBLANES, DN_DK), lambda bi, i: (bi, i, 0, 0))
    masks = dn_masks()
    return pl.pallas_call(
        _dn_prep_kernel,
        grid=(b, n_chunks // DN_PREP_CHUNKS),
        in_specs=[pl.BlockSpec((1, rows, w), lambda bi, i: (bi, i, 0)),
                  pl.BlockSpec((1, rows, LANES), lambda bi, i: (bi, i, C_SMALL // LANES)),
                  pl.BlockSpec((SUBLANES, LANES), lambda bi, i: (0, 0)),
                  pl.BlockSpec(masks.shape, lambda bi, i: (0, 0, 0, 0))],
        out_specs=[big_spec, big_spec, big_spec, small_spec] * 2,
        out_shape=[big, big, big, small] * 2,
        compiler_params=_cparams(("parallel", "parallel")),
        name="dn_prep",
    )(qkv, proj, gp, masks)


def _dn_state_kernel(*refs):
    (uwf, qkf, pf, ef, uwr, qkr, pr, er, of_ref, or_ref, s_ref) = refs

    @pl.when(pl.program_id(1) == 0)
    def _():
        s_ref[...] = jnp.zeros_like(s_ref)

    groups = ((uwf, qkf, pf, ef, of_ref), (uwr, qkr, pr, er, or_ref))
    for jj in range(STATE_GROUP):
        for d, (uw_ref, qk_ref, p_ref, e_ref, o_ref) in enumerate(groups):
            j = STATE_GROUP - 1 - jj if d == 1 else jj
            uw = uw_ref[0, j]
            qk = qk_ref[0, j]
            v_new, o_state = [], []
            for h in range(DN_HEADS):
                rows = slice(h * CHUNK, (h + 1) * CHUNK)
                sb = s_ref[d, h].astype(BF16)
                v_new.append(uw[rows, :DN_DV].astype(F32)
                             - jnp.dot(uw[rows, DN_DV:], sb, preferred_element_type=F32))
                o_state.append(jnp.dot(qk[rows, :DN_DK], sb, preferred_element_type=F32))
            v_new = jnp.concatenate(v_new, axis=0).astype(BF16)
            o_st = jnp.dot(p_ref[0, j], v_new, preferred_element_type=F32)
            outs = []
            for h in range(DN_HEADS):
                rows = slice(h * CHUNK, (h + 1) * CHUNK)
                outs.append(o_st[rows] + o_state[h])
                s_ref[d, h] = s_ref[d, h] * e_ref[0, j, h:h + 1, :] + lax.dot_general(
                    qk[rows, DN_DK:], v_new[rows], TN_DIMS, preferred_element_type=F32)
            o_ref[0, j * CHUNK:(j + 1) * CHUNK, :] = jnp.concatenate(outs, axis=-1).astype(o_ref.dtype)


def dn_scan(qkv, proj, a_log, dt_bias, l, lc):
    b, lt, _ = qkv.shape
    n_steps, fwd, rev = _scan_maps(l, lc)
    terms = dn_prep(qkv, proj, a_log, dt_bias)
    g = STATE_GROUP

    def specs(cmap):
        big = pl.BlockSpec((1, g, DN_N, 2 * DN_DK), lambda bi, n: (bi, cmap(n), 0, 0))
        small = pl.BlockSpec((1, g, SUBLANES, DN_DK), lambda bi, n: (bi, cmap(n), 0, 0))
        return [big, big, big, small]

    out_sd = jax.ShapeDtypeStruct((b, lt, DN_HEADS * DN_DV), BF16)
    return pl.pallas_call(
        _dn_state_kernel,
        grid=(b, n_steps),
        in_specs=specs(fwd) + specs(rev),
        out_specs=[pl.BlockSpec((1, g * CHUNK, DN_HEADS * DN_DV), lambda bi, n: (bi, fwd(n), 0)),
                   pl.BlockSpec((1, g * CHUNK, DN_HEADS * DN_DV), lambda bi, n: (bi, rev(n), 0))],
        out_shape=[out_sd, out_sd],
        scratch_shapes=[pltpu.VMEM((2, DN_HEADS, DN_DK, DN_DV), F32)],
        compiler_params=_cparams(("parallel", "arbitrary")),
        name="dn_scan",
    )(*terms)


CTX_TM = 256


def kernel(x, c, ctx, c_ctx, w_ada, b_ada, w_in, dn_conv, dn_a_log, dn_dt_bias, dn_norm, gla_w2, gla_b2,
           gla_norm, mla_q_norm, mla_w_uq, mla_kv_norm, mla_w_ukv, w_out, peer_wq, peer_keys, peer_u, peer_v,
           final_norm):
    b, l, d = x.shape
    lc = ctx.shape[1]
    lt = l + lc
    depth = w_ada.shape[0]
    lat_tm = min(1024, l)
    mix_tm = min(512, l)
    peer_tm = 512
    assert lc == CTX_TM and l % lat_tm == 0 and l % CTX_TM == 0

    pad = (-(b + 1)) % SUBLANES
    c_all = jnp.concatenate([c, c_ctx[None, :], jnp.zeros((pad, d), F32)], axis=0)
    mod = ada_mod(c_all, w_ada, b_ada)

    cs = rope_table(l, lc)
    cc, sc = channel_dft_tables()
    cl_lat, sl_lat = dft_tables(l, (l * FN_CH) ** -0.5)
    cl_ctx, sl_ctx = dft_tables(lc, (lc * FN_CH) ** -0.5)
    ctx_blk = l // lc

    x_lat, x_lat_off, x_ctx, x_ctx_off = x, 0, ctx, 0
    out = None
    for li in range(depth):
        last = li == depth - 1
        m_lat = [mod[li, :b, k * d:(k + 1) * d].reshape(b, 1, d) for k in range(N_MOD)]
        m_ctx = [mod[li, b:b + 1, k * d:(k + 1) * d].reshape(1, 1, d) for k in range(N_MOD)]

        w_p = prep_w_in(w_in[li])
        proj = proj_in(x_lat, x_lat_off, l // lat_tm, lat_tm, m_lat[0], m_lat[1], w_p, lt, 0, None)
        proj = proj_in(x_ctx, x_ctx_off, 1, CTX_TM, m_ctx[0], m_ctx[1], w_p, lt, l // CTX_TM, proj)

        wq_p, wkv_p = prep_mla_weights(mla_q_norm[li], mla_w_uq[li], mla_kv_norm[li], mla_w_ukv[li])
        q4, k4, v4 = mla_prep(proj, cs, wq_p, wkv_p, CTX_TM)
        y_mla = mla_attend(q4, k4, v4, 0, l // CTX_TM, CTX_TM, lt, 0, None)
        y_fn = fourier_mix(proj, 0, l, cl_lat, sl_lat, cc, sc, None)
        if not last:
            y_mla = mla_attend(q4, k4, v4, l // CTX_TM, lc // CTX_TM, CTX_TM, lc, ctx_blk, y_mla)
            y_fn = fourier_mix(proj, ctx_blk, lc, cl_ctx, sl_ctx, cc, sc, y_fn)
        gla_f, gla_b = gla_scan(proj, gla_w2[li], gla_b2[li], l, lc)
        qkv = dn_conv_norm(proj, dn_conv[li], l, lc)
        dn_f, dn_b = dn_scan(qkv, proj, dn_a_log[li], dn_dt_bias[li], l, lc)

        w_o = w_out[li].astype(BF16)
        rows = l if last else lt
        xn, h = mix_out(dn_f, dn_b, gla_f, gla_b, y_mla, y_fn, proj, 0,
                        x_lat, x_lat_off, l // mix_tm, mix_tm, m_lat[2], m_lat[3], m_lat[4],
                        dn_norm[li], gla_norm[li], w_o, rows, 0, None, None)
        if not last:
            xn, h = mix_out(dn_f, dn_b, gla_f, gla_b, y_mla, y_fn, proj,
                            l // CTX_TM, x_ctx, x_ctx_off, 1, CTX_TM, m_ctx[2], m_ctx[3], m_ctx[4],
                            dn_norm[li], gla_norm[li], w_o, rows, l // CTX_TM, xn, h)

        tn = b * rows
        g2_lat = jnp.broadcast_to(m_lat[5], (b, l // PEER_G2_ROWS, d))
        if last:
            g2_rows = g2_lat.reshape(tn // PEER_G2_ROWS, 1, d)
        else:
            g2_ctx = jnp.broadcast_to(m_ctx[5], (b, lc // PEER_G2_ROWS, d))
            g2_rows = jnp.concatenate([g2_lat, g2_ctx], axis=1).reshape(tn // PEER_G2_ROWS, 1, d)
        hf = h.reshape(tn, d)
        s, aux = peer_scores(hf, peer_wq[li].astype(BF16),
                             peer_keys[li].reshape(2 * PEER_HEADS, PEER_N_KEYS, PEER_HALF).astype(BF16), peer_tm)
        y = peer_experts(hf, s, aux, peer_u[li].astype(BF16), peer_v[li].T.astype(BF16), xn.reshape(tn, d),
                         g2_rows, final_norm, peer_tm, last)
        out = y.reshape(b, rows, d)
        x_lat, x_lat_off, x_ctx, x_ctx_off = out, 0, out, l // CTX_TM
    return out
```

```python
import functools
import math

import numpy as np
import jax
import jax.numpy as jnp
from jax import lax
from jax.experimental import pallas as pl
from jax.experimental.pallas import tpu as pltpu

F32 = jnp.float32
BF16 = jnp.bfloat16

EPS = 1e-6
N_MOD = 6
GRID_W = 64
ROPE_BASE = 10000.0

DN_HEADS, DN_DK, DN_DV, DN_CONV, DN_CHUNK = 4, 128, 128, 5, 64
GLA_HEADS, GLA_DK, GLA_DV, GLA_RANK, GLA_TAU, GLA_CHUNK = 4, 64, 128, 16, 16.0, 64
MLA_HEADS, MLA_NOPE, MLA_ROPE, MLA_V, MLA_Q_RANK, MLA_KV_RANK = 4, 128, 64, 128, 448, 128
FN_GROUPS, FN_CH = 4, 128
PEER_HEADS, PEER_N_KEYS, PEER_HALF, PEER_TOPK = 8, 128, 128, 16

VMEM_LIMIT_BYTES = 56 * 1024 * 1024
LANES = 128
SUBLANES = 8

C_DNQ, C_DNK, C_DNV, C_DNG = 0, 512, 1024, 1536
C_GLAV, C_GLAG, C_FN, C_CQ = 2048, 2560, 3072, 3584
C_GLAQ, C_GLAK = 4096, 4352
C_CKV, C_KR, C_SMALL = 4608, 4736, 4864
NP = 4992
PROJ_TN = NP // 3

NT_DIMS = (((1,), (1,)), ((), ()))
TN_DIMS = (((0,), (0,)), ((), ()))


def _cparams(sem):
    return pltpu.CompilerParams(dimension_semantics=sem, vmem_limit_bytes=VMEM_LIMIT_BYTES)


def _rms(x):
    return x * lax.rsqrt(jnp.mean(x * x, axis=-1, keepdims=True) + EPS)


def _ada_kernel(c_ref, w_ref, b_ref, o_ref):
    c = c_ref[...]
    a = (c * jax.nn.sigmoid(c)).astype(BF16)
    o_ref[0] = jnp.dot(a, w_ref[0].astype(BF16), preferred_element_type=F32) + b_ref[0]


def ada_mod(c_all, w_ada, b_ada):
    depth, d, n = w_ada.shape
    rows = c_all.shape[0]
    tn = 1024
    return pl.pallas_call(
        _ada_kernel,
        grid=(depth, n // tn),
        in_specs=[
            pl.BlockSpec((rows, d), lambda l, j: (0, 0)),
            pl.BlockSpec((1, d, tn), lambda l, j: (l, 0, j)),
            pl.BlockSpec((1, 1, tn), lambda l, j: (l, 0, j)),
        ],
        out_specs=pl.BlockSpec((1, rows, tn), lambda l, j: (l, 0, j)),
        out_shape=jax.ShapeDtypeStruct((depth, rows, n), F32),
        compiler_params=_cparams(("parallel", "parallel")),
        name="ada_mod",
    )(c_all, w_ada, b_ada.reshape(depth, 1, n))


def _proj_kernel(x_ref, sh_ref, sc_ref, w_ref, *rest):
    o_ref, xn_ref = rest[-2], rest[-1]

    @pl.when(pl.program_id(2) == 0)
    def _():
        xn_ref[...] = (_rms(x_ref[0]) * (1.0 + sc_ref[0]) + sh_ref[0]).astype(BF16)

    o_ref[0] = jnp.dot(xn_ref[...], w_ref[...], preferred_element_type=F32).astype(o_ref.dtype)


def proj_in(x, x_off, n_tiles, tm, shift, scale, w, out_rows, out_off, prev):
    b, _, d = x.shape
    bm = shift.shape[0]
    nj = w.shape[1] // PROJ_TN

    def mod_map(bi, i, j):
        return (bi if bm > 1 else 0, 0, 0)

    in_specs = [
        pl.BlockSpec((1, tm, d), lambda bi, i, j: (bi, i + x_off, 0)),
        pl.BlockSpec((1, 1, d), mod_map),
        pl.BlockSpec((1, 1, d), mod_map),
        pl.BlockSpec((d, PROJ_TN), lambda bi, i, j: (0, j)),
    ]
    args = [x, shift, scale, w]
    aliases = {}
    if prev is not None:
        in_specs.append(pl.BlockSpec(memory_space=pl.ANY))
        args.append(prev)
        aliases = {4: 0}
    return pl.pallas_call(
        _proj_kernel,
        grid=(b, n_tiles, nj),
        in_specs=in_specs,
        out_specs=pl.BlockSpec((1, tm, PROJ_TN), lambda bi, i, j: (bi, i + out_off, j)),
        out_shape=jax.ShapeDtypeStruct((b, out_rows, w.shape[1]), BF16),
        scratch_shapes=[pltpu.VMEM((tm, d), BF16)],
        input_output_aliases=aliases,
        compiler_params=_cparams(("parallel", "arbitrary", "arbitrary")),
        name="proj_in",
    )(*args)


def _mix_kernel(odf, odb, dg, ogf, ogb, gg, ym, yf, x_ref, g1, sh2, sc2, dnw, glw, w_ref, *rest):
    xo_ref, h_ref = rest[-2], rest[-1]

    def gated(of, ob, gate, wn):
        o = of[0].astype(F32) + ob[0].astype(F32)
        on = jnp.concatenate(
            [_rms(o[:, hh * LANES:(hh + 1) * LANES]) for hh in range(o.shape[1] // LANES)], axis=-1)
        g = gate[0].astype(F32)
        return (on * wn[...] * (g * jax.nn.sigmoid(g))).astype(BF16)

    q = w_ref.shape[0] // 4
    acc = jnp.dot(gated(odf, odb, dg, dnw), w_ref[0:q], preferred_element_type=F32)
    acc += jnp.dot(gated(ogf, ogb, gg, glw), w_ref[q:2 * q], preferred_element_type=F32)
    acc += jnp.dot(ym[0], w_ref[2 * q:3 * q], preferred_element_type=F32)
    acc += jnp.dot(yf[0], w_ref[3 * q:4 * q], preferred_element_type=F32)
    xn = x_ref[0] + g1[0] * acc
    xo_ref[0] = xn
    h_ref[0] = (_rms(xn) * (1.0 + sc2[0]) + sh2[0]).astype(BF16)


def mix_out(o_dn_f, o_dn_b, o_gla_f, o_gla_b, y_mla, y_fn, proj, off, x, x_off, n_tiles, tm,
            g1, sh2, sc2, dn_norm, gla_norm, w_out, out_rows, out_off, prev_x, prev_h):
    b, _, d = x.shape
    bm = g1.shape[0]
    wq = 512

    def mix_spec(col):
        return pl.BlockSpec((1, tm, wq), lambda bi, i: (bi, i + off, col))

    def mod_map(bi, i):
        return (bi if bm > 1 else 0, 0, 0)

    in_specs = [
        mix_spec(0), mix_spec(0), mix_spec(C_DNG // wq),
        mix_spec(0), mix_spec(0), mix_spec(C_GLAG // wq),
        mix_spec(0), mix_spec(0),
        pl.BlockSpec((1, tm, d), lambda bi, i: (bi, i + x_off, 0)),
        pl.BlockSpec((1, 1, d), mod_map), pl.BlockSpec((1, 1, d), mod_map), pl.BlockSpec((1, 1, d), mod_map),
        pl.BlockSpec((1, wq), lambda bi, i: (0, 0)), pl.BlockSpec((1, wq), lambda bi, i: (0, 0)),
        pl.BlockSpec((4 * wq, d), lambda bi, i: (0, 0)),
    ]
    args = [o_dn_f, o_dn_b, proj, o_gla_f, o_gla_b, proj, y_mla, y_fn, x, g1, sh2, sc2,
            jnp.tile(dn_norm.astype(F32), DN_HEADS).reshape(1, wq),
            jnp.tile(gla_norm.astype(F32), GLA_HEADS).reshape(1, wq), w_out]
    aliases = {}
    if prev_x is not None:
        in_specs += [pl.BlockSpec(memory_space=pl.ANY), pl.BlockSpec(memory_space=pl.ANY)]
        args += [prev_x, prev_h]
        aliases = {len(args) - 2: 0, len(args) - 1: 1}
    out_spec = pl.BlockSpec((1, tm, d), lambda bi, i: (bi, i + out_off, 0))
    return pl.pallas_call(
        _mix_kernel,
        grid=(b, n_tiles),
        in_specs=in_specs,
        out_specs=[out_spec, out_spec],
        out_shape=[jax.ShapeDtypeStruct((b, out_rows, d), F32), jax.ShapeDtypeStruct((b, out_rows, d), BF16)],
        input_output_aliases=aliases,
        compiler_params=_cparams(("parallel", "parallel")),
        name="mix_out",
    )(*args)


N_RANK = PEER_TOPK + 1
VAL_ROWS = 24
NEG = -1e30


def _candidate_tiles():
    runs = []
    for r1 in range(N_RANK):
        n = N_RANK // (r1 + 1)
        for r2_0 in range(0, n, SUBLANES):
            runs.append((r1, r2_0, min(SUBLANES, n - r2_0)))
    runs.sort(key=lambda t: -t[2])
    tiles, used = [], []
    for r1, r2_0, length in runs:
        for i in range(len(tiles)):
            if used[i] + length <= SUBLANES:
                tiles[i].append((used[i], r1, r2_0, length))
                used[i] += length
                break
        else:
            tiles.append([(0, r1, r2_0, length)])
            used.append(length)
    return tiles


_CAND_TILES = _candidate_tiles()


def _sorting_network(n):
    pairs = []
    p = 1
    while p < n:
        k = p
        while k >= 1:
            for j in range(k % p, n - k, 2 * k):
                for i in range(min(k, n - j - k)):
                    if (i + j) // (2 * p) == (i + j + k) // (2 * p):
                        pairs.append((i + j, i + j + k))
            k //= 2
        p *= 2
    return pairs


def _top_values(work, n):
    levels = [work[t * SUBLANES:(t + 1) * SUBLANES] for t in range(work.shape[0] // SUBLANES)]
    for a, b in _sorting_network(len(levels)):
        levels[a], levels[b] = jnp.maximum(levels[a], levels[b]), jnp.minimum(levels[a], levels[b])
    neg = jnp.full((SUBLANES, LANES), NEG, F32)
    vals = []
    for r in range(n):
        m = jnp.max(levels[0], axis=0, keepdims=True)
        vals.append(m)
        pop = levels[0] >= m
        depth = min(len(levels), n - r)
        for t in range(depth):
            below = levels[t + 1] if t + 1 < len(levels) else neg
            levels[t] = jnp.where(pop, below, levels[t])
    return vals


def _peer_score_kernel(h_ref, wq_ref, keys_ref, s_ref, aux_ref, vals_ref):
    tm = h_ref.shape[0]
    n_chunks = tm // LANES
    q = jnp.dot(h_ref[...], wq_ref[...], preferred_element_type=F32).astype(BF16)
    for hp in range(2 * PEER_HEADS):
        s_ref[hp] = lax.dot_general(keys_ref[hp], q[:, hp * PEER_HALF:(hp + 1) * PEER_HALF], NT_DIMS,
                                    preferred_element_type=F32)

    def stage1(hp, carry):
        for c in range(n_chunks):
            lanes = slice(c * LANES, (c + 1) * LANES)
            vals = _top_values(s_ref[hp, :, lanes], N_RANK)
            vals += [jnp.full((1, LANES), NEG, F32)] * (VAL_ROWS - N_RANK)
            vals_ref[hp, :, lanes] = jnp.concatenate(vals, axis=0)
        return carry

    lax.fori_loop(0, 2 * PEER_HEADS, stage1, 0)

    row = lax.broadcasted_iota(jnp.int32, (SUBLANES, LANES), 0)

    def stage2(hh, carry):
        for c in range(n_chunks):
            lanes = slice(c * LANES, (c + 1) * LANES)
            v1 = vals_ref[2 * hh, :, lanes]
            v2 = vals_ref[2 * hh + 1, :, lanes]
            cands = []
            for segments in _CAND_TILES:
                tile = jnp.full((SUBLANES, LANES), NEG, F32)
                for row0, r1, r2_0, length in segments:
                    blk, off = r2_0 // SUBLANES, r2_0 % SUBLANES
                    src = v2[blk * SUBLANES:(blk + 1) * SUBLANES]
                    shift = (row0 - off) % SUBLANES
                    if shift:
                        src = pltpu.roll(src, shift, axis=0)
                    tile = jnp.where((row >= row0) & (row < row0 + length), src + v1[r1:r1 + 1], tile)
                cands.append(tile)
            t0 = v1[0:1] + v2[0:1]
            taken = jnp.zeros((1, LANES), F32)
            z = jnp.zeros((1, LANES), F32)
            t_k = jnp.zeros((1, LANES), F32)
            t_k1 = jnp.zeros((1, LANES), F32)
            neg = jnp.full((SUBLANES, LANES), NEG, F32)
            levels = cands + [neg] * (SUBLANES - len(cands))
            for a, b in _sorting_network(len(levels)):
                levels[a], levels[b] = jnp.maximum(levels[a], levels[b]), jnp.minimum(levels[a], levels[b])
            for r in range(N_RANK):
                m = jnp.max(levels[0], axis=0, keepdims=True)
                pop = levels[0] >= m
                cnt = jnp.sum(jnp.where(pop, 1.0, 0.0), axis=0, keepdims=True)
                after = taken + cnt
                z = z + jnp.clip(PEER_TOPK - taken, 0.0, cnt) * jnp.exp(m - t0)
                t_k = jnp.where((taken < PEER_TOPK) & (after >= PEER_TOPK), m, t_k)
                t_k1 = jnp.where((taken < N_RANK) & (after >= N_RANK), m, t_k1)
                taken = after
                for t in range(min(len(levels), N_RANK - r)):
                    below = levels[t + 1] if t + 1 < len(levels) else neg
                    levels[t] = jnp.where(pop, below, levels[t])
            tau = 0.5 * (t_k + t_k1)
            aux = jnp.concatenate([tau, v1[0:1], v2[0:1], 1.0 / z, jnp.zeros((4, LANES), F32)], axis=0)
            aux_ref[hh, :, lanes] = aux
        return carry

    lax.fori_loop(0, PEER_HEADS, stage2, 0)


def peer_scores(h, wq, keys, tm):
    tn, d = h.shape
    nk = 2 * PEER_HEADS
    return pl.pallas_call(
        _peer_score_kernel,
        grid=(tn // tm,),
        in_specs=[
            pl.BlockSpec((tm, d), lambda i: (i, 0)),
            pl.BlockSpec(wq.shape, lambda i: (0, 0)),
            pl.BlockSpec(keys.shape, lambda i: (0, 0, 0)),
        ],
        out_specs=[
            pl.BlockSpec((nk, PEER_N_KEYS, tm), lambda i: (0, 0, i)),
            pl.BlockSpec((PEER_HEADS, SUBLANES, tm), lambda i: (0, 0, i)),
        ],
        out_shape=[jax.ShapeDtypeStruct((nk, PEER_N_KEYS, tn), F32),
                   jax.ShapeDtypeStruct((PEER_HEADS, SUBLANES, tn), F32)],
        scratch_shapes=[pltpu.VMEM((nk, VAL_ROWS, tm), F32)],
        compiler_params=_cparams(("parallel",)),
        name="peer_scores",
    )(h, wq, keys)


PEER_TE = 1024
PEER_G2_ROWS = 256


def _gelu_x2(a):
    return a * (1.0 + lax.erf(a * (1.0 / math.sqrt(2.0))))


def _peer_expert_kernel(h_ref, s_ref, aux_ref, u_ref, vt_ref, x_ref, g2_ref, fn_ref, o_ref,
                        acc_ref, e2_ref, act_ref, wt_ref, *, final):
    e = pl.program_id(1)
    tm = h_ref.shape[0]
    n_chunks = tm // LANES
    n_i = PEER_TE // PEER_N_KEYS

    @pl.when(e == 0)
    def _():
        acc_ref[...] = jnp.zeros_like(acc_ref)
        for hh in range(PEER_HEADS):
            e2_ref[hh] = jnp.exp(s_ref[2 * hh + 1] - aux_ref[hh, 2:3, :]) * (0.5 * aux_ref[hh, 3:4, :])

    act_ref[...] = lax.dot_general(u_ref[...], h_ref[...], NT_DIMS, preferred_element_type=F32)

    i0 = pl.multiple_of(e * n_i, n_i)
    thr, e1 = [], []
    for hh in range(PEER_HEADS):
        s1 = s_ref[2 * hh, pl.ds(i0, n_i), :]
        thr.append(aux_ref[hh, 0:1, :] - s1)
        e1.append(jnp.exp(s1 - aux_ref[hh, 1:2, :]))
    for il in range(n_i):
        rows = slice(il * PEER_N_KEYS, (il + 1) * PEER_N_KEYS)
        for c in range(n_chunks):
            lanes = slice(c * LANES, (c + 1) * LANES)
            g = jnp.zeros((PEER_N_KEYS, LANES), F32)
            for hh in range(PEER_HEADS):
                sel = jnp.where(s_ref[2 * hh + 1, :, lanes] >= thr[hh][il:il + 1, lanes], e2_ref[hh, :, lanes], 0.0)
                g = g + sel * e1[hh][il:il + 1, lanes]
            wt_ref[rows, lanes] = (g * _gelu_x2(act_ref[rows, lanes])).astype(BF16)

    acc_ref[...] += jnp.dot(vt_ref[...], wt_ref[...], preferred_element_type=F32)

    @pl.when(e == pl.num_programs(1) - 1)
    def _():
        y = jnp.transpose(acc_ref[...])
        for r in range(tm // PEER_G2_ROWS):
            rows = slice(r * PEER_G2_ROWS, (r + 1) * PEER_G2_ROWS)
            out = x_ref[rows, :] + g2_ref[r] * y[rows, :]
            if final:
                out = _rms(out) * fn_ref[...]
            o_ref[rows, :] = out


def peer_experts(h, s, aux, u, vt, x, g2_rows, final_norm, tm, final):
    tn, d = h.shape
    ne = u.shape[0]
    nk = 2 * PEER_HEADS
    once = pl.Buffered(1)
    return pl.pallas_call(
        functools.partial(_peer_expert_kernel, final=final),
        grid=(tn // tm, ne // PEER_TE),
        in_specs=[
            pl.BlockSpec((tm, d), lambda i, e: (i, 0), pipeline_mode=once),
            pl.BlockSpec((nk, PEER_N_KEYS, tm), lambda i, e: (0, 0, i), pipeline_mode=once),
            pl.BlockSpec((PEER_HEADS, SUBLANES, tm), lambda i, e: (0, 0, i)),
            pl.BlockSpec((PEER_TE, d), lambda i, e: (e, 0)),
            pl.BlockSpec((d, PEER_TE), lambda i, e: (0, e)),
            pl.BlockSpec((tm, d), lambda i, e: (i, 0), pipeline_mode=once),
            pl.BlockSpec((tm // PEER_G2_ROWS, 1, d), lambda i, e: (i, 0, 0)),
            pl.BlockSpec((1, d), lambda i, e: (0, 0)),
        ],
        out_specs=pl.BlockSpec((tm, d), lambda i, e: (i, 0)),
        out_shape=jax.ShapeDtypeStruct((tn, d), F32),
        scratch_shapes=[
            pltpu.VMEM((d, tm), F32),
            pltpu.VMEM((PEER_HEADS, PEER_N_KEYS, tm), F32),
            pltpu.VMEM((PEER_TE, tm), F32),
            pltpu.VMEM((PEER_TE, tm), BF16),
        ],
        compiler_params=_cparams(("parallel", "arbitrary")),
        name="peer_experts",
    )(h, s, aux, u, vt, x, g2_rows, final_norm.reshape(1, d).astype(F32))


IN_LAYOUT = (
    ("dn_qkv", 1536), ("dn_gate", 512), ("dn_a", 8), ("dn_b", 8), ("gla_q", 256), ("gla_k", 256),
    ("gla_v", 512), ("gla_gate", 512), ("gla_alpha", 32), ("mla_cq", 448), ("mla_ckv", 128),
    ("mla_kr", 64), ("fn", 512),
)


def _in_cols():
    offs, o = {}, 0
    for name, w in IN_LAYOUT:
        offs[name] = (o, o + w)
        o += w
    return offs


def _rot_half_cols(w):
    return jnp.concatenate([-w[:, 16:32], w[:, 0:16], -w[:, 48:64], w[:, 32:48]], axis=1)


def prep_w_in(w_in):
    c = _in_cols()
    d = w_in.shape[0]

    def cols(name):
        a, b = c[name]
        return w_in[:, a:b]

    z = lambda n: jnp.zeros((d, n), w_in.dtype)
    kr = cols("mla_kr")
    pieces = [
        cols("dn_qkv"), cols("dn_gate"), cols("gla_v"), cols("gla_gate"), cols("fn"),
        cols("mla_cq"), z(512 - MLA_Q_RANK), cols("gla_q"), cols("gla_k"), cols("mla_ckv"),
        kr, _rot_half_cols(kr), cols("dn_a"), cols("dn_b"), cols("gla_alpha"), z(128 - 48),
    ]
    w = jnp.concatenate(pieces, axis=1)
    assert w.shape[1] == NP
    return w.astype(BF16)


MLA_SLOT = 256


def prep_mla_weights(q_norm, w_uq, kv_norm, w_ukv):
    scale = (MLA_NOPE + MLA_ROPE) ** -0.5 * math.log2(math.e)
    wq = w_uq * q_norm[:, None] * scale
    per = MLA_NOPE + MLA_ROPE
    cols = []
    for h in range(MLA_HEADS):
        nope = wq[:, h * per:h * per + MLA_NOPE]
        rope = wq[:, h * per + MLA_NOPE:(h + 1) * per]
        cols += [nope, rope, _rot_half_cols(rope)]
    wq_p = jnp.concatenate(cols, axis=1)
    wq_p = jnp.concatenate([wq_p, jnp.zeros((512 - MLA_Q_RANK, wq_p.shape[1]), wq_p.dtype)], axis=0)
    wkv = w_ukv * kv_norm[:, None]
    per = MLA_NOPE + MLA_V
    wkv_p = jnp.concatenate([wkv[:, h * per:h * per + MLA_NOPE] for h in range(MLA_HEADS)]
                            + [wkv[:, h * per + MLA_NOPE:(h + 1) * per] for h in range(MLA_HEADS)], axis=1)
    return wq_p.astype(BF16), wkv_p.astype(BF16)


def rope_table(l, lc):
    rows = l // GRID_W
    row = np.repeat(np.arange(rows), GRID_W).astype(np.float32)
    col = np.tile(np.arange(GRID_W), rows).astype(np.float32)
    half = MLA_ROPE // 2
    inv_freq = jnp.asarray(ROPE_BASE, F32) ** (-jnp.arange(0, half, 2, dtype=F32) / half)
    ang_r = jnp.asarray(row)[:, None] * inv_freq
    ang_c = jnp.asarray(col)[:, None] * inv_freq
    cos = jnp.concatenate([jnp.cos(ang_r)] * 2 + [jnp.cos(ang_c)] * 2, axis=1)
    sin = jnp.concatenate([jnp.sin(ang_r)] * 2 + [jnp.sin(ang_c)] * 2, axis=1)
    lat = jnp.concatenate([cos, sin], axis=1)
    ctx = jnp.concatenate([jnp.ones((lc, MLA_ROPE), F32), jnp.zeros((lc, MLA_ROPE), F32)], axis=1)
    return jnp.concatenate([lat, ctx], axis=0)


def _mla_prep_kernel(cq_ref, ckv_ref, kr_ref, cs_ref, wq_ref, wkv_ref, q_ref, k_ref, v_ref):
    cs = cs_ref[...]
    half = LANES // 2
    lane = lax.broadcasted_iota(jnp.int32, cs.shape, 1)

    def rotary(u):
        return u + pltpu.roll(u, half, axis=1)

    cq = cq_ref[0].astype(F32)
    cqn = cq * lax.rsqrt(jnp.sum(cq * cq, axis=-1, keepdims=True) * (1.0 / MLA_Q_RANK) + EPS)
    q = jnp.dot(cqn.astype(BF16), wq_ref[...], preferred_element_type=F32)
    qs = []
    for h in range(MLA_HEADS):
        qs.append(q[:, h * MLA_SLOT:h * MLA_SLOT + LANES])
        qs.append(rotary(q[:, h * MLA_SLOT + LANES:(h + 1) * MLA_SLOT] * cs))
    q_ref[0] = jnp.concatenate(qs, axis=-1).astype(BF16)

    kv = jnp.dot(_rms(ckv_ref[0].astype(F32)).astype(BF16), wkv_ref[...], preferred_element_type=F32)
    kr = jnp.where(lane < half, rotary(kr_ref[0].astype(F32) * cs), 0.0)
    ks = []
    for h in range(MLA_HEADS):
        ks += [kv[:, h * LANES:(h + 1) * LANES], kr]
    k_ref[0] = jnp.concatenate(ks, axis=-1).astype(BF16)
    v_ref[0] = kv[:, MLA_HEADS * LANES:].astype(BF16)


def mla_prep(proj, cs, wq_p, wkv_p, tm):
    b, lt, _ = proj.shape
    nq = MLA_HEADS * MLA_SLOT
    return pl.pallas_call(
        _mla_prep_kernel,
        grid=(b, lt // tm),
        in_specs=[
            pl.BlockSpec((1, tm, 512), lambda bi, i: (bi, i, C_CQ // 512)),
            pl.BlockSpec((1, tm, LANES), lambda bi, i: (bi, i, C_CKV // LANES)),
            pl.BlockSpec((1, tm, LANES), lambda bi, i: (bi, i, C_KR // LANES)),
            pl.BlockSpec((tm, LANES), lambda bi, i: (i, 0)),
            pl.BlockSpec(wq_p.shape, lambda bi, i: (0, 0)),
            pl.BlockSpec(wkv_p.shape, lambda bi, i: (0, 0)),
        ],
        out_specs=[
            pl.BlockSpec((1, tm, nq), lambda bi, i: (bi, i, 0)),
            pl.BlockSpec((1, tm, nq), lambda bi, i: (bi, i, 0)),
            pl.BlockSpec((1, tm, MLA_HEADS * MLA_V), lambda bi, i: (bi, i, 0)),
        ],
        out_shape=[jax.ShapeDtypeStruct((b, lt, nq), BF16), jax.ShapeDtypeStruct((b, lt, nq), BF16),
                   jax.ShapeDtypeStruct((b, lt, MLA_HEADS * MLA_V), BF16)],
        compiler_params=_cparams(("parallel", "parallel")),
        name="mla_prep",
    )(proj, proj, proj, cs, wq_p, wkv_p)


def _mla_attn_kernel(q_ref, k_ref, v_ref, *rest):
    o_ref = rest[-1]
    s = lax.dot_general(q_ref[0], k_ref[0], NT_DIMS, preferred_element_type=F32)
    p = jnp.exp2(s - jnp.max(s, axis=-1, keepdims=True))
    o = jnp.dot(p.astype(BF16), v_ref[0], preferred_element_type=F32)
    o_ref[0] = (o / jnp.sum(p, axis=-1, keepdims=True)).astype(o_ref.dtype)


def mla_attend(q4, k4, v4, q_off, n_tiles, tq, k_rows, k_blk, prev):
    b, lt, _ = q4.shape
    in_specs = [
        pl.BlockSpec((1, tq, MLA_SLOT), lambda bi, h, i: (bi, i + q_off, h)),
        pl.BlockSpec((1, k_rows, MLA_SLOT), lambda bi, h, i: (bi, k_blk, h)),
        pl.BlockSpec((1, k_rows, MLA_V), lambda bi, h, i: (bi, k_blk, h)),
    ]
    args = [q4, k4, v4]
    aliases = {}
    if prev is not None:
        in_specs.append(pl.BlockSpec(memory_space=pl.ANY))
        args.append(prev)
        aliases = {3: 0}
    return pl.pallas_call(
        _mla_attn_kernel,
        grid=(b, MLA_HEADS, n_tiles),
        in_specs=in_specs,
        out_specs=pl.BlockSpec((1, tq, MLA_V), lambda bi, h, i: (bi, i + q_off, h)),
        out_shape=jax.ShapeDtypeStruct((b, lt, MLA_HEADS * MLA_V), BF16),
        input_output_aliases=aliases,
        compiler_params=_cparams(("parallel", "parallel", "parallel")),
        name="mla_attend",
    )(*args)


def dft_tables(n, scale):
    j = np.arange(n, dtype=np.int64)
    root = int(round(math.sqrt(n))) if int(round(math.sqrt(n))) ** 2 == n else 1
    if root == 1:
        ph = 2.0 * np.pi * ((j[:, None] * j[None, :]) % n) / n
        return jnp.asarray(np.cos(ph) * scale, BF16), jnp.asarray(np.sin(ph) * scale, BF16)
    u = np.arange(root, dtype=np.int64)
    pa = 2.0 * np.pi * ((j[:, None] * (root * u)[None, :]) % n) / n
    pb = 2.0 * np.pi * ((j[:, None] * u[None, :]) % n) / n
    ca, sa, cb, sb = (jnp.asarray(t, F32) for t in (np.cos(pa), np.sin(pa), np.cos(pb) * scale, np.sin(pb) * scale))
    cos = ca[:, :, None] * cb[:, None, :] - sa[:, :, None] * sb[:, None, :]
    sin = sa[:, :, None] * cb[:, None, :] + ca[:, :, None] * sb[:, None, :]
    return cos.reshape(n, n).astype(BF16), sin.reshape(n, n).astype(BF16)


def channel_dft_tables():
    c, s = dft_tables(FN_CH, 1.0)
    eye = jnp.eye(FN_GROUPS, dtype=F32)
    return (jnp.kron(eye, c.astype(F32)).astype(BF16), jnp.kron(eye, s.astype(F32)).astype(BF16))


def _fourier_kernel(z_ref, cl_ref, sl_ref, cc_ref, sc_ref, *rest):
    o_ref = rest[-1]
    z = z_ref[0]
    re = jnp.dot(cl_ref[...], z, preferred_element_type=F32).astype(BF16)
    im = jnp.dot(sl_ref[...], z, preferred_element_type=F32).astype(BF16)
    y = jnp.dot(re, cc_ref[...], preferred_element_type=F32) - jnp.dot(im, sc_ref[...], preferred_element_type=F32)
    o_ref[0] = y.astype(o_ref.dtype)


def fourier_mix(proj, seq_blk, n, cl, sl, cc, sc, prev):
    b, lt, _ = proj.shape
    w = FN_GROUPS * FN_CH
    tm = min(512, n)
    nt = n // tm
    in_specs = [
        pl.BlockSpec((1, n, w), lambda i, bi: (bi, seq_blk, C_FN // w)),
        pl.BlockSpec((tm, n), lambda i, bi: (i, 0)),
        pl.BlockSpec((tm, n), lambda i, bi: (i, 0)),
        pl.BlockSpec((w, w), lambda i, bi: (0, 0)),
        pl.BlockSpec((w, w), lambda i, bi: (0, 0)),
    ]
    args = [proj, cl, sl, cc, sc]
    aliases = {}
    if prev is not None:
        in_specs.append(pl.BlockSpec(memory_space=pl.ANY))
        args.append(prev)
        aliases = {5: 0}
    return pl.pallas_call(
        _fourier_kernel,
        grid=(nt, b),
        in_specs=in_specs,
        out_specs=pl.BlockSpec((1, tm, w), lambda i, bi: (bi, seq_blk * nt + i, 0)),
        out_shape=jax.ShapeDtypeStruct((b, lt, w), BF16),
        input_output_aliases=aliases,
        compiler_params=_cparams(("parallel", "parallel")),
        name="fourier_mix",
    )(*args)


CHUNK = 64
BMM_DIMS = (((2,), (1,)), ((0,), (0,)))
BMM_NT_DIMS = (((2,), (2,)), ((0,), (0,)))


STATE_GROUP = 4


def _scan_maps(l, lc):
    span = STATE_GROUP * CHUNK
    assert l % span == 0 and lc % span == 0
    n_lat, n_ctx = l // span, lc // span

    def fwd(n):
        return jnp.where(n < n_ctx, n + n_lat, n - n_ctx)

    def rev(n):
        return n_lat + n_ctx - 1 - n

    return n_lat + n_ctx, fwd, rev


def _bmm(a, b, dims=BMM_DIMS):
    return lax.dot_general(a, b, dims, preferred_element_type=F32)


def _bmm_exact_lhs(m, x):
    mb = m.astype(BF16)
    hi = x.astype(BF16)
    r1 = x - hi.astype(F32)
    mid = r1.astype(BF16)
    lo = (r1 - mid.astype(F32)).astype(BF16)
    return _bmm(mb, hi) + _bmm(mb, mid) + _bmm(mb, lo)


GLA_PREP_CHUNKS = 4
GLA_NQ = GLA_HEADS * GLA_DK
GLA_NV = GLA_HEADS * GLA_DV


def _gla_prep_kernel(q_ref, k_ref, v_ref, a_ref, w2_ref, b2_ref, m_ref, *out_refs):
    nb = 2 * GLA_PREP_CHUNKS
    inc = m_ref[...]
    lane_q = lax.broadcasted_iota(jnp.int32, (nb, CHUNK, GLA_NQ), 2)
    a, q, k, v = [], [], [], []
    for j in range(GLA_PREP_CHUNKS):
        rows = slice(j * CHUNK, (j + 1) * CHUNK)
        for d in range(2):
            a.append(a_ref[0, rows, :].astype(F32))
            q.append(q_ref[0, rows, :].astype(F32) * (GLA_DK ** -0.5))
            k.append(k_ref[0, rows, :].astype(F32))
            v.append(v_ref[0, rows, :])
    a, q, k, v = jnp.stack(a), jnp.stack(q), jnp.stack(k), jnp.stack(v)
    w2 = jnp.stack([w2_ref[it % 2] for it in range(nb)])
    b2 = jnp.stack([jnp.broadcast_to(b2_ref[it % 2], (CHUNK, GLA_NQ)) for it in range(nb)])
    z = lax.dot_general(a, w2, BMM_DIMS, preferred_element_type=F32, precision=lax.Precision.HIGHEST) + b2
    log_a = jax.nn.log_sigmoid(z) * (1.0 / GLA_TAU)
    bcum = _bmm_exact_lhs(inc, log_a)
    b_rows = [bcum[it, (0 if it % 2 else CHUNK - 1):(1 if it % 2 else CHUNK), :] for it in range(nb)]
    b_last = jnp.stack([jnp.broadcast_to(row, (CHUNK, GLA_NQ)) for row in b_rows])
    q_in = (q * jnp.exp(bcum)).astype(BF16)
    q_rel = q * jnp.exp(bcum - b_last)
    k_rel = (k * jnp.exp(b_last - bcum)).astype(BF16)
    inc_f = inc.astype(F32)
    outs = []
    for h in range(GLA_HEADS):
        qh = jnp.where(lane_q // GLA_DK == h, q_rel, 0.0).astype(BF16)
        att = (_bmm(qh, k_rel, BMM_NT_DIMS) * inc_f).astype(BF16)
        outs.append(_bmm(att, v[:, :, h * GLA_DV:(h + 1) * GLA_DV]))
    o_intra = jnp.concatenate(outs, axis=-1).astype(BF16)
    for j in range(GLA_PREP_CHUNKS):
        rows = slice(j * CHUNK, (j + 1) * CHUNK)
        for d in range(2):
            it = 2 * j + d
            o_ref, qi_ref, kr_ref, e_ref = out_refs[4 * d:4 * d + 4]
            o_ref[0, rows, :] = o_intra[it]
            qi_ref[0, rows, :] = q_in[it]
            kr_ref[0, rows, :] = k_rel[it]
            e_ref[0, j] = jnp.exp(jnp.broadcast_to(b_rows[it], (SUBLANES, GLA_NQ)))


def _gla_state_kernel(*refs):
    (oif, qif, krf, ef, vf, oir, qir, krr, er, vr, of_ref, or_ref, st_ref) = refs

    @pl.when(pl.program_id(1) == 0)
    def _():
        st_ref[...] = jnp.zeros_like(st_ref)

    bd = (lax.broadcasted_iota(jnp.int32, (GLA_NV, GLA_NQ), 0) // GLA_DV) == (
        lax.broadcasted_iota(jnp.int32, (GLA_NV, GLA_NQ), 1) // GLA_DK)
    groups = ((oif, qif, krf, ef, vf, of_ref), (oir, qir, krr, er, vr, or_ref))
    for jj in range(STATE_GROUP):
        for d, (oi_ref, qi_ref, kr_ref, e_ref, v_ref, o_ref) in enumerate(groups):
            j = STATE_GROUP - 1 - jj if d == 1 else jj
            rows = slice(j * CHUNK, (j + 1) * CHUNK)
            st = st_ref[d]
            o_inter = lax.dot_general(qi_ref[0, rows, :], st.astype(BF16), NT_DIMS, preferred_element_type=F32)
            o_ref[0, rows, :] = (oi_ref[0, rows, :].astype(F32) + o_inter).astype(o_ref.dtype)
            upd = lax.dot_general(v_ref[0, rows, :], kr_ref[0, rows, :], TN_DIMS, preferred_element_type=F32)
            st_ref[d] = st * e_ref[0, j, 0:1, :] + jnp.where(bd, upd, 0.0)


def gla_scan(proj, w2, b2, l, lc):
    b, lt, _ = proj.shape
    n_steps, fwd, rev = _scan_maps(l, lc)
    nq, nv = GLA_NQ, GLA_NV
    n_chunks = lt // CHUNK
    rows = GLA_PREP_CHUNKS * CHUNK
    w2p = jnp.zeros((2, LANES, nq), F32)
    for d in range(2):
        lo = 2 * DN_HEADS * 2 + GLA_RANK * d
        w2p = w2p.at[d, lo:lo + GLA_RANK].set(w2[d].astype(F32))
    r, c = np.meshgrid(np.arange(CHUNK), np.arange(CHUNK), indexing="ij")
    inc = jnp.asarray(np.stack([(c >= r) if it % 2 else (c <= r) for it in range(2 * GLA_PREP_CHUNKS)])
                      .astype(np.float32), BF16)

    wide = jax.ShapeDtypeStruct((b, lt, nv), BF16)
    mid = jax.ShapeDtypeStruct((b, lt, nq), BF16)
    small = jax.ShapeDtypeStruct((b, n_chunks, SUBLANES, nq), F32)
    wide_spec = pl.BlockSpec((1, rows, nv), lambda bi, i: (bi, i, 0))
    mid_spec = pl.BlockSpec((1, rows, nq), lambda bi, i: (bi, i, 0))
    small_spec = pl.BlockSpec((1, GLA_PREP_CHUNKS, SUBLANES, nq), lambda bi, i: (bi, i, 0, 0))
    terms = pl.pallas_call(
        _gla_prep_kernel,
        grid=(b, n_chunks // GLA_PREP_CHUNKS),
        in_specs=[
            pl.BlockSpec((1, rows, nq), lambda bi, i: (bi, i, C_GLAQ // nq)),
            pl.BlockSpec((1, rows, nq), lambda bi, i: (bi, i, C_GLAK // nq)),
            pl.BlockSpec((1, rows, nv), lambda bi, i: (bi, i, C_GLAV // nv)),
            pl.BlockSpec((1, rows, LANES), lambda bi, i: (bi, i, C_SMALL // LANES)),
            pl.BlockSpec((2, LANES, nq), lambda bi, i: (0, 0, 0)),
            pl.BlockSpec((2, 1, nq), lambda bi, i: (0, 0, 0)),
            pl.BlockSpec(inc.shape, lambda bi, i: (0, 0, 0)),
        ],
        out_specs=[wide_spec, mid_spec, mid_spec, small_spec] * 2,
        out_shape=[wide, mid, mid, small] * 2,
        compiler_params=_cparams(("parallel", "parallel")),
        name="gla_prep",
    )(proj, proj, proj, proj, w2p, b2.astype(F32).reshape(2, 1, nq), inc)

    span = STATE_GROUP * CHUNK

    def specs(cmap):
        return [pl.BlockSpec((1, span, nv), lambda bi, n: (bi, cmap(n), 0)),
                pl.BlockSpec((1, span, nq), lambda bi, n: (bi, cmap(n), 0)),
                pl.BlockSpec((1, span, nq), lambda bi, n: (bi, cmap(n), 0)),
                pl.BlockSpec((1, STATE_GROUP, SUBLANES, nq), lambda bi, n: (bi, cmap(n), 0, 0)),
                pl.BlockSpec((1, span, nv), lambda bi, n: (bi, cmap(n), C_GLAV // nv))]

    return pl.pallas_call(
        _gla_state_kernel,
        grid=(b, n_steps),
        in_specs=specs(fwd) + specs(rev),
        out_specs=[pl.BlockSpec((1, span, nv), lambda bi, n: (bi, fwd(n), 0)),
                   pl.BlockSpec((1, span, nv), lambda bi, n: (bi, rev(n), 0))],
        out_shape=[wide, wide],
        scratch_shapes=[pltpu.VMEM((2, nv, nq), F32)],
        compiler_params=_cparams(("parallel", "arbitrary")),
        name="gla_state",
    )(*terms[0:4], proj, *terms[4:8], proj)


DN_TL = 256
DN_W = DN_HEADS * DN_DK


def _dn_conv_kernel(x_ref, p_ref, n_ref, w_ref, o_ref, *, first_tiles, last_tiles):
    i = pl.program_id(1)
    tl = x_ref.shape[1]
    is_first = functools.reduce(jnp.logical_or, [i == t for t in first_tiles])
    is_last = functools.reduce(jnp.logical_or, [i == t for t in last_tiles])
    prev = jnp.where(is_first, 0.0, p_ref[0].astype(F32))
    nxt = jnp.where(is_last, 0.0, n_ref[0].astype(F32))
    ext = jnp.concatenate([prev, x_ref[0].astype(F32), nxt], axis=0)
    pad = DN_CONV // 2
    acc = jnp.zeros((tl, ext.shape[1]), F32)
    for t in range(DN_CONV):
        start = SUBLANES - pad + t
        acc = acc + ext[start:start + tl, :] * w_ref[t:t + 1, :]
    y = acc * jax.nn.sigmoid(acc)
    outs = []
    for h in range(3 * DN_HEADS):
        yh = y[:, h * DN_DK:(h + 1) * DN_DK]
        if h < 2 * DN_HEADS:
            yh = yh * lax.rsqrt(jnp.sum(yh * yh, axis=-1, keepdims=True) + EPS)
            if h < DN_HEADS:
                yh = yh * (DN_DK ** -0.5)
        outs.append(yh)
    o_ref[0] = jnp.concatenate(outs, axis=-1).astype(o_ref.dtype)


def dn_conv_norm(proj, conv_w, l, lc):
    b, lt, _ = proj.shape
    w = 3 * DN_W
    tl = DN_TL
    hb = tl // SUBLANES
    n_halo = lt // SUBLANES
    first_tiles = (0, l // tl)
    last_tiles = (l // tl - 1, lt // tl - 1)
    return pl.pallas_call(
        functools.partial(_dn_conv_kernel, first_tiles=first_tiles, last_tiles=last_tiles),
        grid=(b, lt // tl),
        in_specs=[
            pl.BlockSpec((1, tl, w), lambda bi, i: (bi, i, 0)),
            pl.BlockSpec((1, SUBLANES, w), lambda bi, i: (bi, jnp.maximum(i * hb - 1, 0), 0)),
            pl.BlockSpec((1, SUBLANES, w), lambda bi, i: (bi, jnp.minimum((i + 1) * hb, n_halo - 1), 0)),
            pl.BlockSpec((DN_CONV, w), lambda bi, i: (0, 0)),
        ],
        out_specs=pl.BlockSpec((1, tl, w), lambda bi, i: (bi, i, 0)),
        out_shape=jax.ShapeDtypeStruct((b, lt, w), BF16),
        compiler_params=_cparams(("parallel", "parallel")),
        name="dn_conv",
    )(proj, proj, proj, conv_w.astype(F32))


def _stack_heads(x, width):
    return jnp.concatenate([x[:, h * width:(h + 1) * width] for h in range(DN_HEADS)], axis=0)


DN_PREP_CHUNKS = 2
DN_N = DN_HEADS * CHUNK


def dn_masks():
    n = DN_N
    r, c = np.meshgrid(np.arange(n), np.arange(n), indexing="ij")
    same = (r // CHUNK) == (c // CHUNK)
    kinds = [[r == c] * 2, [same & (c <= r), same & (c >= r)], [same & (c < r), same & (c > r)]]
    size = 1
    while size < CHUNK:
        blk = (r // (2 * size)) == (c // (2 * size))
        lo_r, lo_c = (r % (2 * size)) < size, (c % (2 * size)) < size
        kinds.append([blk & ~lo_r & lo_c, blk & lo_r & ~lo_c])
        size *= 2
    m = np.stack([np.stack([k[it % 2] for it in range(2 * DN_PREP_CHUNKS)]) for k in kinds])
    return jnp.asarray(m.astype(np.float32), BF16)


def _dn_prep_kernel(x_ref, a_ref, gp_ref, m_ref, *out_refs):
    n = DN_N
    nb = 2 * DN_PREP_CHUNKS
    shape = (nb, n, n)
    eye, inc, strict = m_ref[0], m_ref[1], m_ref[2]

    def pair_mask(size):
        return m_ref[3 + int(math.log2(size))]

    qs, ks, vs, g_col, b_col = [], [], [], [], []
    for j in range(DN_PREP_CHUNKS):
        rows = slice(j * CHUNK, (j + 1) * CHUNK)
        x = x_ref[0, rows, :].astype(F32)
        a = a_ref[0, rows, :].astype(F32)
        g8 = gp_ref[0:1, :] * jax.nn.softplus(a + gp_ref[1:2, :])
        beta8 = jax.nn.sigmoid(a)
        for d in range(2):
            qs.append(_stack_heads(x[:, 0:DN_W], DN_DK))
            ks.append(_stack_heads(x[:, DN_W:2 * DN_W], DN_DK))
            vs.append(_stack_heads(x[:, 2 * DN_W:3 * DN_W], DN_DV))
            lo = DN_HEADS * d
            g_col.append(jnp.concatenate([g8[:, lo + h:lo + h + 1] for h in range(DN_HEADS)], axis=0))
            lo = 2 * DN_HEADS + DN_HEADS * d
            b_col.append(jnp.concatenate([beta8[:, lo + h:lo + h + 1] for h in range(DN_HEADS)], axis=0))
    qs, ks, vs = jnp.stack(qs), jnp.stack(ks), jnp.stack(vs)
    g_b = jnp.broadcast_to(jnp.stack(g_col), shape)
    beta_b = jnp.broadcast_to(jnp.stack(b_col), shape)
    diff = _bmm_exact_lhs(inc, g_b * strict.astype(F32))
    decay = jnp.exp(diff) * inc.astype(F32)
    gc = []
    for it in range(nb):
        first = CHUNK - 1 if it % 2 else 0
        blocks = []
        for h in range(DN_HEADS):
            col = h * CHUNK + first
            cum = diff[it, h * CHUNK:(h + 1) * CHUNK, col:col + 1] + g_col[it][col:col + 1, :]
            blocks.append(jnp.broadcast_to(cum, (CHUNK, DN_DK)))
        gc.append(jnp.concatenate(blocks, axis=0))
    gc = jnp.stack(gc)
    kb = ks.astype(BF16)
    kk = _bmm(kb, kb, BMM_NT_DIMS)
    a_mat = beta_b * kk * decay * strict.astype(F32)
    a_b = a_mat.astype(BF16)
    inv = eye.astype(F32) - a_mat * pair_mask(1).astype(F32)
    size = 2
    while size < CHUNK:
        t = _bmm(inv.astype(BF16), a_b * pair_mask(size))
        inv = inv - _bmm(t.astype(BF16), inv.astype(BF16))
        size *= 2
    beta_k = beta_b[:, :, :DN_DK]
    e_gc = jnp.exp(gc)
    invb = inv.astype(BF16)
    u = _bmm(invb, (vs * beta_k).astype(BF16))
    w = _bmm(invb, (ks * beta_k * e_gc).astype(BF16))
    p = (_bmm(qs.astype(BF16), kb, BMM_NT_DIMS) * decay).astype(BF16)
    q_dec = qs * e_gc
    for j in range(DN_PREP_CHUNKS):
        for d in range(2):
            it = 2 * j + d
            last = 0 if d == 1 else CHUNK - 1
            g_end = [gc[it, h * CHUNK + last:h * CHUNK + last + 1, :] for h in range(DN_HEADS)]
            gl = jnp.concatenate([jnp.broadcast_to(g, (CHUNK, DN_DK)) for g in g_end], axis=0)
            k_dec = ks[it] * jnp.exp(gl - gc[it])
            uw_ref, qk_ref, p_ref, e_ref = out_refs[4 * d:4 * d + 4]
            uw_ref[0, j] = jnp.concatenate([u[it], w[it]], axis=-1).astype(BF16)
            qk_ref[0, j] = jnp.concatenate([q_dec[it], k_dec], axis=-1).astype(BF16)
            p_ref[0, j] = p[it]
            e_ref[0, j] = jnp.exp(jnp.concatenate(g_end + [jnp.zeros((SUBLANES - DN_HEADS, DN_DK), F32)], axis=0))


def dn_prep(qkv, proj, a_log, dt_bias):
    b, lt, _ = qkv.shape
    w = 3 * DN_W
    n_chunks = lt // CHUNK
    gp = jnp.zeros((SUBLANES, LANES), F32)
    gp = gp.at[0, :2 * DN_HEADS].set(-jnp.exp(a_log.astype(F32)).reshape(-1))
    gp = gp.at[1, :2 * DN_HEADS].set(dt_bias.astype(F32).reshape(-1))
    rows = DN_PREP_CHUNKS * CHUNK
    big = jax.ShapeDtypeStruct((b, n_chunks, DN_N, 2 * DN_DK), BF16)
    small = jax.ShapeDtypeStruct((b, n_chunks, SUBLANES, DN_DK), F32)
    big_spec = pl.BlockSpec((1, DN_PREP_CHUNKS, DN_N, 2 * DN_DK), lambda bi, i: (bi, i, 0, 0))
    small_spec = pl.BlockSpec((1, DN_PREP_CHUNKS, SUBLANES, DN_DK), lambda bi, i: (bi, i, 0, 0))
    masks = dn_masks()
    return pl.pallas_call(
        _dn_prep_kernel,
        grid=(b, n_chunks // DN_PREP_CHUNKS),
        in_specs=[pl.BlockSpec((1, rows, w), lambda bi, i: (bi, i, 0)),
                  pl.BlockSpec((1, rows, LANES), lambda bi, i: (bi, i, C_SMALL // LANES)),
                  pl.BlockSpec((SUBLANES, LANES), lambda bi, i: (0, 0)),
                  pl.BlockSpec(masks.shape, lambda bi, i: (0, 0, 0, 0))],
        out_specs=[big_spec, big_spec, big_spec, small_spec] * 2,
        out_shape=[big, big, big, small] * 2,
        compiler_params=_cparams(("parallel", "parallel")),
        name="dn_prep",
    )(qkv, proj, gp, masks)


def _dn_state_kernel(*refs):
    (uwf, qkf, pf, ef, uwr, qkr, pr, er, of_ref, or_ref, s_ref) = refs

    @pl.when(pl.program_id(1) == 0)
    def _():
        s_ref[...] = jnp.zeros_like(s_ref)

    groups = ((uwf, qkf, pf, ef, of_ref), (uwr, qkr, pr, er, or_ref))
    for jj in range(STATE_GROUP):
        for d, (uw_ref, qk_ref, p_ref, e_ref, o_ref) in enumerate(groups):
            j = STATE_GROUP - 1 - jj if d == 1 else jj
            uw = uw_ref[0, j]
            qk = qk_ref[0, j]
            v_new, o_state = [], []
            for h in range(DN_HEADS):
                rows = slice(h * CHUNK, (h + 1) * CHUNK)
                sb = s_ref[d, h].astype(BF16)
                v_new.append(uw[rows, :DN_DV].astype(F32)
                             - jnp.dot(uw[rows, DN_DV:], sb, preferred_element_type=F32))
                o_state.append(jnp.dot(qk[rows, :DN_DK], sb, preferred_element_type=F32))
            v_new = jnp.concatenate(v_new, axis=0).astype(BF16)
            o_st = jnp.dot(p_ref[0, j], v_new, preferred_element_type=F32)
            outs = []
            for h in range(DN_HEADS):
                rows = slice(h * CHUNK, (h + 1) * CHUNK)
                outs.append(o_st[rows] + o_state[h])
                s_ref[d, h] = s_ref[d, h] * e_ref[0, j, h:h + 1, :] + lax.dot_general(
                    qk[rows, DN_DK:], v_new[rows], TN_DIMS, preferred_element_type=F32)
            o_ref[0, j * CHUNK:(j + 1) * CHUNK, :] = jnp.concatenate(outs, axis=-1).astype(o_ref.dtype)


def dn_scan(qkv, proj, a_log, dt_bias, l, lc):
    b, lt, _ = qkv.shape
    n_steps, fwd, rev = _scan_maps(l, lc)
    terms = dn_prep(qkv, proj, a_log, dt_bias)
    g = STATE_GROUP

    def specs(cmap):
        big = pl.BlockSpec((1, g, DN_N, 2 * DN_DK), lambda bi, n: (bi, cmap(n), 0, 0))
        small = pl.BlockSpec((1, g, SUBLANES, DN_DK), lambda bi, n: (bi, cmap(n), 0, 0))
        return [big, big, big, small]

    out_sd = jax.ShapeDtypeStruct((b, lt, DN_HEADS * DN_DV), BF16)
    return pl.pallas_call(
        _dn_state_kernel,
        grid=(b, n_steps),
        in_specs=specs(fwd) + specs(rev),
        out_specs=[pl.BlockSpec((1, g * CHUNK, DN_HEADS * DN_DV), lambda bi, n: (bi, fwd(n), 0)),
                   pl.BlockSpec((1, g * CHUNK, DN_HEADS * DN_DV), lambda bi, n: (bi, rev(n), 0))],
        out_shape=[out_sd, out_sd],
        scratch_shapes=[pltpu.VMEM((2, DN_HEADS, DN_DK, DN_DV), F32)],
        compiler_params=_cparams(("parallel", "arbitrary")),
        name="dn_scan",
    )(*terms)


CTX_TM = 256


def kernel(x, c, ctx, c_ctx, w_ada, b_ada, w_in, dn_conv, dn_a_log, dn_dt_bias, dn_norm, gla_w2, gla_b2,
           gla_norm, mla_q_norm, mla_w_uq, mla_kv_norm, mla_w_ukv, w_out, peer_wq, peer_keys, peer_u, peer_v,
           final_norm):
    b, l, d = x.shape
    lc = ctx.shape[1]
    lt = l + lc
    depth = w_ada.shape[0]
    lat_tm = min(1024, l)
    mix_tm = min(512, l)
    peer_tm = 512
    assert lc == CTX_TM and l % lat_tm == 0 and l % CTX_TM == 0

    pad = (-(b + 1)) % SUBLANES
    c_all = jnp.concatenate([c, c_ctx[None, :], jnp.zeros((pad, d), F32)], axis=0)
    mod = ada_mod(c_all, w_ada, b_ada)

    cs = rope_table(l, lc)
    cc, sc = channel_dft_tables()
    cl_lat, sl_lat = dft_tables(l, (l * FN_CH) ** -0.5)
    cl_ctx, sl_ctx = dft_tables(lc, (lc * FN_CH) ** -0.5)
    ctx_blk = l // lc

    x_lat, x_lat_off, x_ctx, x_ctx_off = x, 0, ctx, 0
    out = None
    for li in range(depth):
        last = li == depth - 1
        m_lat = [mod[li, :b, k * d:(k + 1) * d].reshape(b, 1, d) for k in range(N_MOD)]
        m_ctx = [mod[li, b:b + 1, k * d:(k + 1) * d].reshape(1, 1, d) for k in range(N_MOD)]

        w_p = prep_w_in(w_in[li])
        proj = proj_in(x_lat, x_lat_off, l // lat_tm, lat_tm, m_lat[0], m_lat[1], w_p, lt, 0, None)
        proj = proj_in(x_ctx, x_ctx_off, 1, CTX_TM, m_ctx[0], m_ctx[1], w_p, lt, l // CTX_TM, proj)

        wq_p, wkv_p = prep_mla_weights(mla_q_norm[li], mla_w_uq[li], mla_kv_norm[li], mla_w_ukv[li])
        q4, k4, v4 = mla_prep(proj, cs, wq_p, wkv_p, CTX_TM)
        y_mla = mla_attend(q4, k4, v4, 0, l // CTX_TM, CTX_TM, lt, 0, None)
        y_fn = fourier_mix(proj, 0, l, cl_lat, sl_lat, cc, sc, None)
        if not last:
            y_mla = mla_attend(q4, k4, v4, l // CTX_TM, lc // CTX_TM, CTX_TM, lc, ctx_blk, y_mla)
            y_fn = fourier_mix(proj, ctx_blk, lc, cl_ctx, sl_ctx, cc, sc, y_fn)
        gla_f, gla_b = gla_scan(proj, gla_w2[li], gla_b2[li], l, lc)
        qkv = dn_conv_norm(proj, dn_conv[li], l, lc)
        dn_f, dn_b = dn_scan(qkv, proj, dn_a_log[li], dn_dt_bias[li], l, lc)

        w_o = w_out[li].astype(BF16)
        rows = l if last else lt
        xn, h = mix_out(dn_f, dn_b, gla_f, gla_b, y_mla, y_fn, proj, 0,
                        x_lat, x_lat_off, l // mix_tm, mix_tm, m_lat[2], m_lat[3], m_lat[4],
                        dn_norm[li], gla_norm[li], w_o, rows, 0, None, None)
        if not last:
            xn, h = mix_out(dn_f, dn_b, gla_f, gla_b, y_mla, y_fn, proj,
                            l // CTX_TM, x_ctx, x_ctx_off, 1, CTX_TM, m_ctx[2], m_ctx[3], m_ctx[4],
                            dn_norm[li], gla_norm[li], w_o, rows, l // CTX_TM, xn, h)

        tn = b * rows
        g2_lat = jnp.broadcast_to(m_lat[5], (b, l // PEER_G2_ROWS, d))
        if last:
            g2_rows = g2_lat.reshape(tn // PEER_G2_ROWS, 1, d)
        else:
            g2_ctx = jnp.broadcast_to(m_ctx[5], (b, lc // PEER_G2_ROWS, d))
            g2_rows = jnp.concatenate([g2_lat, g2_ctx], axis=1).reshape(tn // PEER_G2_ROWS, 1, d)
        hf = h.reshape(tn, d)
        s, aux = peer_scores(hf, peer_wq[li].astype(BF16),
                             peer_keys[li].reshape(2 * PEER_HEADS, PEER_N_KEYS, PEER_HALF).astype(BF16), peer_tm)
        y = peer_experts(hf, s, aux, peer_u[li].astype(BF16), peer_v[li].T.astype(BF16), xn.reshape(tn, d),
                         g2_rows, final_norm, peer_tm, last)
        out = y.reshape(b, rows, d)
        x_lat, x_lat_off, x_ctx, x_ctx_off = out, 0, out, l // CTX_TM
    return out
```

```python
import functools
import math

import numpy as np
import jax
import jax.numpy as jnp
from jax import lax
from jax.experimental import pallas as pl
from jax.experimental.pallas import tpu as pltpu

F32 = jnp.float32
BF16 = jnp.bfloat16

EPS = 1e-6
N_MOD = 6
GRID_W = 64
ROPE_BASE = 10000.0

DN_HEADS, DN_DK, DN_DV, DN_CONV, DN_CHUNK = 4, 128, 128, 5, 64
GLA_HEADS, GLA_DK, GLA_DV, GLA_RANK, GLA_TAU, GLA_CHUNK = 4, 64, 128, 16, 16.0, 64
MLA_HEADS, MLA_NOPE, MLA_ROPE, MLA_V, MLA_Q_RANK, MLA_KV_RANK = 4, 128, 64, 128, 448, 128
FN_GROUPS, FN_CH = 4, 128
PEER_HEADS, PEER_N_KEYS, PEER_HALF, PEER_TOPK = 8, 128, 128, 16

VMEM_LIMIT_BYTES = 56 * 1024 * 1024
LANES = 128
SUBLANES = 8

C_DNQ, C_DNK, C_DNV, C_DNG = 0, 512, 1024, 1536
C_GLAV, C_GLAG, C_FN, C_CQ = 2048, 2560, 3072, 3584
C_GLAQ, C_GLAK = 4096, 4352
C_CKV, C_KR, C_SMALL = 4608, 4736, 4864
NP = 4992
PROJ_TN = NP // 3

NT_DIMS = (((1,), (1,)), ((), ()))
TN_DIMS = (((0,), (0,)), ((), ()))


def _cparams(sem):
    return pltpu.CompilerParams(dimension_semantics=sem, vmem_limit_bytes=VMEM_LIMIT_BYTES)


def _rms(x):
    return x * lax.rsqrt(jnp.mean(x * x, axis=-1, keepdims=True) + EPS)


def _ada_kernel(c_ref, w_ref, b_ref, o_ref):
    c = c_ref[...]
    a = (c * jax.nn.sigmoid(c)).astype(BF16)
    o_ref[0] = jnp.dot(a, w_ref[0].astype(BF16), preferred_element_type=F32) + b_ref[0]


def ada_mod(c_all, w_ada, b_ada):
    depth, d, n = w_ada.shape
    rows = c_all.shape[0]
    tn = 1024
    return pl.pallas_call(
        _ada_kernel,
        grid=(depth, n // tn),
        in_specs=[
            pl.BlockSpec((rows, d), lambda l, j: (0, 0)),
            pl.BlockSpec((1, d, tn), lambda l, j: (l, 0, j)),
            pl.BlockSpec((1, 1, tn), lambda l, j: (l, 0, j)),
        ],
        out_specs=pl.BlockSpec((1, rows, tn), lambda l, j: (l, 0, j)),
        out_shape=jax.ShapeDtypeStruct((depth, rows, n), F32),
        compiler_params=_cparams(("parallel", "parallel")),
        name="ada_mod",
    )(c_all, w_ada, b_ada.reshape(depth, 1, n))


def _proj_kernel(x_ref, sh_ref, sc_ref, w_ref, *rest):
    o_ref, xn_ref = rest[-2], rest[-1]

    @pl.when(pl.program_id(2) == 0)
    def _():
        xn_ref[...] = (_rms(x_ref[0]) * (1.0 + sc_ref[0]) + sh_ref[0]).astype(BF16)

    o_ref[0] = jnp.dot(xn_ref[...], w_ref[...], preferred_element_type=F32).astype(o_ref.dtype)


def proj_in(x, x_off, n_tiles, tm, shift, scale, w, out_rows, out_off, prev):
    b, _, d = x.shape
    bm = shift.shape[0]
    nj = w.shape[1] // PROJ_TN

    def mod_map(bi, i, j):
        return (bi if bm > 1 else 0, 0, 0)

    in_specs = [
        pl.BlockSpec((1, tm, d), lambda bi, i, j: (bi, i + x_off, 0)),
        pl.BlockSpec((1, 1, d), mod_map),
        pl.BlockSpec((1, 1, d), mod_map),
        pl.BlockSpec((d, PROJ_TN), lambda bi, i, j: (0, j)),
    ]
    args = [x, shift, scale, w]
    aliases = {}
    if prev is not None:
        in_specs.append(pl.BlockSpec(memory_space=pl.ANY))
        args.append(prev)
        aliases = {4: 0}
    return pl.pallas_call(
        _proj_kernel,
        grid=(b, n_tiles, nj),
        in_specs=in_specs,
        out_specs=pl.BlockSpec((1, tm, PROJ_TN), lambda bi, i, j: (bi, i + out_off, j)),
        out_shape=jax.ShapeDtypeStruct((b, out_rows, w.shape[1]), BF16),
        scratch_shapes=[pltpu.VMEM((tm, d), BF16)],
        input_output_aliases=aliases,
        compiler_params=_cparams(("parallel", "arbitrary", "arbitrary")),
        name="proj_in",
    )(*args)


def _mix_kernel(odf, odb, dg, ogf, ogb, gg, ym, yf, x_ref, g1, sh2, sc2, dnw, glw, w_ref, *rest):
    xo_ref, h_ref = rest[-2], rest[-1]

    def gated(of, ob, gate, wn):
        o = of[0].astype(F32) + ob[0].astype(F32)
        on = jnp.concatenate(
            [_rms(o[:, hh * LANES:(hh + 1) * LANES]) for hh in range(o.shape[1] // LANES)], axis=-1)
        g = gate[0].astype(F32)
        return (on * wn[...] * (g * jax.nn.sigmoid(g))).astype(BF16)

    q = w_ref.shape[0] // 4
    acc = jnp.dot(gated(odf, odb, dg, dnw), w_ref[0:q], preferred_element_type=F32)
    acc += jnp.dot(gated(ogf, ogb, gg, glw), w_ref[q:2 * q], preferred_element_type=F32)
    acc += jnp.dot(ym[0], w_ref[2 * q:3 * q], preferred_element_type=F32)
    acc += jnp.dot(yf[0], w_ref[3 * q:4 * q], preferred_element_type=F32)
    xn = x_ref[0] + g1[0] * acc
    xo_ref[0] = xn
    h_ref[0] = (_rms(xn) * (1.0 + sc2[0]) + sh2[0]).astype(BF16)


def mix_out(o_dn_f, o_dn_b, o_gla_f, o_gla_b, y_mla, y_fn, proj, off, x, x_off, n_tiles, tm,
            g1, sh2, sc2, dn_norm, gla_norm, w_out, out_rows, out_off, prev_x, prev_h):
    b, _, d = x.shape
    bm = g1.shape[0]
    wq = 512

    def mix_spec(col):
        return pl.BlockSpec((1, tm, wq), lambda bi, i: (bi, i + off, col))

    def mod_map(bi, i):
        return (bi if bm > 1 else 0, 0, 0)

    in_specs = [
        mix_spec(0), mix_spec(0), mix_spec(C_DNG // wq),
        mix_spec(0), mix_spec(0), mix_spec(C_GLAG // wq),
        mix_spec(0), mix_spec(0),
        pl.BlockSpec((1, tm, d), lambda bi, i: (bi, i + x_off, 0)),
        pl.BlockSpec((1, 1, d), mod_map), pl.BlockSpec((1, 1, d), mod_map), pl.BlockSpec((1, 1, d), mod_map),
        pl.BlockSpec((1, wq), lambda bi, i: (0, 0)), pl.BlockSpec((1, wq), lambda bi, i: (0, 0)),
        pl.BlockSpec((4 * wq, d), lambda bi, i: (0, 0)),
    ]
    args = [o_dn_f, o_dn_b, proj, o_gla_f, o_gla_b, proj, y_mla, y_fn, x, g1, sh2, sc2,
            jnp.tile(dn_norm.astype(F32), DN_HEADS).reshape(1, wq),
            jnp.tile(gla_norm.astype(F32), GLA_HEADS).reshape(1, wq), w_out]
    aliases = {}
    if prev_x is not None:
        in_specs += [pl.BlockSpec(memory_space=pl.ANY), pl.BlockSpec(memory_space=pl.ANY)]
        args += [prev_x, prev_h]
        aliases = {len(args) - 2: 0, len(args) - 1: 1}
    out_spec = pl.BlockSpec((1, tm, d), lambda bi, i: (bi, i + out_off, 0))
    return pl.pallas_call(
        _mix_kernel,
        grid=(b, n_tiles),
        in_specs=in_specs,
        out_specs=[out_spec, out_spec],
        out_shape=[jax.ShapeDtypeStruct((b, out_rows, d), F32), jax.ShapeDtypeStruct((b, out_rows, d), BF16)],
        input_output_aliases=aliases,
        compiler_params=_cparams(("parallel", "parallel")),
        name="mix_out",
    )(*args)


N_RANK = PEER_TOPK + 1
VAL_ROWS = 24
NEG = -1e30


def _candidate_tiles():
    runs = []
    for r1 in range(N_RANK):
        n = N_RANK // (r1 + 1)
        for r2_0 in range(0, n, SUBLANES):
            runs.append((r1, r2_0, min(SUBLANES, n - r2_0)))
    runs.sort(key=lambda t: -t[2])
    tiles, used = [], []
    for r1, r2_0, length in runs:
        for i in range(len(tiles)):
            if used[i] + length <= SUBLANES:
                tiles[i].append((used[i], r1, r2_0, length))
                used[i] += length
                break
        else:
            tiles.append([(0, r1, r2_0, length)])
            used.append(length)
    return tiles


_CAND_TILES = _candidate_tiles()


def _sorting_network(n):
    pairs = []
    p = 1
    while p < n:
        k = p
        while k >= 1:
            for j in range(k % p, n - k, 2 * k):
                for i in range(min(k, n - j - k)):
                    if (i + j) // (2 * p) == (i + j + k) // (2 * p):
                        pairs.append((i + j, i + j + k))
            k //= 2
        p *= 2
    return pairs


def _top_values(work, n):
    levels = [work[t * SUBLANES:(t + 1) * SUBLANES] for t in range(work.shape[0] // SUBLANES)]
    for a, b in _sorting_network(len(levels)):
        levels[a], levels[b] = jnp.maximum(levels[a], levels[b]), jnp.minimum(levels[a], levels[b])
    neg = jnp.full((SUBLANES, LANES), NEG, F32)
    vals = []
    for r in range(n):
        m = jnp.max(levels[0], axis=0, keepdims=True)
        vals.append(m)
        pop = levels[0] >= m
        depth = min(len(levels), n - r)
        for t in range(depth):
            below = levels[t + 1] if t + 1 < len(levels) else neg
            levels[t] = jnp.where(pop, below, levels[t])
    return vals


def _peer_score_kernel(h_ref, wq_ref, keys_ref, s_ref, aux_ref, vals_ref):
    tm = h_ref.shape[0]
    n_chunks = tm // LANES
    q = jnp.dot(h_ref[...], wq_ref[...], preferred_element_type=F32).astype(BF16)
    for hp in range(2 * PEER_HEADS):
        s_ref[hp] = lax.dot_general(keys_ref[hp], q[:, hp * PEER_HALF:(hp + 1) * PEER_HALF], NT_DIMS,
                                    preferred_element_type=F32)

    def stage1(hp, carry):
        for c in range(n_chunks):
            lanes = slice(c * LANES, (c + 1) * LANES)
            vals = _top_values(s_ref[hp, :, lanes], N_RANK)
            vals += [jnp.full((1, LANES), NEG, F32)] * (VAL_ROWS - N_RANK)
            vals_ref[hp, :, lanes] = jnp.concatenate(vals, axis=0)
        return carry

    lax.fori_loop(0, 2 * PEER_HEADS, stage1, 0)

    row = lax.broadcasted_iota(jnp.int32, (SUBLANES, LANES), 0)

    def stage2(hh, carry):
        for c in range(n_chunks):
            lanes = slice(c * LANES, (c + 1) * LANES)
            v1 = vals_ref[2 * hh, :, lanes]
            v2 = vals_ref[2 * hh + 1, :, lanes]
            cands = []
            for segments in _CAND_TILES:
                tile = jnp.full((SUBLANES, LANES), NEG, F32)
                for row0, r1, r2_0, length in segments:
                    blk, off = r2_0 // SUBLANES, r2_0 % SUBLANES
                    src = v2[blk * SUBLANES:(blk + 1) * SUBLANES]
                    shift = (row0 - off) % SUBLANES
                    if shift:
                        src = pltpu.roll(src, shift, axis=0)
                    tile = jnp.where((row >= row0) & (row < row0 + length), src + v1[r1:r1 + 1], tile)
                cands.append(tile)
            t0 = v1[0:1] + v2[0:1]
            taken = jnp.zeros((1, LANES), F32)
            z = jnp.zeros((1, LANES), F32)
            t_k = jnp.zeros((1, LANES), F32)
            t_k1 = jnp.zeros((1, LANES), F32)
            neg = jnp.full((SUBLANES, LANES), NEG, F32)
            levels = cands + [neg] * (SUBLANES - len(cands))
            for a, b in _sorting_network(len(levels)):
                levels[a], levels[b] = jnp.maximum(levels[a], levels[b]), jnp.minimum(levels[a], levels[b])
            for r in range(N_RANK):
                m = jnp.max(levels[0], axis=0, keepdims=True)
                pop = levels[0] >= m
                cnt = jnp.sum(jnp.where(pop, 1.0, 0.0), axis=0, keepdims=True)
                after = taken + cnt
                z = z + jnp.clip(PEER_TOPK - taken, 0.0, cnt) * jnp.exp(m - t0)
                t_k = jnp.where((taken < PEER_TOPK) & (after >= PEER_TOPK), m, t_k)
                t_k1 = jnp.where((taken < N_RANK) & (after >= N_RANK), m, t_k1)
                taken = after
                for t in range(min(len(levels), N_RANK - r)):
                    below = levels[t + 1] if t + 1 < len(levels) else neg
                    levels[t] = jnp.where(pop, below, levels[t])
            tau = 0.5 * (t_k + t_k1)
            aux = jnp.concatenate([tau, v1[0:1], v2[0:1], 1.0 / z, jnp.zeros((4, LANES), F32)], axis=0)
            aux_ref[hh, :, lanes] = aux
        return carry

    lax.fori_loop(0, PEER_HEADS, stage2, 0)


def peer_scores(h, wq, keys, tm):
    tn, d = h.shape
    nk = 2 * PEER_HEADS
    return pl.pallas_call(
        _peer_score_kernel,
        grid=(tn // tm,),
        in_specs=[
            pl.BlockSpec((tm, d), lambda i: (i, 0)),
            pl.BlockSpec(wq.shape, lambda i: (0, 0)),
            pl.BlockSpec(keys.shape, lambda i: (0, 0, 0)),
        ],
        out_specs=[
            pl.BlockSpec((nk, PEER_N_KEYS, tm), lambda i: (0, 0, i)),
            pl.BlockSpec((PEER_HEADS, SUBLANES, tm), lambda i: (0, 0, i)),
        ],
        out_shape=[jax.ShapeDtypeStruct((nk, PEER_N_KEYS, tn), F32),
                   jax.ShapeDtypeStruct((PEER_HEADS, SUBLANES, tn), F32)],
        scratch_shapes=[pltpu.VMEM((nk, VAL_ROWS, tm), F32)],
        compiler_params=_cparams(("parallel",)),
        name="peer_scores",
    )(h, wq, keys)


PEER_TE = 1024
PEER_G2_ROWS = 256


def _gelu_x2(a):
    return a * (1.0 + lax.erf(a * (1.0 / math.sqrt(2.0))))


def _peer_expert_kernel(h_ref, s_ref, aux_ref, u_ref, vt_ref, x_ref, g2_ref, fn_ref, o_ref,
                        acc_ref, e2_ref, act_ref, wt_ref, *, final):
    e = pl.program_id(1)
    tm = h_ref.shape[0]
    n_chunks = tm // LANES
    n_i = PEER_TE // PEER_N_KEYS

    @pl.when(e == 0)
    def _():
        acc_ref[...] = jnp.zeros_like(acc_ref)
        for hh in range(PEER_HEADS):
            e2_ref[hh] = jnp.exp(s_ref[2 * hh + 1] - aux_ref[hh, 2:3, :]) * (0.5 * aux_ref[hh, 3:4, :])

    act_ref[...] = lax.dot_general(u_ref[...], h_ref[...], NT_DIMS, preferred_element_type=F32)

    i0 = pl.multiple_of(e * n_i, n_i)
    thr, e1 = [], []
    for hh in range(PEER_HEADS):
        s1 = s_ref[2 * hh, pl.ds(i0, n_i), :]
        thr.append(aux_ref[hh, 0:1, :] - s1)
        e1.append(jnp.exp(s1 - aux_ref[hh, 1:2, :]))
    for il in range(n_i):
        rows = slice(il * PEER_N_KEYS, (il + 1) * PEER_N_KEYS)
        for c in range(n_chunks):
            lanes = slice(c * LANES, (c + 1) * LANES)
            g = jnp.zeros((PEER_N_KEYS, LANES), F32)
            for hh in range(PEER_HEADS):
                sel = jnp.where(s_ref[2 * hh + 1, :, lanes] >= thr[hh][il:il + 1, lanes], e2_ref[hh, :, lanes], 0.0)
                g = g + sel * e1[hh][il:il + 1, lanes]
            wt_ref[rows, lanes] = (g * _gelu_x2(act_ref[rows, lanes])).astype(BF16)

    acc_ref[...] += lax.dot_general(vt_ref[...], wt_ref[...], TN_DIMS, preferred_element_type=F32)

    @pl.when(e == pl.num_programs(1) - 1)
    def _():
        y = jnp.transpose(acc_ref[...])
        for r in range(tm // PEER_G2_ROWS):
            rows = slice(r * PEER_G2_ROWS, (r + 1) * PEER_G2_ROWS)
            out = x_ref[rows, :] + g2_ref[r] * y[rows, :]
            if final:
                out = _rms(out) * fn_ref[...]
            o_ref[rows, :] = out


def peer_experts(h, s, aux, u, vt, x, g2_rows, final_norm, tm, final):
    tn, d = h.shape
    ne = u.shape[0]
    nk = 2 * PEER_HEADS
    once = pl.Buffered(1)
    return pl.pallas_call(
        functools.partial(_peer_expert_kernel, final=final),
        grid=(tn // tm, ne // PEER_TE),
        in_specs=[
            pl.BlockSpec((tm, d), lambda i, e: (i, 0), pipeline_mode=once),
            pl.BlockSpec((nk, PEER_N_KEYS, tm), lambda i, e: (0, 0, i), pipeline_mode=once),
            pl.BlockSpec((PEER_HEADS, SUBLANES, tm), lambda i, e: (0, 0, i)),
            pl.BlockSpec((PEER_TE, d), lambda i, e: (e, 0)),
            pl.BlockSpec((PEER_TE, d), lambda i, e: (e, 0)),
            pl.BlockSpec((tm, d), lambda i, e: (i, 0), pipeline_mode=once),
            pl.BlockSpec((tm // PEER_G2_ROWS, 1, d), lambda i, e: (i, 0, 0)),
            pl.BlockSpec((1, d), lambda i, e: (0, 0)),
        ],
        out_specs=pl.BlockSpec((tm, d), lambda i, e: (i, 0)),
        out_shape=jax.ShapeDtypeStruct((tn, d), F32),
        scratch_shapes=[
            pltpu.VMEM((d, tm), F32),
            pltpu.VMEM((PEER_HEADS, PEER_N_KEYS, tm), F32),
            pltpu.VMEM((PEER_TE, tm), F32),
            pltpu.VMEM((PEER_TE, tm), BF16),
        ],
        compiler_params=_cparams(("parallel", "arbitrary")),
        name="peer_experts",
    )(h, s, aux, u, vt, x, g2_rows, final_norm.reshape(1, d).astype(F32))


IN_LAYOUT = (
    ("dn_qkv", 1536), ("dn_gate", 512), ("dn_a", 8), ("dn_b", 8), ("gla_q", 256), ("gla_k", 256),
    ("gla_v", 512), ("gla_gate", 512), ("gla_alpha", 32), ("mla_cq", 448), ("mla_ckv", 128),
    ("mla_kr", 64), ("fn", 512),
)


def _in_cols():
    offs, o = {}, 0
    for name, w in IN_LAYOUT:
        offs[name] = (o, o + w)
        o += w
    return offs


def _rot_half_cols(w):
    return jnp.concatenate([-w[:, 16:32], w[:, 0:16], -w[:, 48:64], w[:, 32:48]], axis=1)


def prep_w_in(w_in):
    c = _in_cols()
    d = w_in.shape[0]

    def cols(name):
        a, b = c[name]
        return w_in[:, a:b]

    z = lambda n: jnp.zeros((d, n), w_in.dtype)
    kr = cols("mla_kr")
    pieces = [
        cols("dn_qkv"), cols("dn_gate"), cols("gla_v"), cols("gla_gate"), cols("fn"),
        cols("mla_cq"), z(512 - MLA_Q_RANK), cols("gla_q"), cols("gla_k"), cols("mla_ckv"),
        kr, _rot_half_cols(kr), cols("dn_a"), cols("dn_b"), cols("gla_alpha"), z(128 - 48),
    ]
    w = jnp.concatenate(pieces, axis=1)
    assert w.shape[1] == NP
    return w.astype(BF16)


MLA_SLOT = 256


def prep_mla_weights(q_norm, w_uq, kv_norm, w_ukv):
    scale = (MLA_NOPE + MLA_ROPE) ** -0.5 * math.log2(math.e)
    wq = w_uq * q_norm[:, None] * scale
    per = MLA_NOPE + MLA_ROPE
    cols = []
    for h in range(MLA_HEADS):
        nope = wq[:, h * per:h * per + MLA_NOPE]
        rope = wq[:, h * per + MLA_NOPE:(h + 1) * per]
        cols += [nope, rope, _rot_half_cols(rope)]
    wq_p = jnp.concatenate(cols, axis=1)
    wq_p = jnp.concatenate([wq_p, jnp.zeros((512 - MLA_Q_RANK, wq_p.shape[1]), wq_p.dtype)], axis=0)
    wkv = w_ukv * kv_norm[:, None]
    per = MLA_NOPE + MLA_V
    wkv_p = jnp.concatenate([wkv[:, h * per:h * per + MLA_NOPE] for h in range(MLA_HEADS)]
                            + [wkv[:, h * per + MLA_NOPE:(h + 1) * per] for h in range(MLA_HEADS)], axis=1)
    return wq_p.astype(BF16), wkv_p.astype(BF16)


def rope_table(l, lc):
    rows = l // GRID_W
    row = np.repeat(np.arange(rows), GRID_W).astype(np.float32)
    col = np.tile(np.arange(GRID_W), rows).astype(np.float32)
    half = MLA_ROPE // 2
    inv_freq = jnp.asarray(ROPE_BASE, F32) ** (-jnp.arange(0, half, 2, dtype=F32) / half)
    ang_r = jnp.asarray(row)[:, None] * inv_freq
    ang_c = jnp.asarray(col)[:, None] * inv_freq
    cos = jnp.concatenate([jnp.cos(ang_r)] * 2 + [jnp.cos(ang_c)] * 2, axis=1)
    sin = jnp.concatenate([jnp.sin(ang_r)] * 2 + [jnp.sin(ang_c)] * 2, axis=1)
    lat = jnp.concatenate([cos, sin], axis=1)
    ctx = jnp.concatenate([jnp.ones((lc, MLA_ROPE), F32), jnp.zeros((lc, MLA_ROPE), F32)], axis=1)
    return jnp.concatenate([lat, ctx], axis=0)


def _mla_prep_kernel(cq_ref, ckv_ref, kr_ref, cs_ref, wq_ref, wkv_ref, q_ref, k_ref, v_ref):
    cs = cs_ref[...]
    half = LANES // 2
    lane = lax.broadcasted_iota(jnp.int32, cs.shape, 1)

    def rotary(u):
        return u + pltpu.roll(u, half, axis=1)

    cq = cq_ref[0].astype(F32)
    cqn = cq * lax.rsqrt(jnp.sum(cq * cq, axis=-1, keepdims=True) * (1.0 / MLA_Q_RANK) + EPS)
    q = jnp.dot(cqn.astype(BF16), wq_ref[...], preferred_element_type=F32)
    qs = []
    for h in range(MLA_HEADS):
        qs.append(q[:, h * MLA_SLOT:h * MLA_SLOT + LANES])
        qs.append(rotary(q[:, h * MLA_SLOT + LANES:(h + 1) * MLA_SLOT] * cs))
    q_ref[0] = jnp.concatenate(qs, axis=-1).astype(BF16)

    kv = jnp.dot(_rms(ckv_ref[0].astype(F32)).astype(BF16), wkv_ref[...], preferred_element_type=F32)
    kr = jnp.where(lane < half, rotary(kr_ref[0].astype(F32) * cs), 0.0)
    ks = []
    for h in range(MLA_HEADS):
        ks += [kv[:, h * LANES:(h + 1) * LANES], kr]
    k_ref[0] = jnp.concatenate(ks, axis=-1).astype(BF16)
    v_ref[0] = kv[:, MLA_HEADS * LANES:].astype(BF16)


def mla_prep(proj, cs, wq_p, wkv_p, tm):
    b, lt, _ = proj.shape
    nq = MLA_HEADS * MLA_SLOT
    return pl.pallas_call(
        _mla_prep_kernel,
        grid=(b, lt // tm),
        in_specs=[
            pl.BlockSpec((1, tm, 512), lambda bi, i: (bi, i, C_CQ // 512)),
            pl.BlockSpec((1, tm, LANES), lambda bi, i: (bi, i, C_CKV // LANES)),
            pl.BlockSpec((1, tm, LANES), lambda bi, i: (bi, i, C_KR // LANES)),
            pl.BlockSpec((tm, LANES), lambda bi, i: (i, 0)),
            pl.BlockSpec(wq_p.shape, lambda bi, i: (0, 0)),
            pl.BlockSpec(wkv_p.shape, lambda bi, i: (0, 0)),
        ],
        out_specs=[
            pl.BlockSpec((1, tm, nq), lambda bi, i: (bi, i, 0)),
            pl.BlockSpec((1, tm, nq), lambda bi, i: (bi, i, 0)),
            pl.BlockSpec((1, tm, MLA_HEADS * MLA_V), lambda bi, i: (bi, i, 0)),
        ],
        out_shape=[jax.ShapeDtypeStruct((b, lt, nq), BF16), jax.ShapeDtypeStruct((b, lt, nq), BF16),
                   jax.ShapeDtypeStruct((b, lt, MLA_HEADS * MLA_V), BF16)],
        compiler_params=_cparams(("parallel", "parallel")),
        name="mla_prep",
    )(proj, proj, proj, cs, wq_p, wkv_p)


def _mla_attn_kernel(q_ref, k_ref, v_ref, *rest):
    o_ref = rest[-1]
    s = lax.dot_general(q_ref[0], k_ref[0], NT_DIMS, preferred_element_type=F32)
    p = jnp.exp2(s - jnp.max(s, axis=-1, keepdims=True))
    o = jnp.dot(p.astype(BF16), v_ref[0], preferred_element_type=F32)
    o_ref[0] = (o / jnp.sum(p, axis=-1, keepdims=True)).astype(o_ref.dtype)


def mla_attend(q4, k4, v4, q_off, n_tiles, tq, k_rows, k_blk, prev):
    b, lt, _ = q4.shape
    in_specs = [
        pl.BlockSpec((1, tq, MLA_SLOT), lambda bi, h, i: (bi, i + q_off, h)),
        pl.BlockSpec((1, k_rows, MLA_SLOT), lambda bi, h, i: (bi, k_blk, h)),
        pl.BlockSpec((1, k_rows, MLA_V), lambda bi, h, i: (bi, k_blk, h)),
    ]
    args = [q4, k4, v4]
    aliases = {}
    if prev is not None:
        in_specs.append(pl.BlockSpec(memory_space=pl.ANY))
        args.append(prev)
        aliases = {3: 0}
    return pl.pallas_call(
        _mla_attn_kernel,
        grid=(b, MLA_HEADS, n_tiles),
        in_specs=in_specs,
        out_specs=pl.BlockSpec((1, tq, MLA_V), lambda bi, h, i: (bi, i + q_off, h)),
        out_shape=jax.ShapeDtypeStruct((b, lt, MLA_HEADS * MLA_V), BF16),
        input_output_aliases=aliases,
        compiler_params=_cparams(("parallel", "parallel", "parallel")),
        name="mla_attend",
    )(*args)


def dft_tables(n, scale):
    j = np.arange(n, dtype=np.int64)
    root = int(round(math.sqrt(n))) if int(round(math.sqrt(n))) ** 2 == n else 1
    if root == 1:
        ph = 2.0 * np.pi * ((j[:, None] * j[None, :]) % n) / n
        return jnp.asarray(np.cos(ph) * scale, BF16), jnp.asarray(np.sin(ph) * scale, BF16)
    u = np.arange(root, dtype=np.int64)
    pa = 2.0 * np.pi * ((j[:, None] * (root * u)[None, :]) % n) / n
    pb = 2.0 * np.pi * ((j[:, None] * u[None, :]) % n) / n
    ca, sa, cb, sb = (jnp.asarray(t, F32) for t in (np.cos(pa), np.sin(pa), np.cos(pb) * scale, np.sin(pb) * scale))
    cos = ca[:, :, None] * cb[:, None, :] - sa[:, :, None] * sb[:, None, :]
    sin = sa[:, :, None] * cb[:, None, :] + ca[:, :, None] * sb[:, None, :]
    return cos.reshape(n, n).astype(BF16), sin.reshape(n, n).astype(BF16)


def channel_dft_tables():
    c, s = dft_tables(FN_CH, 1.0)
    eye = jnp.eye(FN_GROUPS, dtype=F32)
    return (jnp.kron(eye, c.astype(F32)).astype(BF16), jnp.kron(eye, s.astype(F32)).astype(BF16))


def _fourier_kernel(z_ref, cl_ref, sl_ref, cc_ref, sc_ref, *rest):
    o_ref = rest[-1]
    z = z_ref[0]
    re = jnp.dot(cl_ref[...], z, preferred_element_type=F32).astype(BF16)
    im = jnp.dot(sl_ref[...], z, preferred_element_type=F32).astype(BF16)
    y = jnp.dot(re, cc_ref[...], preferred_element_type=F32) - jnp.dot(im, sc_ref[...], preferred_element_type=F32)
    o_ref[0] = y.astype(o_ref.dtype)


def fourier_mix(proj, seq_blk, n, cl, sl, cc, sc, prev):
    b, lt, _ = proj.shape
    w = FN_GROUPS * FN_CH
    tm = min(512, n)
    nt = n // tm
    in_specs = [
        pl.BlockSpec((1, n, w), lambda i, bi: (bi, seq_blk, C_FN // w)),
        pl.BlockSpec((tm, n), lambda i, bi: (i, 0)),
        pl.BlockSpec((tm, n), lambda i, bi: (i, 0)),
        pl.BlockSpec((w, w), lambda i, bi: (0, 0)),
        pl.BlockSpec((w, w), lambda i, bi: (0, 0)),
    ]
    args = [proj, cl, sl, cc, sc]
    aliases = {}
    if prev is not None:
        in_specs.append(pl.BlockSpec(memory_space=pl.ANY))
        args.append(prev)
        aliases = {5: 0}
    return pl.pallas_call(
        _fourier_kernel,
        grid=(nt, b),
        in_specs=in_specs,
        out_specs=pl.BlockSpec((1, tm, w), lambda i, bi: (bi, seq_blk * nt + i, 0)),
        out_shape=jax.ShapeDtypeStruct((b, lt, w), BF16),
        input_output_aliases=aliases,
        compiler_params=_cparams(("parallel", "parallel")),
        name="fourier_mix",
    )(*args)


CHUNK = 64
BMM_DIMS = (((2,), (1,)), ((0,), (0,)))
BMM_NT_DIMS = (((2,), (2,)), ((0,), (0,)))


STATE_GROUP = 4


def _scan_maps(l, lc):
    span = STATE_GROUP * CHUNK
    assert l % span == 0 and lc % span == 0
    n_lat, n_ctx = l // span, lc // span

    def fwd(n):
        return jnp.where(n < n_ctx, n + n_lat, n - n_ctx)

    def rev(n):
        return n_lat + n_ctx - 1 - n

    return n_lat + n_ctx, fwd, rev


def _bmm(a, b, dims=BMM_DIMS):
    return lax.dot_general(a, b, dims, preferred_element_type=F32)


def _bmm_exact_lhs(m, x):
    mb = m.astype(BF16)
    hi = x.astype(BF16)
    r1 = x - hi.astype(F32)
    mid = r1.astype(BF16)
    lo = (r1 - mid.astype(F32)).astype(BF16)
    return _bmm(mb, hi) + _bmm(mb, mid) + _bmm(mb, lo)


GLA_PREP_CHUNKS = 4
GLA_NQ = GLA_HEADS * GLA_DK
GLA_NV = GLA_HEADS * GLA_DV


def _gla_prep_kernel(q_ref, k_ref, v_ref, a_ref, w2_ref, b2_ref, m_ref, *out_refs):
    nb = 2 * GLA_PREP_CHUNKS
    inc = m_ref[...]
    lane_q = lax.broadcasted_iota(jnp.int32, (nb, CHUNK, GLA_NQ), 2)
    a, q, k, v = [], [], [], []
    for j in range(GLA_PREP_CHUNKS):
        rows = slice(j * CHUNK, (j + 1) * CHUNK)
        for d in range(2):
            a.append(a_ref[0, rows, :].astype(F32))
            q.append(q_ref[0, rows, :].astype(F32) * (GLA_DK ** -0.5))
            k.append(k_ref[0, rows, :].astype(F32))
            v.append(v_ref[0, rows, :])
    a, q, k, v = jnp.stack(a), jnp.stack(q), jnp.stack(k), jnp.stack(v)
    w2 = jnp.stack([w2_ref[it % 2] for it in range(nb)])
    b2 = jnp.stack([jnp.broadcast_to(b2_ref[it % 2], (CHUNK, GLA_NQ)) for it in range(nb)])
    z = lax.dot_general(a, w2, BMM_DIMS, preferred_element_type=F32, precision=lax.Precision.HIGHEST) + b2
    log_a = jax.nn.log_sigmoid(z) * (1.0 / GLA_TAU)
    bcum = _bmm_exact_lhs(inc, log_a)
    b_rows = [bcum[it, (0 if it % 2 else CHUNK - 1):(1 if it % 2 else CHUNK), :] for it in range(nb)]
    b_last = jnp.stack([jnp.broadcast_to(row, (CHUNK, GLA_NQ)) for row in b_rows])
    q_in = (q * jnp.exp(bcum)).astype(BF16)
    q_rel = q * jnp.exp(bcum - b_last)
    k_rel = (k * jnp.exp(b_last - bcum)).astype(BF16)
    inc_f = inc.astype(F32)
    outs = []
    for h in range(GLA_HEADS):
        qh = jnp.where(lane_q // GLA_DK == h, q_rel, 0.0).astype(BF16)
        att = (_bmm(qh, k_rel, BMM_NT_DIMS) * inc_f).astype(BF16)
        outs.append(_bmm(att, v[:, :, h * GLA_DV:(h + 1) * GLA_DV]))
    o_intra = jnp.concatenate(outs, axis=-1).astype(BF16)
    for j in range(GLA_PREP_CHUNKS):
        rows = slice(j * CHUNK, (j + 1) * CHUNK)
        for d in range(2):
            it = 2 * j + d
            o_ref, qi_ref, kr_ref, e_ref = out_refs[4 * d:4 * d + 4]
            o_ref[0, rows, :] = o_intra[it]
            qi_ref[0, rows, :] = q_in[it]
            kr_ref[0, rows, :] = k_rel[it]
            e_ref[0, j] = jnp.exp(jnp.broadcast_to(b_rows[it], (SUBLANES, GLA_NQ)))


def _gla_state_kernel(*refs):
    (oif, qif, krf, ef, vf, oir, qir, krr, er, vr, of_ref, or_ref, st_ref) = refs

    @pl.when(pl.program_id(1) == 0)
    def _():
        st_ref[...] = jnp.zeros_like(st_ref)

    bd = (lax.broadcasted_iota(jnp.int32, (GLA_NV, GLA_NQ), 0) // GLA_DV) == (
        lax.broadcasted_iota(jnp.int32, (GLA_NV, GLA_NQ), 1) // GLA_DK)
    groups = ((oif, qif, krf, ef, vf, of_ref), (oir, qir, krr, er, vr, or_ref))
    for jj in range(STATE_GROUP):
        for d, (oi_ref, qi_ref, kr_ref, e_ref, v_ref, o_ref) in enumerate(groups):
            j = STATE_GROUP - 1 - jj if d == 1 else jj
            rows = slice(j * CHUNK, (j + 1) * CHUNK)
            st = st_ref[d]
            o_inter = lax.dot_general(qi_ref[0, rows, :], st.astype(BF16), NT_DIMS, preferred_element_type=F32)
            o_ref[0, rows, :] = (oi_ref[0, rows, :].astype(F32) + o_inter).astype(o_ref.dtype)
            upd = lax.dot_general(v_ref[0, rows, :], kr_ref[0, rows, :], TN_DIMS, preferred_element_type=F32)
            st_ref[d] = st * e_ref[0, j, 0:1, :] + jnp.where(bd, upd, 0.0)


def gla_scan(proj, w2, b2, l, lc):
    b, lt, _ = proj.shape
    n_steps, fwd, rev = _scan_maps(l, lc)
    nq, nv = GLA_NQ, GLA_NV
    n_chunks = lt // CHUNK
    rows = GLA_PREP_CHUNKS * CHUNK
    w2p = jnp.zeros((2, LANES, nq), F32)
    for d in range(2):
        lo = 2 * DN_HEADS * 2 + GLA_RANK * d
        w2p = w2p.at[d, lo:lo + GLA_RANK].set(w2[d].astype(F32))
    r, c = np.meshgrid(np.arange(CHUNK), np.arange(CHUNK), indexing="ij")
    inc = jnp.asarray(np.stack([(c >= r) if it % 2 else (c <= r) for it in range(2 * GLA_PREP_CHUNKS)])
                      .astype(np.float32), BF16)

    wide = jax.ShapeDtypeStruct((b, lt, nv), BF16)
    mid = jax.ShapeDtypeStruct((b, lt, nq), BF16)
    small = jax.ShapeDtypeStruct((b, n_chunks, SUBLANES, nq), F32)
    wide_spec = pl.BlockSpec((1, rows, nv), lambda bi, i: (bi, i, 0))
    mid_spec = pl.BlockSpec((1, rows, nq), lambda bi, i: (bi, i, 0))
    small_spec = pl.BlockSpec((1, GLA_PREP_CHUNKS, SUBLANES, nq), lambda bi, i: (bi, i, 0, 0))
    terms = pl.pallas_call(
        _gla_prep_kernel,
        grid=(b, n_chunks // GLA_PREP_CHUNKS),
        in_specs=[
            pl.BlockSpec((1, rows, nq), lambda bi, i: (bi, i, C_GLAQ // nq)),
            pl.BlockSpec((1, rows, nq), lambda bi, i: (bi, i, C_GLAK // nq)),
            pl.BlockSpec((1, rows, nv), lambda bi, i: (bi, i, C_GLAV // nv)),
            pl.BlockSpec((1, rows, LANES), lambda bi, i: (bi, i, C_SMALL // LANES)),
            pl.BlockSpec((2, LANES, nq), lambda bi, i: (0, 0, 0)),
            pl.BlockSpec((2, 1, nq), lambda bi, i: (0, 0, 0)),
            pl.BlockSpec(inc.shape, lambda bi, i: (0, 0, 0)),
        ],
        out_specs=[wide_spec, mid_spec, mid_spec, small_spec] * 2,
        out_shape=[wide, mid, mid, small] * 2,
        compiler_params=_cparams(("parallel", "parallel")),
        name="gla_prep",
    )(proj, proj, proj, proj, w2p, b2.astype(F32).reshape(2, 1, nq), inc)

    span = STATE_GROUP * CHUNK

    def specs(cmap):
        return [pl.BlockSpec((1, span, nv), lambda bi, n: (bi, cmap(n), 0)),
                pl.BlockSpec((1, span, nq), lambda bi, n: (bi, cmap(n), 0)),
                pl.BlockSpec((1, span, nq), lambda bi, n: (bi, cmap(n), 0)),
                pl.BlockSpec((1, STATE_GROUP, SUBLANES, nq), lambda bi, n: (bi, cmap(n), 0, 0)),
                pl.BlockSpec((1, span, nv), lambda bi, n: (bi, cmap(n), C_GLAV // nv))]

    return pl.pallas_call(
        _gla_state_kernel,
        grid=(b, n_steps),
        in_specs=specs(fwd) + specs(rev),
        out_specs=[pl.BlockSpec((1, span, nv), lambda bi, n: (bi, fwd(n), 0)),
                   pl.BlockSpec((1, span, nv), lambda bi, n: (bi, rev(n), 0))],
        out_shape=[wide, wide],
        scratch_shapes=[pltpu.VMEM((2, nv, nq), F32)],
        compiler_params=_cparams(("parallel", "arbitrary")),
        name="gla_state",
    )(*terms[0:4], proj, *terms[4:8], proj)


DN_TL = 256
DN_W = DN_HEADS * DN_DK


def _dn_conv_kernel(x_ref, p_ref, n_ref, w_ref, o_ref, *, first_tiles, last_tiles):
    i = pl.program_id(1)
    tl = x_ref.shape[1]
    is_first = functools.reduce(jnp.logical_or, [i == t for t in first_tiles])
    is_last = functools.reduce(jnp.logical_or, [i == t for t in last_tiles])
    prev = jnp.where(is_first, 0.0, p_ref[0].astype(F32))
    nxt = jnp.where(is_last, 0.0, n_ref[0].astype(F32))
    ext = jnp.concatenate([prev, x_ref[0].astype(F32), nxt], axis=0)
    pad = DN_CONV // 2
    acc = jnp.zeros((tl, ext.shape[1]), F32)
    for t in range(DN_CONV):
        start = SUBLANES - pad + t
        acc = acc + ext[start:start + tl, :] * w_ref[t:t + 1, :]
    y = acc * jax.nn.sigmoid(acc)
    outs = []
    for h in range(3 * DN_HEADS):
        yh = y[:, h * DN_DK:(h + 1) * DN_DK]
        if h < 2 * DN_HEADS:
            yh = yh * lax.rsqrt(jnp.sum(yh * yh, axis=-1, keepdims=True) + EPS)
            if h < DN_HEADS:
                yh = yh * (DN_DK ** -0.5)
        outs.append(yh)
    o_ref[0] = jnp.concatenate(outs, axis=-1).astype(o_ref.dtype)


def dn_conv_norm(proj, conv_w, l, lc):
    b, lt, _ = proj.shape
    w = 3 * DN_W
    tl = DN_TL
    hb = tl // SUBLANES
    n_halo = lt // SUBLANES
    first_tiles = (0, l // tl)
    last_tiles = (l // tl - 1, lt // tl - 1)
    return pl.pallas_call(
        functools.partial(_dn_conv_kernel, first_tiles=first_tiles, last_tiles=last_tiles),
        grid=(b, lt // tl),
        in_specs=[
            pl.BlockSpec((1, tl, w), lambda bi, i: (bi, i, 0)),
            pl.BlockSpec((1, SUBLANES, w), lambda bi, i: (bi, jnp.maximum(i * hb - 1, 0), 0)),
            pl.BlockSpec((1, SUBLANES, w), lambda bi, i: (bi, jnp.minimum((i + 1) * hb, n_halo - 1), 0)),
            pl.BlockSpec((DN_CONV, w), lambda bi, i: (0, 0)),
        ],
        out_specs=pl.BlockSpec((1, tl, w), lambda bi, i: (bi, i, 0)),
        out_shape=jax.ShapeDtypeStruct((b, lt, w), BF16),
        compiler_params=_cparams(("parallel", "parallel")),
        name="dn_conv",
    )(proj, proj, proj, conv_w.astype(F32))


def _stack_heads(x, width):
    return jnp.concatenate([x[:, h * width:(h + 1) * width] for h in range(DN_HEADS)], axis=0)


DN_PREP_CHUNKS = 2
DN_N = DN_HEADS * CHUNK


def dn_masks():
    n = DN_N
    r, c = np.meshgrid(np.arange(n), np.arange(n), indexing="ij")
    same = (r // CHUNK) == (c // CHUNK)
    kinds = [[r == c] * 2, [same & (c <= r), same & (c >= r)], [same & (c < r), same & (c > r)]]
    size = 1
    while size < CHUNK:
        blk = (r // (2 * size)) == (c // (2 * size))
        lo_r, lo_c = (r % (2 * size)) < size, (c % (2 * size)) < size
        kinds.append([blk & ~lo_r & lo_c, blk & lo_r & ~lo_c])
        size *= 2
    m = np.stack([np.stack([k[it % 2] for it in range(2 * DN_PREP_CHUNKS)]) for k in kinds])
    return jnp.asarray(m.astype(np.float32), BF16)


def _dn_prep_kernel(x_ref, a_ref, gp_ref, m_ref, *out_refs):
    n = DN_N
    nb = 2 * DN_PREP_CHUNKS
    shape = (nb, n, n)
    eye, inc, strict = m_ref[0], m_ref[1], m_ref[2]

    def pair_mask(size):
        return m_ref[3 + int(math.log2(size))]

    qs, ks, vs, g_col, b_col = [], [], [], [], []
    for j in range(DN_PREP_CHUNKS):
        rows = slice(j * CHUNK, (j + 1) * CHUNK)
        x = x_ref[0, rows, :].astype(F32)
        a = a_ref[0, rows, :].astype(F32)
        g8 = gp_ref[0:1, :] * jax.nn.softplus(a + gp_ref[1:2, :])
        beta8 = jax.nn.sigmoid(a)
        for d in range(2):
            qs.append(_stack_heads(x[:, 0:DN_W], DN_DK))
            ks.append(_stack_heads(x[:, DN_W:2 * DN_W], DN_DK))
            vs.append(_stack_heads(x[:, 2 * DN_W:3 * DN_W], DN_DV))
            lo = DN_HEADS * d
            g_col.append(jnp.concatenate([g8[:, lo + h:lo + h + 1] for h in range(DN_HEADS)], axis=0))
            lo = 2 * DN_HEADS + DN_HEADS * d
            b_col.append(jnp.concatenate([beta8[:, lo + h:lo + h + 1] for h in range(DN_HEADS)], axis=0))
    qs, ks, vs = jnp.stack(qs), jnp.stack(ks), jnp.stack(vs)
    g_b = jnp.broadcast_to(jnp.stack(g_col), shape)
    beta_b = jnp.broadcast_to(jnp.stack(b_col), shape)
    diff = _bmm_exact_lhs(inc, g_b * strict.astype(F32))
    decay = jnp.exp(diff) * inc.astype(F32)
    gc = []
    for it in range(nb):
        first = CHUNK - 1 if it % 2 else 0
        blocks = []
        for h in range(DN_HEADS):
            col = h * CHUNK + first
            cum = diff[it, h * CHUNK:(h + 1) * CHUNK, col:col + 1] + g_col[it][col:col + 1, :]
            blocks.append(jnp.broadcast_to(cum, (CHUNK, DN_DK)))
        gc.append(jnp.concatenate(blocks, axis=0))
    gc = jnp.stack(gc)
    kb = ks.astype(BF16)
    kk = _bmm(kb, kb, BMM_NT_DIMS)
    a_mat = beta_b * kk * decay * strict.astype(F32)
    a_b = a_mat.astype(BF16)
    inv = eye.astype(F32) - a_mat * pair_mask(1).astype(F32)
    size = 2
    while size < CHUNK:
        t = _bmm(inv.astype(BF16), a_b * pair_mask(size))
        inv = inv - _bmm(t.astype(BF16), inv.astype(BF16))
        size *= 2
    beta_k = beta_b[:, :, :DN_DK]
    e_gc = jnp.exp(gc)
    invb = inv.astype(BF16)
    u = _bmm(invb, (vs * beta_k).astype(BF16))
    w = _bmm(invb, (ks * beta_k * e_gc).astype(BF16))
    p = (_bmm(qs.astype(BF16), kb, BMM_NT_DIMS) * decay).astype(BF16)
    q_dec = qs * e_gc
    for j in range(DN_PREP_CHUNKS):
        for d in range(2):
            it = 2 * j + d
            last = 0 if d == 1 else CHUNK - 1
            g_end = [gc[it, h * CHUNK + last:h * CHUNK + last + 1, :] for h in range(DN_HEADS)]
            gl = jnp.concatenate([jnp.broadcast_to(g, (CHUNK, DN_DK)) for g in g_end], axis=0)
            k_dec = ks[it] * jnp.exp(gl - gc[it])
            uw_ref, qk_ref, p_ref, e_ref = out_refs[4 * d:4 * d + 4]
            uw_ref[0, j] = jnp.concatenate([u[it], w[it]], axis=-1).astype(BF16)
            qk_ref[0, j] = jnp.concatenate([q_dec[it], k_dec], axis=-1).astype(BF16)
            p_ref[0, j] = p[it]
            e_ref[0, j] = jnp.exp(jnp.concatenate(g_end + [jnp.zeros((SUBLANES - DN_HEADS, DN_DK), F32)], axis=0))


def dn_prep(qkv, proj, a_log, dt_bias):
    b, lt, _ = qkv.shape
    w = 3 * DN_W
    n_chunks = lt // CHUNK
    gp = jnp.zeros((SUBLANES, LANES), F32)
    gp = gp.at[0, :2 * DN_HEADS].set(-jnp.exp(a_log.astype(F32)).reshape(-1))
    gp = gp.at[1, :2 * DN_HEADS].set(dt_bias.astype(F32).reshape(-1))
    rows = DN_PREP_CHUNKS * CHUNK
    big = jax.ShapeDtypeStruct((b, n_chunks, DN_N, 2 * DN_DK), BF16)
    small = jax.ShapeDtypeStruct((b, n_chunks, SUBLANES, DN_DK), F32)
    big_spec = pl.BlockSpec((1, DN_PREP_CHUNKS, DN_N, 2 * DN_DK), lambda bi, i: (bi, i, 0, 0))
    small_spec = pl.BlockSpec((1, DN_PREP_CHUNKS, SUBLANES, DN_DK), lambda bi, i: (bi, i, 0, 0))
    masks = dn_masks()
    return pl.pallas_call(
        _dn_prep_kernel,
        grid=(b, n_chunks // DN_PREP_CHUNKS),
        in_specs=[pl.BlockSpec((1, rows, w), lambda bi, i: (bi, i, 0)),
                  pl.BlockSpec((1, rows, LANES), lambda bi, i: (bi, i, C_SMALL // LANES)),
                  pl.BlockSpec((SUBLANES, LANES), lambda bi, i: (0, 0)),
                  pl.BlockSpec(masks.shape, lambda bi, i: (0, 0, 0, 0))],
        out_specs=[big_spec, big_spec, big_spec, small_spec] * 2,
        out_shape=[big, big, big, small] * 2,
        compiler_params=_cparams(("parallel", "parallel")),
        name="dn_prep",
    )(qkv, proj, gp, masks)


def _dn_state_kernel(*refs):
    (uwf, qkf, pf, ef, uwr, qkr, pr, er, of_ref, or_ref, s_ref) = refs

    @pl.when(pl.program_id(1) == 0)
    def _():
        s_ref[...] = jnp.zeros_like(s_ref)

    groups = ((uwf, qkf, pf, ef, of_ref), (uwr, qkr, pr, er, or_ref))
    for jj in range(STATE_GROUP):
        for d, (uw_ref, qk_ref, p_ref, e_ref, o_ref) in enumerate(groups):
            j = STATE_GROUP - 1 - jj if d == 1 else jj
            uw = uw_ref[0, j]
            qk = qk_ref[0, j]
            v_new, o_state = [], []
            for h in range(DN_HEADS):
                rows = slice(h * CHUNK, (h + 1) * CHUNK)
                sb = s_ref[d, h].astype(BF16)
                v_new.append(uw[rows, :DN_DV].astype(F32)
                             - jnp.dot(uw[rows, DN_DV:], sb, preferred_element_type=F32))
                o_state.append(jnp.dot(qk[rows, :DN_DK], sb, preferred_element_type=F32))
            v_new = jnp.concatenate(v_new, axis=0).astype(BF16)
            o_st = jnp.dot(p_ref[0, j], v_new, preferred_element_type=F32)
            outs = []
            for h in range(DN_HEADS):
                rows = slice(h * CHUNK, (h + 1) * CHUNK)
                outs.append(o_st[rows] + o_state[h])
                s_ref[d, h] = s_ref[d, h] * e_ref[0, j, h:h + 1, :] + lax.dot_general(
                    qk[rows, DN_DK:], v_new[rows], TN_DIMS, preferred_element_type=F32)
            o_ref[0, j * CHUNK:(j + 1) * CHUNK, :] = jnp.concatenate(outs, axis=-1).astype(o_ref.dtype)


def dn_scan(qkv, proj, a_log, dt_bias, l, lc):
    b, lt, _ = qkv.shape
    n_steps, fwd, rev = _scan_maps(l, lc)
    terms = dn_prep(qkv, proj, a_log, dt_bias)
    g = STATE_GROUP

    def specs(cmap):
        big = pl.BlockSpec((1, g, DN_N, 2 * DN_DK), lambda bi, n: (bi, cmap(n), 0, 0))
        small = pl.BlockSpec((1, g, SUBLANES, DN_DK), lambda bi, n: (bi, cmap(n), 0, 0))
        return [big, big, big, small]

    out_sd = jax.ShapeDtypeStruct((b, lt, DN_HEADS * DN_DV), BF16)
    return pl.pallas_call(
        _dn_state_kernel,
        grid=(b, n_steps),
        in_specs=specs(fwd) + specs(rev),
        out_specs=[pl.BlockSpec((1, g * CHUNK, DN_HEADS * DN_DV), lambda bi, n: (bi, fwd(n), 0)),
                   pl.BlockSpec((1, g * CHUNK, DN_HEADS * DN_DV), lambda bi, n: (bi, rev(n), 0))],
        out_shape=[out_sd, out_sd],
        scratch_shapes=[pltpu.VMEM((2, DN_HEADS, DN_DK, DN_DV), F32)],
        compiler_params=_cparams(("parallel", "arbitrary")),
        name="dn_scan",
    )(*terms)


CTX_TM = 256


def kernel(x, c, ctx, c_ctx, w_ada, b_ada, w_in, dn_conv, dn_a_log, dn_dt_bias, dn_norm, gla_w2, gla_b2,
           gla_norm, mla_q_norm, mla_w_uq, mla_kv_norm, mla_w_ukv, w_out, peer_wq, peer_keys, peer_u, peer_v,
           final_norm):
    b, l, d = x.shape
    lc = ctx.shape[1]
    lt = l + lc
    depth = w_ada.shape[0]
    lat_tm = min(1024, l)
    mix_tm = min(512, l)
    peer_tm = 512
    assert lc == CTX_TM and l % lat_tm == 0 and l % CTX_TM == 0

    pad = (-(b + 1)) % SUBLANES
    c_all = jnp.concatenate([c, c_ctx[None, :], jnp.zeros((pad, d), F32)], axis=0)
    mod = ada_mod(c_all, w_ada, b_ada)

    cs = rope_table(l, lc)
    cc, sc = channel_dft_tables()
    cl_lat, sl_lat = dft_tables(l, (l * FN_CH) ** -0.5)
    cl_ctx, sl_ctx = dft_tables(lc, (lc * FN_CH) ** -0.5)
    ctx_blk = l // lc

    x_lat, x_lat_off, x_ctx, x_ctx_off = x, 0, ctx, 0
    out = None
    for li in range(depth):
        last = li == depth - 1
        m_lat = [mod[li, :b, k * d:(k + 1) * d].reshape(b, 1, d) for k in range(N_MOD)]
        m_ctx = [mod[li, b:b + 1, k * d:(k + 1) * d].reshape(1, 1, d) for k in range(N_MOD)]

        w_p = prep_w_in(w_in[li])
        proj = proj_in(x_lat, x_lat_off, l // lat_tm, lat_tm, m_lat[0], m_lat[1], w_p, lt, 0, None)
        proj = proj_in(x_ctx, x_ctx_off, 1, CTX_TM, m_ctx[0], m_ctx[1], w_p, lt, l // CTX_TM, proj)

        wq_p, wkv_p = prep_mla_weights(mla_q_norm[li], mla_w_uq[li], mla_kv_norm[li], mla_w_ukv[li])
        q4, k4, v4 = mla_prep(proj, cs, wq_p, wkv_p, CTX_TM)
        y_mla = mla_attend(q4, k4, v4, 0, l // CTX_TM, CTX_TM, lt, 0, None)
        y_fn = fourier_mix(proj, 0, l, cl_lat, sl_lat, cc, sc, None)
        if not last:
            y_mla = mla_attend(q4, k4, v4, l // CTX_TM, lc // CTX_TM, CTX_TM, lc, ctx_blk, y_mla)
            y_fn = fourier_mix(proj, ctx_blk, lc, cl_ctx, sl_ctx, cc, sc, y_fn)
        gla_f, gla_b = gla_scan(proj, gla_w2[li], gla_b2[li], l, lc)
        qkv = dn_conv_norm(proj, dn_conv[li], l, lc)
        dn_f, dn_b = dn_scan(qkv, proj, dn_a_log[li], dn_dt_bias[li], l, lc)

        w_o = w_out[li].astype(BF16)
        rows = l if last else lt
        xn, h = mix_out(dn_f, dn_b, gla_f, gla_b, y_mla, y_fn, proj, 0,
                        x_lat, x_lat_off, l // mix_tm, mix_tm, m_lat[2], m_lat[3], m_lat[4],
                        dn_norm[li], gla_norm[li], w_o, rows, 0, None, None)
        if not last:
            xn, h = mix_out(dn_f, dn_b, gla_f, gla_b, y_mla, y_fn, proj,
                            l // CTX_TM, x_ctx, x_ctx_off, 1, CTX_TM, m_ctx[2], m_ctx[3], m_ctx[4],
                            dn_norm[li], gla_norm[li], w_o, rows, l // CTX_TM, xn, h)

        tn = b * rows
        g2_lat = jnp.broadcast_to(m_lat[5], (b, l // PEER_G2_ROWS, d))
        if last:
            g2_rows = g2_lat.reshape(tn // PEER_G2_ROWS, 1, d)
        else:
            g2_ctx = jnp.broadcast_to(m_ctx[5], (b, lc // PEER_G2_ROWS, d))
            g2_rows = jnp.concatenate([g2_lat, g2_ctx], axis=1).reshape(tn // PEER_G2_ROWS, 1, d)
        hf = h.reshape(tn, d)
        s, aux = peer_scores(hf, peer_wq[li].astype(BF16),
                             peer_keys[li].reshape(2 * PEER_HEADS, PEER_N_KEYS, PEER_HALF).astype(BF16), peer_tm)
        y = peer_experts(hf, s, aux, peer_u[li].astype(BF16), peer_v[li].astype(BF16), xn.reshape(tn, d),
                         g2_rows, final_norm, peer_tm, last)
        out = y.reshape(b, rows, d)
        x_lat, x_lat_off, x_ctx, x_ctx_off = out, 0, out, l // CTX_TM
    return out
```
